```python
import jax, jax.numpy as jnp
from jax import lax
import numpy as np

D_MODEL = 1024
BATCH = 2
SEQ = 8192
DEPTH = 2

N_EVEN = (DEPTH + 1) // 2
N_ODD = DEPTH // 2
RMS_EPS = 1e-6
A_WIDTH = D_MODEL
CONV_WIDTH = 3
B_WIDTH = D_MODEL
B_GROUPS = 8
CHUNK = 128
SGU_LN_EPS = 1e-5
C_WIDTH = D_MODEL
C_HEAD_DIM = 64
C_HEADS = C_WIDTH // C_HEAD_DIM
DECAY_LORA = 64
AAA_LORA = 64
GN_EPS = 64e-5
RWKV_STREAM = 3 * C_WIDTH + DECAY_LORA + AAA_LORA
RWKV_SPLITS = (C_WIDTH, 2 * C_WIDTH, 3 * C_WIDTH, 3 * C_WIDTH + DECAY_LORA)
D_WIDTH = D_MODEL // 2
D_GROUPS = 4
D_GROUP_DIM = D_WIDTH // D_GROUPS
EVEN_PROJ = 4 * A_WIDTH + 3 * B_WIDTH
EVEN_SPLITS = (A_WIDTH, 2 * A_WIDTH, 3 * A_WIDTH, 4 * A_WIDTH,
               4 * A_WIDTH + B_WIDTH, 4 * A_WIDTH + 2 * B_WIDTH)
ODD_PROJ = RWKV_STREAM + C_WIDTH + 2 * D_WIDTH
ODD_SPLITS = (RWKV_STREAM, RWKV_STREAM + C_WIDTH, RWKV_STREAM + C_WIDTH + D_WIDTH)

kernel_name = "hybrid_conv_sgu_rwkv7_fnet_encoder"


def rms_norm(x, g):
    xf = x.astype(jnp.float32)
    y = xf * lax.rsqrt(jnp.mean(xf * xf, axis=-1, keepdims=True) + RMS_EPS)
    return (y * g.astype(jnp.float32)).astype(x.dtype)


def layer_norm(x, g, b, eps):
    xf = x.astype(jnp.float32)
    mu = jnp.mean(xf, axis=-1, keepdims=True)
    var = jnp.mean(jnp.square(xf - mu), axis=-1, keepdims=True)
    y = (xf - mu) * lax.rsqrt(var + eps)
    return (y * g + b).astype(x.dtype)


def shift_prev(p):
    pad = [(0, 0)] * (p.ndim - 2) + [(1, 0), (0, 0)]
    return jnp.pad(p, pad)[..., :-1, :]


def shift_next(p):
    pad = [(0, 0)] * (p.ndim - 2) + [(0, 1), (0, 0)]
    return jnp.pad(p, pad)[..., 1:, :]


def short_conv_branch(h, gate_b, gate_c, conv_w):
    xc = gate_c * h
    y = conv_w[0] * shift_prev(xc) + conv_w[1] * xc + conv_w[2] * shift_next(xc)
    return gate_b * y


def chunked_sgu_branch(u, v, ln_g, ln_b, w_s, b_s):
    bsz, s, c = v.shape
    vn = layer_norm(v, ln_g, ln_b, SGU_LN_EPS)
    vc = vn.reshape(bsz, s // CHUNK, CHUNK, B_GROUPS, c // B_GROUPS)
    mixed = jnp.einsum('gij,bnjgd->bnigd', w_s, vc) + b_s.T[:, :, None]
    return u * mixed.reshape(bsz, s, c)


def wkv7_step(state, inp):
    r, w, k, v, a, b = inp
    sa = jnp.einsum('...ij,...j->...i', state, a)
    state = (state * w[..., None, :] + sa[..., :, None] * b[..., None, :]
             + v[..., :, None] * k[..., None, :])
    y = jnp.einsum('...ij,...j->...i', state, r)
    return state, y


def rwkv7_bidir_branch(p, mu, w0, w2, a0, a2, k_k, k_a, r_k, lnx_g, lnx_b):
    dtype = p.dtype
    bsz, s, _ = p.shape
    pf = p.astype(jnp.float32)
    shifted = jnp.stack([shift_prev(pf), shift_next(pf)])
    q = pf[None] + mu[:, None, None, :] * (shifted - pf[None])
    r, k, v, wd, ad = jnp.split(q, RWKV_SPLITS, axis=-1)
    z_w = w0[:, None, None, :] + jnp.einsum('dbsl,dlc->dbsc', jnp.tanh(wd), w2)
    decay = jnp.exp(-jnp.exp(-jax.nn.softplus(-z_w) - 0.5))
    a = jax.nn.sigmoid(a0[:, None, None, :] + jnp.einsum('dbsl,dlc->dbsc', ad, a2))
    heads = lambda t: t.reshape(t.shape[:3] + (C_HEADS, C_HEAD_DIM))
    kk = heads(k * k_k)
    kk = kk * lax.rsqrt(jnp.maximum(jnp.sum(kk * kk, axis=-1, keepdims=True), 1e-12))
    k = k * (1.0 + (a - 1.0) * k_a)
    r, k, v, decay, a = heads(r), heads(k), heads(v), heads(decay), heads(a)

    def to_scan(t):
        t = jnp.stack([t[0], jnp.flip(t[1], axis=1)])
        return jnp.moveaxis(t, 2, 0)

    init = jnp.zeros((2, bsz, C_HEADS, C_HEAD_DIM, C_HEAD_DIM), jnp.float32)
    xs = (to_scan(r), to_scan(decay), to_scan(k), to_scan(v), to_scan(-kk), to_scan(kk * a))
    _, y = lax.scan(wkv7_step, init, xs)
    y = jnp.moveaxis(y, 0, 2)
    y_sum = y[0] + jnp.flip(y[1], axis=1)
    out = layer_norm(y_sum, lnx_g.reshape(C_HEADS, C_HEAD_DIM),
                     lnx_b.reshape(C_HEADS, C_HEAD_DIM), GN_EPS)
    bonus = jnp.sum(r * k * r_k, axis=-1, keepdims=True) * v
    out = out + bonus[0] + bonus[1]
    return out.reshape(bsz, s, C_WIDTH).astype(dtype)


def fourier_branch(f, w_f):
    bsz, s, c = f.shape
    fg = f.reshape(bsz, s, D_GROUPS, D_GROUP_DIM).astype(jnp.float32)
    spec = jnp.fft.fft2(fg, axes=(1, 3), norm="ortho").real.astype(f.dtype)
    y = jnp.einsum('bsgd,gde->bsge', spec, w_f)
    return y.reshape(bsz, s, c)


def even_layer(h, w_in, conv_w, sgu_ln_g, sgu_ln_b, sgu_w, sgu_b, w_out):
    p = jnp.einsum('bsd,de->bse', h, w_in)
    xa, ba, ca, za, ub, vb, zb = jnp.split(p, EVEN_SPLITS, axis=-1)
    ya = short_conv_branch(xa, ba, ca, conv_w) * jax.nn.silu(za)
    yb = chunked_sgu_branch(ub, vb, sgu_ln_g, sgu_ln_b, sgu_w, sgu_b) * jax.nn.silu(zb)
    return jnp.einsum('bse,ed->bsd', jnp.concatenate([ya, yb], axis=-1), w_out)


def odd_layer(h, w_in, mu, w0, w2, a0, a2, k_k, k_a, r_k, lnx_g, lnx_b, fnet_w, w_out):
    p = jnp.einsum('bsd,de->bse', h, w_in)
    pc, zc, fd, zd = jnp.split(p, ODD_SPLITS, axis=-1)
    yc = rwkv7_bidir_branch(pc, mu, w0, w2, a0, a2, k_k, k_a, r_k, lnx_g, lnx_b) * jax.nn.silu(zc)
    yd = fourier_branch(fd, fnet_w) * jax.nn.silu(zd)
    return jnp.einsum('bse,ed->bsd', jnp.concatenate([yc, yd], axis=-1), w_out)


def setup_inputs(seed: int = 0) -> dict:
    key = jax.random.key(seed)
    ks = iter(jax.random.split(key, 32))
    nrm = lambda shape, scale: scale * jax.random.normal(next(ks), shape, jnp.float32)
    NE, NO = N_EVEN, N_ODD
    return {
        "x": nrm((BATCH, SEQ, D_MODEL), 1.0),
        "e_norm_g": 1.0 + nrm((NE, D_MODEL), 0.02),
        "e_w_in": nrm((NE, D_MODEL, EVEN_PROJ), D_MODEL ** -0.5),
        "e_conv_w": nrm((NE, CONV_WIDTH, A_WIDTH), CONV_WIDTH ** -0.5),
        "e_sgu_ln_g": 1.0 + nrm((NE, B_WIDTH), 0.02),
        "e_sgu_ln_b": nrm((NE, B_WIDTH), 0.02),
        "e_sgu_w": nrm((NE, B_GROUPS, CHUNK, CHUNK), CHUNK ** -0.5),
        "e_sgu_b": 1.0 + nrm((NE, B_GROUPS, CHUNK), 0.01),
        "e_w_out": nrm((NE, A_WIDTH + B_WIDTH, D_MODEL), (A_WIDTH + B_WIDTH) ** -0.5),
        "o_norm_g": 1.0 + nrm((NO, D_MODEL), 0.02),
        "o_w_in": nrm((NO, D_MODEL, ODD_PROJ), D_MODEL ** -0.5),
        "o_mu": jax.random.uniform(next(ks), (NO, 2, RWKV_STREAM), jnp.float32),
        "o_w0": nrm((NO, 2, C_WIDTH), 0.5),
        "o_w2": nrm((NO, 2, DECAY_LORA, C_WIDTH), 0.5 * DECAY_LORA ** -0.5),
        "o_a0": nrm((NO, 2, C_WIDTH), 0.1),
        "o_a2": nrm((NO, 2, AAA_LORA, C_WIDTH), 0.5 * AAA_LORA ** -0.5),
        "o_k_k": 0.85 + nrm((NO, C_WIDTH), 0.05),
        "o_k_a": 1.0 + nrm((NO, C_WIDTH), 0.05),
        "o_r_k": nrm((NO, C_HEADS, C_HEAD_DIM), 0.1),
        "o_lnx_g": 1.0 + nrm((NO, C_WIDTH), 0.02),
        "o_lnx_b": nrm((NO, C_WIDTH), 0.02),
        "o_fnet_w": nrm((NO, D_GROUPS, D_GROUP_DIM, D_GROUP_DIM), D_GROUP_DIM ** -0.5),
        "o_w_out": nrm((NO, C_WIDTH + D_WIDTH, D_MODEL), (C_WIDTH + D_WIDTH) ** -0.5),
        "final_norm_g": 1.0 + nrm((D_MODEL,), 0.02),
    }


def reference(x, e_norm_g, e_w_in, e_conv_w, e_sgu_ln_g, e_sgu_ln_b, e_sgu_w, e_sgu_b, e_w_out,
              o_norm_g, o_w_in, o_mu, o_w0, o_w2, o_a0, o_a2, o_k_k, o_k_a, o_r_k,
              o_lnx_g, o_lnx_b, o_fnet_w, o_w_out, final_norm_g):
    h = x
    for layer in range(DEPTH):
        i = layer // 2
        if layer % 2 == 0:
            h = h + even_layer(rms_norm(h, e_norm_g[i]), e_w_in[i], e_conv_w[i],
                               e_sgu_ln_g[i], e_sgu_ln_b[i], e_sgu_w[i], e_sgu_b[i], e_w_out[i])
        else:
            h = h + odd_layer(rms_norm(h, o_norm_g[i]), o_w_in[i], o_mu[i], o_w0[i], o_w2[i],
                              o_a0[i], o_a2[i], o_k_k[i], o_k_a[i], o_r_k[i],
                              o_lnx_g[i], o_lnx_b[i], o_fnet_w[i], o_w_out[i])
    return rms_norm(h, final_norm_g)
```

```python
import functools
import math

import numpy as np
import jax
import jax.numpy as jnp
from jax import lax
from jax.experimental import pallas as pl
from jax.experimental.pallas import tpu as pltpu

F32 = jnp.float32
BF16 = jnp.bfloat16

RMS_EPS = 1e-6
SGU_LN_EPS = 1e-5
GN_EPS = 64e-5
SGU_CHUNK = 128
SGU_GROUPS = 8
HEAD_DIM = 64
LORA = 64
FNET_GROUPS = 4
FNET_GROUP_DIM = 128

V7X_LANES = 128
V7X_SUBLANES = 8
V7X_VMEM_BYTES = 64 * 1024 * 1024
VMEM_LIMIT = V7X_VMEM_BYTES - 8 * 1024 * 1024


def _cparams(sem):
    return pltpu.CompilerParams(dimension_semantics=sem, vmem_limit_bytes=VMEM_LIMIT)


def _silu(z):
    return z * (1.0 / (1.0 + jnp.exp(-z)))


def _sigmoid(z):
    return 1.0 / (1.0 + jnp.exp(-z))


def _dot(a, b):
    return jnp.dot(a.astype(BF16), b.astype(BF16), preferred_element_type=F32)


def _split(a, n):
    parts = []
    rem = a
    for _ in range(n):
        p = rem.astype(BF16)
        parts.append(p)
        rem = rem - p.astype(F32)
    return parts


def _dot_exact_rhs(a, b_bf16, n=3):
    acc = None
    for p in _split(a, n):
        t = jnp.dot(p, b_bf16, preferred_element_type=F32)
        acc = t if acc is None else acc + t
    return acc


def _proj_kernel(x_ref, g_ref, w_ref, o_ref, hn_ref):
    @pl.when(pl.program_id(1) == 0)
    def _():
        x = x_ref[...]
        ms = jnp.mean(x * x, axis=-1, keepdims=True)
        hn_ref[...] = (x * lax.rsqrt(ms + RMS_EPS) * g_ref[...]).astype(BF16)

    o_ref[...] = jnp.dot(hn_ref[...], w_ref[...], preferred_element_type=F32).astype(o_ref.dtype)


def _norm_proj(x2d, g, w_bf16, tm, tn, out_dtype):
    t, d = x2d.shape
    n = w_bf16.shape[1]
    assert t % tm == 0 and n % tn == 0
    return pl.pallas_call(
        _proj_kernel,
        grid=(t // tm, n // tn),
        in_specs=[
            pl.BlockSpec((tm, d), lambda i, j: (i, 0)),
            pl.BlockSpec((1, d), lambda i, j: (0, 0)),
            pl.BlockSpec((d, tn), lambda i, j: (0, j)),
        ],
        out_specs=pl.BlockSpec((tm, tn), lambda i, j: (i, j)),
        out_shape=jax.ShapeDtypeStruct((t, n), out_dtype),
        scratch_shapes=[pltpu.VMEM((tm, d), BF16)],
        compiler_params=_cparams(("parallel", "arbitrary")),
        name="norm_proj",
    )(x2d, g.reshape(1, d), w_bf16)


def _even_mix_kernel(seq, p_ref, xap_ref, cap_ref, xan_ref, can_ref, h_ref, cw_ref, lng_ref,
                     lnb_ref, sw_ref, sb_ref, wo_ref, o_ref):
    tm = p_ref.shape[0]
    d = h_ref.shape[1]
    row0 = pl.program_id(0) * tm
    at_seq_start = (row0 % seq) == 0
    at_seq_end = ((row0 + tm) % seq) == 0

    xa = p_ref[:, 0 * d:1 * d]
    ba = p_ref[:, 1 * d:2 * d]
    ca = p_ref[:, 2 * d:3 * d]
    za = p_ref[:, 3 * d:4 * d]
    ub = p_ref[:, 4 * d:5 * d]
    vb = p_ref[:, 5 * d:6 * d]
    zb = p_ref[:, 6 * d:7 * d]

    xc = ca * xa
    halo_prev = xap_ref[V7X_SUBLANES - 1:V7X_SUBLANES, :] * cap_ref[V7X_SUBLANES - 1:V7X_SUBLANES, :]
    halo_next = xan_ref[0:1, :] * can_ref[0:1, :]
    halo_prev = jnp.where(at_seq_start, 0.0, halo_prev)
    halo_next = jnp.where(at_seq_end, 0.0, halo_next)
    rows = lax.broadcasted_iota(jnp.int32, (tm, 1), 0)
    prev = jnp.where(rows == 0, halo_prev, pltpu.roll(xc, 1, 0))
    nxt = jnp.where(rows == tm - 1, halo_next, pltpu.roll(xc, tm - 1, 0))
    conv = cw_ref[0:1, :] * prev + cw_ref[1:2, :] * xc + cw_ref[2:3, :] * nxt
    ya = ba * conv * _silu(za)

    mu = jnp.mean(vb, axis=-1, keepdims=True)
    cen = vb - mu
    var = jnp.mean(cen * cen, axis=-1, keepdims=True)
    vn = (cen * lax.rsqrt(var + SGU_LN_EPS) * lng_ref[...] + lnb_ref[...]).astype(BF16)
    gw = d // SGU_GROUPS
    chunk_rows = []
    for n in range(tm // SGU_CHUNK):
        r0 = n * SGU_CHUNK
        cols = []
        for g in range(SGU_GROUPS):
            cols.append(jnp.dot(sw_ref[g], vn[r0:r0 + SGU_CHUNK, g * gw:(g + 1) * gw],
                                preferred_element_type=F32))
        chunk_rows.append(jnp.concatenate(cols, axis=1) + sb_ref[...])
    mixed = jnp.concatenate(chunk_rows, axis=0)
    yb = ub * mixed * _silu(zb)

    out = h_ref[...]
    out = out + jnp.dot(ya.astype(BF16), wo_ref[0:d, :], preferred_element_type=F32)
    out = out + jnp.dot(yb.astype(BF16), wo_ref[d:2 * d, :], preferred_element_type=F32)
    o_ref[...] = out


def _even_mix(p, h2d, seq, conv_w, ln_g, ln_b, sgu_w_bf16, sgu_bias_full, w_out_bf16, tm):
    t, d = h2d.shape
    nblk8 = t // V7X_SUBLANES
    r8 = tm // V7X_SUBLANES
    prev_map = lambda c: (lambda i: (jnp.maximum(i * r8 - 1, 0), c))
    next_map = lambda c: (lambda i: (jnp.minimum((i + 1) * r8, nblk8 - 1), c))
    full = lambda shape: pl.BlockSpec(shape, lambda i: (0,) * len(shape))
    return pl.pallas_call(
        functools.partial(_even_mix_kernel, seq),
        grid=(t // tm,),
        in_specs=[
            pl.BlockSpec((tm, 7 * d), lambda i: (i, 0)),
            pl.BlockSpec((V7X_SUBLANES, d), prev_map(0)),
            pl.BlockSpec((V7X_SUBLANES, d), prev_map(2)),
            pl.BlockSpec((V7X_SUBLANES, d), next_map(0)),
            pl.BlockSpec((V7X_SUBLANES, d), next_map(2)),
            pl.BlockSpec((tm, d), lambda i: (i, 0)),
            full((3, d)),
            full((1, d)),
            full((1, d)),
            full(sgu_w_bf16.shape),
            full(sgu_bias_full.shape),
            full(w_out_bf16.shape),
        ],
        out_specs=pl.BlockSpec((tm, d), lambda i: (i, 0)),
        out_shape=jax.ShapeDtypeStruct((t, d), F32),
        compiler_params=_cparams(("parallel",)),
        name="even_mix",
    )(p, p, p, p, p, h2d, conv_w, ln_g.reshape(1, d), ln_b.reshape(1, d), sgu_w_bf16,
      sgu_bias_full, w_out_bf16)


def _even_layer(h2d, seq, norm_g, w_in, conv_w, ln_g, ln_b, sgu_w, sgu_b, w_out):
    t, d = h2d.shape
    p = _norm_proj(h2d, norm_g, w_in.astype(BF16), tm=min(1024, t), tn=1024, out_dtype=F32)
    bias_full = jnp.repeat(sgu_b.T, d // SGU_GROUPS, axis=1)
    return _even_mix(p, h2d, seq, conv_w, ln_g, ln_b, sgu_w.astype(BF16), bias_full,
                     w_out.astype(BF16), tm=256)


def _dot_exact_lhs(a_bf16, b, n=3):
    acc = None
    for p in _split(b, n):
        t = jnp.dot(a_bf16, p, preferred_element_type=F32)
        acc = t if acc is None else acc + t
    return acc


def _dot3(a, b):
    ah, al = _split(a, 2)
    bh, bl = _split(b, 2)
    d = lambda x, y: jnp.dot(x, y, preferred_element_type=F32)
    return d(ah, bh) + (d(ah, bl) + d(al, bh))


_NN = (((2,), (1,)), ((0,), (0,)))
_NT = (((2,), (2,)), ((0,), (0,)))


def _bmm(a, b, dims, passes):
    d = lambda x, y: lax.dot_general(x, y, dims, preferred_element_type=F32)
    if passes == 1:
        return d(a.astype(BF16), b.astype(BF16))
    ah, al = _split(a, 2)
    bh, bl = _split(b, 2)
    return d(ah, bh) + (d(ah, bl) + d(al, bh))


def _segsum(x, bd_bf16, n=3):
    w = bd_bf16.shape[0]
    cols = [_dot_exact_rhs(x[:, j:j + w], bd_bf16, n) for j in range(0, x.shape[1], w)]
    return cols[0] if len(cols) == 1 else jnp.concatenate(cols, axis=1)


WKV_CHUNK = 64
WKV_PASSES = 3


def _wkv_kernel(direction, r_ref, k_ref, v_ref, wa_ref, mur_ref, muk_ref, muv_ref, muwa_ref,
                w0_ref, a0_ref, kk_ref, ka_ref, rk_ref, lora_ref, tri_ref, bd_ref,
                y_ref, bon_ref, h_ref, cr_ref, ck_ref, cv_ref, cwa_ref):
    rws, gw = r_ref.shape
    L = WKV_CHUNK
    nch = rws // L
    nheads = gw // HEAD_DIM
    P = WKV_PASSES

    @pl.when(pl.program_id(2) == 0)
    def _():
        h_ref[...] = jnp.zeros_like(h_ref)
        cr_ref[...] = jnp.zeros_like(cr_ref)
        ck_ref[...] = jnp.zeros_like(ck_ref)
        cv_ref[...] = jnp.zeros_like(cv_ref)
        cwa_ref[...] = jnp.zeros_like(cwa_ref)

    rows = lax.broadcasted_iota(jnp.int32, (rws, 1), 0)

    def token_shift(x, carry_ref, mu):
        if direction == 0:
            sh = jnp.where(rows == 0, carry_ref[...], pltpu.roll(x, 1, 0))
            carry_ref[...] = x[rws - 1:rws, :]
        else:
            sh = jnp.where(rows == rws - 1, carry_ref[...], pltpu.roll(x, rws - 1, 0))
            carry_ref[...] = x[0:1, :]
        return x + mu * (sh - x)

    r = token_shift(r_ref[...], cr_ref, mur_ref[...])
    k = token_shift(k_ref[...], ck_ref, muk_ref[...])
    v = token_shift(v_ref[...], cv_ref, muv_ref[...])
    wa = token_shift(wa_ref[...], cwa_ref, muwa_ref[...])

    lane_wa = lax.broadcasted_iota(jnp.int32, (1, wa.shape[1]), 1)
    wa = jnp.where(lane_wa < LORA, jnp.tanh(wa), wa)
    za = _dot(wa, lora_ref[...])
    zw = w0_ref[...] + za[:, :gw]
    a = _sigmoid(a0_ref[...] + za[:, gw:])
    lw = -math.exp(-0.5) * _sigmoid(zw)

    bd = bd_ref[...]
    kk = k * kk_ref[...]
    kk = kk * lax.rsqrt(jnp.maximum(_segsum(kk * kk, bd), 1e-12))
    k2 = k * (1.0 + (a - 1.0) * ka_ref[...])
    bon_ref[...] = _segsum(r * k2 * rk_ref[...], bd) * v

    cs = _dot_exact_lhs(tri_ref[...], lw, 3)
    e_inc = jnp.exp(cs)
    e_inv = jnp.exp(-cs)
    e_exc = jnp.exp(cs - lw)
    to3 = lambda x: x.reshape(nch, L, gw)
    rt = to3(r * e_inc)
    kt = to3(k2 * e_inv)
    at = to3(-kk * e_exc)
    bt = to3(kk * a * e_inv)
    v3 = to3(v)
    last = L - 1 if direction == 0 else 0
    e_end = to3(e_inc)[:, last:last + 1, :]
    kh = kt * e_end
    bh = bt * e_end

    lane = lax.broadcasted_iota(jnp.int32, (1, 1, gw), 2)
    head_masks = [(lane // HEAD_DIM) == h for h in range(nheads)]

    def blockdiag(x3):
        return jnp.concatenate([jnp.where(m, x3, 0.0) for m in head_masks], axis=1)

    def hmm(lp, xp):
        return _bmm(lp, blockdiag(xp), _NN, P)

    x_ar = jnp.concatenate([at, rt], axis=1)
    g_b = _bmm(x_ar, blockdiag(bt), _NT, P)
    g_k = _bmm(x_ar, blockdiag(kt), _NT, P)

    t_idx = lax.broadcasted_iota(jnp.int32, (1, L, gw), 1)
    s_idx = lax.broadcasted_iota(jnp.int32, (1, L, gw), 2) % L
    if direction == 0:
        strict, incl = s_idx < t_idx, s_idx <= t_idx
    else:
        strict, incl = s_idx > t_idx, s_idx >= t_idx
    a_ab = jnp.where(strict, g_b[:, :L], 0.0)
    a_rb = jnp.where(incl, g_b[:, L:], 0.0)
    a_ak = jnp.where(strict, g_k[:, :L], 0.0)
    a_rk = jnp.where(incl, g_k[:, L:], 0.0)

    tinv = jnp.where(s_idx == t_idx, 1.0, 0.0) + a_ab
    apow = a_ab
    for _ in range(int(math.log2(L)) - 1):
        apow = hmm(apow, apow)
        tinv = tinv + hmm(apow, tinv)

    wt = hmm(tinv, at)
    u0 = hmm(tinv, hmm(a_ak, v3))
    rh = rt + hmm(a_rb, wt)
    y0 = hmm(a_rb, u0) + hmm(a_rk, v3)

    bh_t = jnp.swapaxes(bh, 1, 2)
    kh_t = jnp.swapaxes(kh, 1, 2)
    ri = lax.broadcasted_iota(jnp.int32, (1, gw, gw), 1)
    ci = lax.broadcasted_iota(jnp.int32, (1, gw, gw), 2)
    same_head = (ri // HEAD_DIM) == (ci // HEAD_DIM)
    m_mat = jnp.where(same_head, _bmm(bh_t, wt, _NN, P), 0.0)
    m_mat = m_mat + jnp.where(ri == ci, e_end, 0.0)
    n_mat = jnp.where(same_head, _bmm(bh_t, u0, _NN, P) + _bmm(kh_t, v3, _NN, P), 0.0)

    h = h_ref[...]
    ys = [None] * nch
    order = range(nch) if direction == 0 else range(nch - 1, -1, -1)
    for c in order:
        ys[c] = _dot3(rh[c], h) + y0[c]
        h = _dot3(m_mat[c], h) + n_mat[c]
    h_ref[...] = h
    y_ref[...] = jnp.concatenate(ys, axis=0)


def _wkv(p3, direction, col0, mu, w0, a0, lora_w, k_k, k_a, r_k, gw, rws):
    bsz, seq, _ = p3.shape
    c = w0.shape[-1]
    ng = c // gw
    nstep = seq // rws
    L = WKV_CHUNK
    t = np.arange(rws)
    same = (t[:, None] // L) == (t[None, :] // L)
    tri = same & ((t[None, :] <= t[:, None]) if direction == 0 else (t[None, :] >= t[:, None]))
    hd = np.arange(gw) // HEAD_DIM
    bd = hd[:, None] == hd[None, :]
    tri = jnp.asarray(tri, BF16)
    bd = jnp.asarray(bd, BF16)

    rb = (lambda s: s) if direction == 0 else (lambda s: nstep - 1 - s)
    stream = lambda off: pl.BlockSpec((None, rws, gw), lambda b, g, s: (b, rb(s), off // gw + g))
    vec = pl.BlockSpec((1, gw), lambda b, g, s: (0, g))
    full2 = lambda shape: pl.BlockSpec(shape, lambda b, g, s: (0, 0))
    wa_w = 2 * LORA
    out_spec = pl.BlockSpec((None, rws, gw), lambda b, g, s: (b, rb(s), g))
    row = lambda x: x.reshape(1, -1)
    mu_d = mu[direction]
    return pl.pallas_call(
        functools.partial(_wkv_kernel, direction),
        grid=(bsz, ng, nstep),
        in_specs=[
            stream(col0), stream(col0 + c), stream(col0 + 2 * c),
            pl.BlockSpec((None, rws, wa_w), lambda b, g, s: (b, rb(s), (col0 + 3 * c) // wa_w)),
            vec, vec, vec, full2((1, wa_w)),
            vec, vec, vec, vec, vec,
            pl.BlockSpec((None, wa_w, 2 * gw), lambda b, g, s: (g, 0, 0)),
            full2((rws, rws)), full2((gw, gw)),
        ],
        out_specs=[out_spec, out_spec],
        out_shape=[jax.ShapeDtypeStruct((bsz, seq, c), F32)] * 2,
        scratch_shapes=[pltpu.VMEM((gw, gw), F32), pltpu.VMEM((1, gw), F32),
                        pltpu.VMEM((1, gw), F32), pltpu.VMEM((1, gw), F32),
                        pltpu.VMEM((1, wa_w), F32)],
        compiler_params=_cparams(("parallel", "parallel", "arbitrary")),
        name="wkv_fwd" if direction == 0 else "wkv_bwd",
    )(p3, p3, p3, p3, row(mu_d[:c]), row(mu_d[c:2 * c]), row(mu_d[2 * c:3 * c]), row(mu_d[3 * c:]),
      row(w0[direction]), row(a0[direction]), row(k_k), row(k_a), row(r_k),
      lora_w[direction], tri, bd)


def _lora_weights(w2, a2, gw):
    nd, lo, c = w2.shape
    ng = c // gw
    w2g = w2.reshape(nd, lo, ng, gw).transpose(0, 2, 1, 3)
    a2g = a2.reshape(nd, lo, ng, gw).transpose(0, 2, 1, 3)
    z = jnp.zeros_like(w2g)
    top = jnp.concatenate([w2g, z], axis=3)
    bot = jnp.concatenate([z, a2g], axis=3)
    return jnp.concatenate([top, bot], axis=2).astype(BF16)


FNET_N2 = 128


def _cos_sin(n, m=None, period=None):
    i = np.arange(n)[:, None]
    j = np.arange(n if m is None else m)[None, :]
    ang = 2.0 * np.pi * ((i * j) % (period or n)) / (period or n)
    return np.cos(ang), np.sin(ang)


def _fnet_weight_kernel(scale, c_ref, s_ref, w_ref, o_ref):
    hi = lax.Precision.HIGHEST
    for g in range(w_ref.shape[0]):
        wc = jnp.dot(c_ref[...], w_ref[g], precision=hi, preferred_element_type=F32)
        ws = jnp.dot(s_ref[...], w_ref[g], precision=hi, preferred_element_type=F32)
        o_ref[g] = jnp.concatenate([wc, -ws], axis=1) * scale


def _fnet_chan_kernel(f_ref, w_ref, gr_ref, gi_ref):
    gd = w_ref.shape[1]
    re, im = [], []
    for g in range(w_ref.shape[0]):
        z = _dot3(f_ref[:, g * gd:(g + 1) * gd], w_ref[g])
        re.append(z[:, :gd])
        im.append(z[:, gd:])
    gr_ref[...] = jnp.concatenate(re, axis=1)
    gi_ref[...] = jnp.concatenate(im, axis=1)


def _fnet_stage1_kernel(nchan, gr_ref, gi_ref, m_ref, twc_ref, tws_ref, zr_ref, zi_ref):
    n1 = gr_ref.shape[0]
    z = _dot3(m_ref[...], jnp.concatenate([gr_ref[...], gi_ref[...]], axis=0))
    zr, zi = z[:n1], z[n1:]
    q = gr_ref.shape[1] // nchan
    shape = (n1, nchan)
    tc = jnp.concatenate([jnp.broadcast_to(twc_ref[:, j:j + 1], shape) for j in range(q)], axis=1)
    ts = jnp.concatenate([jnp.broadcast_to(tws_ref[:, j:j + 1], shape) for j in range(q)], axis=1)
    zr_ref[...] = zr * tc + zi * ts
    zi_ref[...] = zi * tc - zr * ts


def _fnet_stage2_kernel(zr_ref, zi_ref, m_ref, o_ref):
    o_ref[...] = _dot3(m_ref[...], jnp.concatenate([zr_ref[...], zi_ref[...]], axis=0))


def _fnet(p2d, col_blk, bsz, seq, w_f, tm):
    t = p2d.shape[0]
    ng, gd, _ = w_f.shape
    c = ng * gd
    n2 = FNET_N2
    n1 = seq // n2
    cb = 4 * c
    scale = 1.0 / math.sqrt(seq * gd)
    cd, sd = _cos_sin(gd)
    c1, s1 = _cos_sin(n1)
    c2, s2 = _cos_sin(n2)
    twc, tws = _cos_sin(n1, n2, seq)
    m1 = np.block([[c1, s1], [-s1, c1]])
    m2 = np.concatenate([c2, s2], axis=1)
    to_blocks = lambda tw: jnp.asarray(tw.reshape(n1, n2 * c // cb, cb // c).transpose(1, 0, 2), F32)
    const = lambda a: jnp.asarray(a, F32)

    wcat = pl.pallas_call(
        functools.partial(_fnet_weight_kernel, scale),
        out_shape=jax.ShapeDtypeStruct((ng, gd, 2 * gd), F32),
        name="fnet_weights",
    )(const(cd), const(sd), w_f)

    gr, gi = pl.pallas_call(
        _fnet_chan_kernel,
        grid=(t // tm,),
        in_specs=[pl.BlockSpec((tm, c), lambda i: (i, col_blk)),
                  pl.BlockSpec((ng, gd, 2 * gd), lambda i: (0, 0, 0))],
        out_specs=[pl.BlockSpec((tm, c), lambda i: (i, 0))] * 2,
        out_shape=[jax.ShapeDtypeStruct((t, c), F32)] * 2,
        compiler_params=_cparams(("parallel",)),
        name="fnet_chan",
    )(p2d, wcat)

    gr = gr.reshape(bsz, n1, n2 * c)
    gi = gi.reshape(bsz, n1, n2 * c)
    blk = pl.BlockSpec((None, n1, cb), lambda b, j: (b, 0, j))
    twb = pl.BlockSpec((None, n1, cb // c), lambda b, j: (j, 0, 0))
    zr, zi = pl.pallas_call(
        functools.partial(_fnet_stage1_kernel, c),
        grid=(bsz, n2 * c // cb),
        in_specs=[blk, blk, pl.BlockSpec((2 * n1, 2 * n1), lambda b, j: (0, 0)), twb, twb],
        out_specs=[blk, blk],
        out_shape=[jax.ShapeDtypeStruct((bsz, n1, n2 * c), F32)] * 2,
        compiler_params=_cparams(("parallel", "parallel")),
        name="fnet_stage1",
    )(gr, gi, const(m1), to_blocks(twc), to_blocks(tws))

    zr = zr.reshape(bsz, seq, c)
    zi = zi.reshape(bsz, seq, c)
    blk2 = pl.BlockSpec((None, n2, c), lambda b, j: (b, j, 0))
    out = pl.pallas_call(
        _fnet_stage2_kernel,
        grid=(bsz, n1),
        in_specs=[blk2, blk2, pl.BlockSpec((n2, 2 * n2), lambda b, j: (0, 0))],
        out_specs=pl.BlockSpec((None, n2, c), lambda b, j: (b, 0, j)),
        out_shape=jax.ShapeDtypeStruct((bsz, n2, n1 * c), F32),
        compiler_params=_cparams(("parallel", "parallel")),
        name="fnet_stage2",
    )(zr, zi, const(m2))
    return out.reshape(t, c)


def _odd_out_kernel(y0_ref, y1_ref, b0_ref, b1_ref, zc_ref, fn_ref, zd_ref, h_ref, lng_ref,
                    lnb_ref, bd_ref, wo_ref, fg_ref, o_ref):
    c = y0_ref.shape[1]
    bd = bd_ref[...]
    inv_n = 1.0 / HEAD_DIM
    ysum = y0_ref[...] + y1_ref[...]
    mean = _segsum(ysum, bd) * inv_n
    cen = ysum - mean
    var = _segsum(cen * cen, bd) * inv_n
    gn = cen * lax.rsqrt(var + GN_EPS) * lng_ref[...] + lnb_ref[...]
    yc = (gn + b0_ref[...] + b1_ref[...]) * _silu(zc_ref[...])
    yd = fn_ref[...] * _silu(zd_ref[...])
    out = h_ref[...]
    out = out + jnp.dot(yc.astype(BF16), wo_ref[0:c, :], preferred_element_type=F32)
    out = out + jnp.dot(yd.astype(BF16), wo_ref[c:, :], preferred_element_type=F32)
    ms = jnp.mean(out * out, axis=-1, keepdims=True)
    o_ref[...] = out * lax.rsqrt(ms + RMS_EPS) * fg_ref[...]


def _odd_out(y0, y1, b0, b1, p2d, zc_blk, fn, zd_blk, h2d, lnx_g, lnx_b, w_out_bf16, final_g, tm):
    t, d = h2d.shape
    c = y0.shape[1]
    cf = fn.shape[1]
    hd = np.arange(2 * HEAD_DIM) // HEAD_DIM
    bd = jnp.asarray(hd[:, None] == hd[None, :], BF16)
    rowblk = lambda w: pl.BlockSpec((tm, w), lambda i: (i, 0))
    full = lambda shape: pl.BlockSpec(shape, lambda i: (0, 0))
    return pl.pallas_call(
        _odd_out_kernel,
        grid=(t // tm,),
        in_specs=[rowblk(c), rowblk(c), rowblk(c), rowblk(c),
                  pl.BlockSpec((tm, c), lambda i: (i, zc_blk)),
                  rowblk(cf),
                  pl.BlockSpec((tm, cf), lambda i: (i, zd_blk)),
                  rowblk(d), full((1, c)), full((1, c)), full(bd.shape),
                  full(w_out_bf16.shape), full((1, d))],
        out_specs=rowblk(d),
        out_shape=jax.ShapeDtypeStruct((t, d), F32),
        compiler_params=_cparams(("parallel",)),
        name="odd_out",
    )(y0, y1, b0, b1, p2d, fn, p2d, h2d, lnx_g.reshape(1, c), lnx_b.reshape(1, c), bd,
      w_out_bf16, final_g.reshape(1, d))


WKV_LANES = 128
WKV_ROWS = 256


def _odd_layer(h2d, bsz, seq, norm_g, w_in, mu, w0, w2, a0, a2, k_k, k_a, r_k, lnx_g, lnx_b,
               fnet_w, w_out, final_g):
    t, d = h2d.shape
    c = w0.shape[-1]
    rs = 3 * c + 2 * LORA
    cf = fnet_w.shape[0] * fnet_w.shape[1]
    w_perm = jnp.concatenate([w_in[:, rs:], w_in[:, :rs]], axis=1).astype(BF16)
    p2 = _norm_proj(h2d, norm_g, w_perm, tm=min(256, t), tn=w_perm.shape[1], out_dtype=F32)
    col0 = c + 2 * cf
    p3 = p2.reshape(bsz, seq, -1)
    lora = _lora_weights(w2, a2, WKV_LANES)
    rws = min(WKV_ROWS, seq)
    y0, b0 = _wkv(p3, 0, col0, mu, w0, a0, lora, k_k, k_a, r_k.reshape(-1), WKV_LANES, rws)
    y1, b1 = _wkv(p3, 1, col0, mu, w0, a0, lora, k_k, k_a, r_k.reshape(-1), WKV_LANES, rws)
    fn = _fnet(p2, c // cf, bsz, seq, fnet_w, tm=min(512, t))
    flat = lambda a: a.reshape(t, c)
    return _odd_out(flat(y0), flat(y1), flat(b0), flat(b1), p2, 0, fn, c // cf + 1, h2d,
                    lnx_g, lnx_b, w_out.astype(BF16), final_g, tm=min(256, t))


def kernel(x, e_norm_g, e_w_in, e_conv_w, e_sgu_ln_g, e_sgu_ln_b, e_sgu_w, e_sgu_b, e_w_out,
           o_norm_g, o_w_in, o_mu, o_w0, o_w2, o_a0, o_a2, o_k_k, o_k_a, o_r_k, o_lnx_g, o_lnx_b,
           o_fnet_w, o_w_out, final_norm_g):
    bsz, seq, d = x.shape
    assert e_norm_g.shape[0] == 1 and o_norm_g.shape[0] == 1, "two-layer trunk: one even, one odd layer"
    h = x.reshape(bsz * seq, d)
    h = _even_layer(h, seq, e_norm_g[0], e_w_in[0], e_conv_w[0], e_sgu_ln_g[0], e_sgu_ln_b[0],
                    e_sgu_w[0], e_sgu_b[0], e_w_out[0])
    out = _odd_layer(h, bsz, seq, o_norm_g[0], o_w_in[0], o_mu[0], o_w0[0], o_w2[0], o_a0[0],
                     o_a2[0], o_k_k[0], o_k_a[0], o_r_k[0], o_lnx_g[0], o_lnx_b[0], o_fnet_w[0],
                     o_w_out[0], final_norm_g)
    return out.reshape(bsz, seq, d)
```

```python
import functools
import math

import numpy as np
import jax
import jax.numpy as jnp
from jax import lax
from jax.experimental import pallas as pl
from jax.experimental.pallas import tpu as pltpu

F32 = jnp.float32
BF16 = jnp.bfloat16

RMS_EPS = 1e-6
SGU_LN_EPS = 1e-5
GN_EPS = 64e-5
SGU_CHUNK = 128
SGU_GROUPS = 8
HEAD_DIM = 64
LORA = 64
FNET_GROUPS = 4
FNET_GROUP_DIM = 128

V7X_LANES = 128
V7X_SUBLANES = 8
V7X_VMEM_BYTES = 64 * 1024 * 1024
VMEM_LIMIT = V7X_VMEM_BYTES - 8 * 1024 * 1024


def _cparams(sem):
    return pltpu.CompilerParams(dimension_semantics=sem, vmem_limit_bytes=VMEM_LIMIT)


def _silu(z):
    return z * (1.0 / (1.0 + jnp.exp(-z)))


def _sigmoid(z):
    return 1.0 / (1.0 + jnp.exp(-z))


def _dot(a, b):
    return jnp.dot(a.astype(BF16), b.astype(BF16), preferred_element_type=F32)


def _split(a, n):
    parts = []
    rem = a
    for _ in range(n):
        p = rem.astype(BF16)
        parts.append(p)
        rem = rem - p.astype(F32)
    return parts


def _dot_exact_rhs(a, b_bf16, n=3):
    acc = None
    for p in _split(a, n):
        t = jnp.dot(p, b_bf16, preferred_element_type=F32)
        acc = t if acc is None else acc + t
    return acc


def _proj_kernel(x_ref, g_ref, w_ref, o_ref, hn_ref):
    @pl.when(pl.program_id(1) == 0)
    def _():
        x = x_ref[...]
        ms = jnp.mean(x * x, axis=-1, keepdims=True)
        hn_ref[...] = (x * lax.rsqrt(ms + RMS_EPS) * g_ref[...]).astype(BF16)

    o_ref[...] = jnp.dot(hn_ref[...], w_ref[...], preferred_element_type=F32).astype(o_ref.dtype)


def _norm_proj(x2d, g, w_bf16, tm, tn, out_dtype):
    t, d = x2d.shape
    n = w_bf16.shape[1]
    assert t % tm == 0 and n % tn == 0
    return pl.pallas_call(
        _proj_kernel,
        grid=(t // tm, n // tn),
        in_specs=[
            pl.BlockSpec((tm, d), lambda i, j: (i, 0)),
            pl.BlockSpec((1, d), lambda i, j: (0, 0)),
            pl.BlockSpec((d, tn), lambda i, j: (0, j)),
        ],
        out_specs=pl.BlockSpec((tm, tn), lambda i, j: (i, j)),
        out_shape=jax.ShapeDtypeStruct((t, n), out_dtype),
        scratch_shapes=[pltpu.VMEM((tm, d), BF16)],
        compiler_params=_cparams(("parallel", "arbitrary")),
        name="norm_proj",
    )(x2d, g.reshape(1, d), w_bf16)


def _even_mix_kernel(seq, p_ref, xap_ref, cap_ref, xan_ref, can_ref, h_ref, cw_ref, lng_ref,
                     lnb_ref, sw_ref, sb_ref, wo_ref, o_ref):
    tm = p_ref.shape[0]
    d = h_ref.shape[1]
    row0 = pl.program_id(0) * tm
    at_seq_start = (row0 % seq) == 0
    at_seq_end = ((row0 + tm) % seq) == 0

    xa = p_ref[:, 0 * d:1 * d]
    ba = p_ref[:, 1 * d:2 * d]
    ca = p_ref[:, 2 * d:3 * d]
    za = p_ref[:, 3 * d:4 * d]
    ub = p_ref[:, 4 * d:5 * d]
    vb = p_ref[:, 5 * d:6 * d]
    zb = p_ref[:, 6 * d:7 * d]

    xc = ca * xa
    halo_prev = xap_ref[V7X_SUBLANES - 1:V7X_SUBLANES, :] * cap_ref[V7X_SUBLANES - 1:V7X_SUBLANES, :]
    halo_next = xan_ref[0:1, :] * can_ref[0:1, :]
    halo_prev = jnp.where(at_seq_start, 0.0, halo_prev)
    halo_next = jnp.where(at_seq_end, 0.0, halo_next)
    rows = lax.broadcasted_iota(jnp.int32, (tm, 1), 0)
    prev = jnp.where(rows == 0, halo_prev, pltpu.roll(xc, 1, 0))
    nxt = jnp.where(rows == tm - 1, halo_next, pltpu.roll(xc, tm - 1, 0))
    conv = cw_ref[0:1, :] * prev + cw_ref[1:2, :] * xc + cw_ref[2:3, :] * nxt
    ya = ba * conv * _silu(za)

    mu = jnp.mean(vb, axis=-1, keepdims=True)
    cen = vb - mu
    var = jnp.mean(cen * cen, axis=-1, keepdims=True)
    vn = (cen * lax.rsqrt(var + SGU_LN_EPS) * lng_ref[...] + lnb_ref[...]).astype(BF16)
    gw = d // SGU_GROUPS
    chunk_rows = []
    for n in range(tm // SGU_CHUNK):
        r0 = n * SGU_CHUNK
        cols = []
        for g in range(SGU_GROUPS):
            cols.append(jnp.dot(sw_ref[g], vn[r0:r0 + SGU_CHUNK, g * gw:(g + 1) * gw],
                                preferred_element_type=F32))
        chunk_rows.append(jnp.concatenate(cols, axis=1) + sb_ref[...])
    mixed = jnp.concatenate(chunk_rows, axis=0)
    yb = ub * mixed * _silu(zb)

    out = h_ref[...]
    out = out + jnp.dot(ya.astype(BF16), wo_ref[0:d, :], preferred_element_type=F32)
    out = out + jnp.dot(yb.astype(BF16), wo_ref[d:2 * d, :], preferred_element_type=F32)
    o_ref[...] = out


def _even_mix(p, h2d, seq, conv_w, ln_g, ln_b, sgu_w_bf16, sgu_bias_full, w_out_bf16, tm):
    t, d = h2d.shape
    nblk8 = t // V7X_SUBLANES
    r8 = tm // V7X_SUBLANES
    prev_map = lambda c: (lambda i: (jnp.maximum(i * r8 - 1, 0), c))
    next_map = lambda c: (lambda i: (jnp.minimum((i + 1) * r8, nblk8 - 1), c))
    full = lambda shape: pl.BlockSpec(shape, lambda i: (0,) * len(shape))
    return pl.pallas_call(
        functools.partial(_even_mix_kernel, seq),
        grid=(t // tm,),
        in_specs=[
            pl.BlockSpec((tm, 7 * d), lambda i: (i, 0)),
            pl.BlockSpec((V7X_SUBLANES, d), prev_map(0)),
            pl.BlockSpec((V7X_SUBLANES, d), prev_map(2)),
            pl.BlockSpec((V7X_SUBLANES, d), next_map(0)),
            pl.BlockSpec((V7X_SUBLANES, d), next_map(2)),
            pl.BlockSpec((tm, d), lambda i: (i, 0)),
            full((3, d)),
            full((1, d)),
            full((1, d)),
            full(sgu_w_bf16.shape),
            full(sgu_bias_full.shape),
            full(w_out_bf16.shape),
        ],
        out_specs=pl.BlockSpec((tm, d), lambda i: (i, 0)),
        out_shape=jax.ShapeDtypeStruct((t, d), F32),
        compiler_params=_cparams(("parallel",)),
        name="even_mix",
    )(p, p, p, p, p, h2d, conv_w, ln_g.reshape(1, d), ln_b.reshape(1, d), sgu_w_bf16,
      sgu_bias_full, w_out_bf16)


def _even_layer(h2d, seq, norm_g, w_in, conv_w, ln_g, ln_b, sgu_w, sgu_b, w_out):
    t, d = h2d.shape
    p = _norm_proj(h2d, norm_g, w_in.astype(BF16), tm=min(1024, t), tn=1024, out_dtype=F32)
    bias_full = jnp.repeat(sgu_b.T, d // SGU_GROUPS, axis=1)
    return _even_mix(p, h2d, seq, conv_w, ln_g, ln_b, sgu_w.astype(BF16), bias_full,
                     w_out.astype(BF16), tm=256)


def _dot_exact_lhs(a_bf16, b, n=3):
    acc = None
    for p in _split(b, n):
        t = jnp.dot(a_bf16, p, preferred_element_type=F32)
        acc = t if acc is None else acc + t
    return acc


def _dot3(a, b):
    ah, al = _split(a, 2)
    bh, bl = _split(b, 2)
    d = lambda x, y: jnp.dot(x, y, preferred_element_type=F32)
    return d(ah, bh) + (d(ah, bl) + d(al, bh))


_NN = (((2,), (1,)), ((0,), (0,)))
_NT = (((2,), (2,)), ((0,), (0,)))


def _bmm(a, b, dims, passes):
    d = lambda x, y: lax.dot_general(x, y, dims, preferred_element_type=F32)
    if passes == 1:
        return d(a.astype(BF16), b.astype(BF16))
    ah, al = _split(a, 2)
    bh, bl = _split(b, 2)
    return d(ah, bh) + (d(ah, bl) + d(al, bh))


def _segsum(x, bd_bf16, n=3):
    w = bd_bf16.shape[0]
    cols = [_dot_exact_rhs(x[:, j:j + w], bd_bf16, n) for j in range(0, x.shape[1], w)]
    return cols[0] if len(cols) == 1 else jnp.concatenate(cols, axis=1)


WKV_CHUNK = 64


def _wkv_kernel(direction, r_ref, k_ref, v_ref, wa_ref, mur_ref, muk_ref, muv_ref, muwa_ref,
                w0_ref, a0_ref, kk_ref, ka_ref, rk_ref, lora_ref, tri_ref, bd_ref,
                y_ref, bon_ref, h_ref, cr_ref, ck_ref, cv_ref, cwa_ref,
                lhs_ref, n_ref, y0_ref, bonp_ref):
    rws, gw = r_ref.shape
    L = WKV_CHUNK
    nch = rws // L
    nheads = gw // HEAD_DIM

    @pl.when(pl.program_id(2) == 0)
    def _():
        for ref in (h_ref, cr_ref, ck_ref, cv_ref, cwa_ref, lhs_ref, n_ref, y0_ref, bonp_ref):
            ref[...] = jnp.zeros_like(ref)

    def recurrence():
        h = h_ref[...]
        ys = [None] * nch
        for c in (range(nch) if direction == 0 else range(nch - 1, -1, -1)):
            both = jnp.dot(lhs_ref[c], h.astype(BF16), preferred_element_type=F32)
            ys[c] = both[:L] + y0_ref[c]
            h = both[L:] + n_ref[c]
            yield
        h_ref[...] = h
        y_ref[...] = jnp.concatenate(ys, axis=0)

    chain = recurrence()
    bon_ref[...] = bonp_ref[...]

    streams = ((r_ref, cr_ref), (k_ref, ck_ref), (v_ref, cv_ref), (wa_ref, cwa_ref))
    edge = [c_ref[...] for _, c_ref in streams]
    far = rws - 1 if direction == 0 else 0
    for x_ref, c_ref in streams:
        c_ref[...] = x_ref[far:far + 1, :]

    ncg = min(WKV_GROUP, nch)
    groups = [_wkv_local_stages(direction, g0, ncg, edge, r_ref, k_ref, v_ref, wa_ref, mur_ref,
                                muk_ref, muv_ref, muwa_ref, w0_ref, a0_ref, kk_ref, ka_ref, rk_ref,
                                lora_ref, tri_ref, bd_ref, lhs_ref, n_ref, y0_ref, bonp_ref, chain)
              for g0 in range(0, nch, ncg)]
    slot = 0
    while groups:
        for gi, gen in enumerate(list(groups)):
            if slot >= gi * WKV_SKEW and next(gen, "done") == "done":
                groups.remove(gen)
        next(chain, None)
        slot += 1
    for _ in chain:
        pass


WKV_GROUP = 4
WKV_SKEW = 1


def _wkv_local_stages(direction, c0, ncg, edge, r_ref, k_ref, v_ref, wa_ref, mur_ref, muk_ref,
                      muv_ref, muwa_ref, w0_ref, a0_ref, kk_ref, ka_ref, rk_ref, lora_ref, tri_ref,
                      bd_ref, lhs_ref, n_ref, y0_ref, bonp_ref, chain):
    L = WKV_CHUNK
    nch = ncg
    rws, gw = r_ref.shape
    nheads = gw // HEAD_DIM
    rg = ncg * L
    row0 = c0 * L
    rows = lax.broadcasted_iota(jnp.int32, (rg, 1), 0)

    def token_shift(x_ref, edge_row, mu):
        x = x_ref[row0:row0 + rg, :]
        if direction == 0:
            nb = edge_row if row0 == 0 else x_ref[row0 - 1:row0, :]
            sh = jnp.where(rows == 0, nb, pltpu.roll(x, 1, 0))
        else:
            nb = edge_row if row0 + rg == rws else x_ref[row0 + rg:row0 + rg + 1, :]
            sh = jnp.where(rows == rg - 1, nb, pltpu.roll(x, rg - 1, 0))
        return x + mu * (sh - x)

    r = token_shift(r_ref, edge[0], mur_ref[...])
    k = token_shift(k_ref, edge[1], muk_ref[...])
    v = token_shift(v_ref, edge[2], muv_ref[...])
    wa = token_shift(wa_ref, edge[3], muwa_ref[...])
    yield

    lane_wa = lax.broadcasted_iota(jnp.int32, (1, wa.shape[1]), 1)
    wa = jnp.where(lane_wa < LORA, jnp.tanh(wa), wa)
    za = _dot(wa, lora_ref[...])
    zw = w0_ref[...] + za[:, :gw]
    a = _sigmoid(a0_ref[...] + za[:, gw:])
    lw = -math.exp(-0.5) * _sigmoid(zw)

    bd = bd_ref[...]
    kk = k * kk_ref[...]
    kk = kk * lax.rsqrt(jnp.maximum(_segsum(kk * kk, bd, 1), 1e-12))
    k2 = k * (1.0 + (a - 1.0) * ka_ref[...])
    bonp_ref[row0:row0 + rg, :] = _segsum(r * k2 * rk_ref[...], bd, 2) * v
    yield

    cs = _dot_exact_lhs(tri_ref[0:rg, 0:rg], lw, 2)
    yield
    e_inc = jnp.exp(cs)
    e_inv = jnp.exp(-cs)
    e_exc = jnp.exp(cs - lw)
    to3 = lambda x: x.reshape(nch, L, gw)
    rt = to3(r * e_inc)
    kt = to3(k2 * e_inv)
    at = to3(-kk * e_exc)
    bt = to3(kk * a * e_inv)
    v3 = to3(v)
    last = L - 1 if direction == 0 else 0
    e_end = to3(e_inc)[:, last:last + 1, :]
    kh = kt * e_end
    bh = bt * e_end

    lane = lax.broadcasted_iota(jnp.int32, (1, 1, gw), 2)
    head_masks = [(lane // HEAD_DIM) == h for h in range(nheads)]

    def blockdiag(x3):
        return jnp.concatenate([jnp.where(m, x3, 0.0) for m in head_masks], axis=1)

    def hmm(lp, *xps):
        rhs = [blockdiag(xp.astype(BF16)) for xp in xps]
        rhs = rhs[0] if len(rhs) == 1 else jnp.concatenate(rhs, axis=2)
        return _bmm(lp, rhs, _NN, 1)

    x_ar = jnp.concatenate([at, rt], axis=1)
    g = _bmm(x_ar, jnp.concatenate([blockdiag(bt.astype(BF16)), blockdiag(kt.astype(BF16))], axis=1),
             _NT, 1)
    g_b, g_k = g[:, :, :gw], g[:, :, gw:]
    yield

    t_idx = lax.broadcasted_iota(jnp.int32, (1, L, gw), 1)
    s_idx = lax.broadcasted_iota(jnp.int32, (1, L, gw), 2) % L
    if direction == 0:
        strict, incl = s_idx < t_idx, s_idx <= t_idx
    else:
        strict, incl = s_idx > t_idx, s_idx >= t_idx
    a_ab = jnp.where(strict, g_b[:, :L], 0.0)
    a_rb = jnp.where(incl, g_b[:, L:], 0.0)
    a_ak = jnp.where(strict, g_k[:, :L], 0.0)
    a_rk = jnp.where(incl, g_k[:, L:], 0.0)

    tinv = jnp.where(s_idx == t_idx, 1.0, 0.0) + a_ab
    apow = hmm(a_ab, a_ab)
    nsq = int(math.log2(L)) - 1
    for i in range(nsq):
        yield
        if i + 1 < nsq:
            both = hmm(apow, apow, tinv)
            apow, tinv = both[:, :, :gw], tinv + both[:, :, gw:]
        else:
            tinv = tinv + hmm(apow, tinv)

    akv = _bmm(jnp.concatenate([a_ak, a_rk], axis=1), blockdiag(v3.astype(BF16)), _NN, 1)
    wu = hmm(tinv, at, akv[:, :L])
    wt, u0 = wu[:, :, :gw], wu[:, :, gw:]
    yield
    ry = hmm(a_rb, wt, u0)
    rh = rt + ry[:, :, :gw]
    y0 = ry[:, :, gw:] + akv[:, L:]

    bk_t = jnp.swapaxes(jnp.concatenate([bh, kh], axis=1), 1, 2)
    rhs = jnp.concatenate([wu, jnp.concatenate([jnp.zeros_like(v3), v3], axis=2)], axis=1)
    mn = _bmm(bk_t, rhs, _NN, 1)
    ri = lax.broadcasted_iota(jnp.int32, (1, gw, gw), 1)
    ci = lax.broadcasted_iota(jnp.int32, (1, gw, gw), 2)
    same_head = (ri // HEAD_DIM) == (ci // HEAD_DIM)
    m_mat = jnp.where(same_head, mn[:, :, :gw], 0.0) + jnp.where(ri == ci, e_end, 0.0)
    yield
    for _ in chain:
        pass
    lhs_ref[c0:c0 + ncg] = jnp.concatenate([rh, m_mat], axis=1).astype(BF16)
    n_ref[c0:c0 + ncg] = jnp.where(same_head, mn[:, :, gw:], 0.0)
    y0_ref[c0:c0 + ncg] = y0


def _wkv(p3, direction, col0, mu, w0, a0, lora_w, k_k, k_a, r_k, gw, rws):
    bsz, seq, _ = p3.shape
    c = w0.shape[-1]
    ng = c // gw
    nstep = seq // rws
    L = WKV_CHUNK
    t = np.arange(rws)
    same = (t[:, None] // L) == (t[None, :] // L)
    tri = same & ((t[None, :] <= t[:, None]) if direction == 0 else (t[None, :] >= t[:, None]))
    hd = np.arange(gw) // HEAD_DIM
    bd = hd[:, None] == hd[None, :]
    tri = jnp.asarray(tri, BF16)
    bd = jnp.asarray(bd, BF16)

    blk = (lambda s: s) if direction == 0 else (lambda s: nstep - 1 - s)
    rb = lambda s: blk(jnp.minimum(s, nstep - 1))
    ob = lambda s: blk(jnp.maximum(s - 1, 0))
    stream = lambda off: pl.BlockSpec((None, rws, gw), lambda b, g, s: (b, rb(s), off // gw + g))
    vec = pl.BlockSpec((1, gw), lambda b, g, s: (0, g))
    full2 = lambda shape: pl.BlockSpec(shape, lambda b, g, s: (0, 0))
    wa_w = 2 * LORA
    out_spec = pl.BlockSpec((None, rws, gw), lambda b, g, s: (b, ob(s), g))
    row = lambda x: x.reshape(1, -1)
    mu_d = mu[direction]
    nch = rws // L
    return pl.pallas_call(
        functools.partial(_wkv_kernel, direction),
        grid=(bsz, ng, nstep + 1),
        in_specs=[
            stream(col0), stream(col0 + c), stream(col0 + 2 * c),
            pl.BlockSpec((None, rws, wa_w), lambda b, g, s: (b, rb(s), (col0 + 3 * c) // wa_w)),
            vec, vec, vec, full2((1, wa_w)),
            vec, vec, vec, vec, vec,
            pl.BlockSpec((None, wa_w, 2 * gw), lambda b, g, s: (g, 0, 0)),
            full2((rws, rws)), full2((gw, gw)),
        ],
        out_specs=[out_spec, out_spec],
        out_shape=[jax.ShapeDtypeStruct((bsz, seq, c), F32)] * 2,
        scratch_shapes=[pltpu.VMEM((gw, gw), F32), pltpu.VMEM((1, gw), F32),
                        pltpu.VMEM((1, gw), F32), pltpu.VMEM((1, gw), F32),
                        pltpu.VMEM((1, wa_w), F32),
                        pltpu.VMEM((nch, L + gw, gw), BF16), pltpu.VMEM((nch, gw, gw), F32),
                        pltpu.VMEM((nch, L, gw), F32), pltpu.VMEM((rws, gw), F32)],
        compiler_params=_cparams(("parallel", "parallel", "arbitrary")),
        name="wkv_fwd" if direction == 0 else "wkv_bwd",
    )(p3, p3, p3, p3, row(mu_d[:c]), row(mu_d[c:2 * c]), row(mu_d[2 * c:3 * c]), row(mu_d[3 * c:]),
      row(w0[direction]), row(a0[direction]), row(k_k), row(k_a), row(r_k),
      lora_w[direction], tri, bd)


def _lora_weights(w2, a2, gw):
    nd, lo, c = w2.shape
    ng = c // gw
    w2g = w2.reshape(nd, lo, ng, gw).transpose(0, 2, 1, 3)
    a2g = a2.reshape(nd, lo, ng, gw).transpose(0, 2, 1, 3)
    z = jnp.zeros_like(w2g)
    top = jnp.concatenate([w2g, z], axis=3)
    bot = jnp.concatenate([z, a2g], axis=3)
    return jnp.concatenate([top, bot], axis=2).astype(BF16)


FNET_N2 = 128


def _cos_sin(n, m=None, period=None):
    i = np.arange(n)[:, None]
    j = np.arange(n if m is None else m)[None, :]
    ang = 2.0 * np.pi * ((i * j) % (period or n)) / (period or n)
    return np.cos(ang), np.sin(ang)


def _fnet_weight_kernel(scale, c_ref, s_ref, w_ref, o_ref):
    hi = lax.Precision.HIGHEST
    for g in range(w_ref.shape[0]):
        wc = jnp.dot(c_ref[...], w_ref[g], precision=hi, preferred_element_type=F32)
        ws = jnp.dot(s_ref[...], w_ref[g], precision=hi, preferred_element_type=F32)
        o_ref[g] = jnp.concatenate([wc, -ws], axis=1) * scale


def _fnet_chan_kernel(f_ref, w_ref, gr_ref, gi_ref):
    gd = w_ref.shape[1]
    re, im = [], []
    for g in range(w_ref.shape[0]):
        z = _dot3(f_ref[:, g * gd:(g + 1) * gd], w_ref[g])
        re.append(z[:, :gd])
        im.append(z[:, gd:])
    gr_ref[...] = jnp.concatenate(re, axis=1)
    gi_ref[...] = jnp.concatenate(im, axis=1)


def _fnet_stage1_kernel(nchan, gr_ref, gi_ref, m_ref, twc_ref, tws_ref, zr_ref, zi_ref):
    n1 = gr_ref.shape[0]
    z = _dot3(m_ref[...], jnp.concatenate([gr_ref[...], gi_ref[...]], axis=0))
    zr, zi = z[:n1], z[n1:]
    q = gr_ref.shape[1] // nchan
    shape = (n1, nchan)
    tc = jnp.concatenate([jnp.broadcast_to(twc_ref[:, j:j + 1], shape) for j in range(q)], axis=1)
    ts = jnp.concatenate([jnp.broadcast_to(tws_ref[:, j:j + 1], shape) for j in range(q)], axis=1)
    zr_ref[...] = zr * tc + zi * ts
    zi_ref[...] = zi * tc - zr * ts


def _fnet_stage2_kernel(zr_ref, zi_ref, m_ref, o_ref):
    o_ref[...] = _dot3(m_ref[...], jnp.concatenate([zr_ref[...], zi_ref[...]], axis=0))


def _fnet(p2d, col_blk, bsz, seq, w_f, tm):
    t = p2d.shape[0]
    ng, gd, _ = w_f.shape
    c = ng * gd
    n2 = FNET_N2
    n1 = seq // n2
    cb = 4 * c
    scale = 1.0 / math.sqrt(seq * gd)
    cd, sd = _cos_sin(gd)
    c1, s1 = _cos_sin(n1)
    c2, s2 = _cos_sin(n2)
    twc, tws = _cos_sin(n1, n2, seq)
    m1 = np.block([[c1, s1], [-s1, c1]])
    m2 = np.concatenate([c2, s2], axis=1)
    to_blocks = lambda tw: jnp.asarray(tw.reshape(n1, n2 * c // cb, cb // c).transpose(1, 0, 2), F32)
    const = lambda a: jnp.asarray(a, F32)

    wcat = pl.pallas_call(
        functools.partial(_fnet_weight_kernel, scale),
        out_shape=jax.ShapeDtypeStruct((ng, gd, 2 * gd), F32),
        name="fnet_weights",
    )(const(cd), const(sd), w_f)

    gr, gi = pl.pallas_call(
        _fnet_chan_kernel,
        grid=(t // tm,),
        in_specs=[pl.BlockSpec((tm, c), lambda i: (i, col_blk)),
                  pl.BlockSpec((ng, gd, 2 * gd), lambda i: (0, 0, 0))],
        out_specs=[pl.BlockSpec((tm, c), lambda i: (i, 0))] * 2,
        out_shape=[jax.ShapeDtypeStruct((t, c), F32)] * 2,
        compiler_params=_cparams(("parallel",)),
        name="fnet_chan",
    )(p2d, wcat)

    gr = gr.reshape(bsz, n1, n2 * c)
    gi = gi.reshape(bsz, n1, n2 * c)
    blk = pl.BlockSpec((None, n1, cb), lambda b, j: (b, 0, j))
    twb = pl.BlockSpec((None, n1, cb // c), lambda b, j: (j, 0, 0))
    zr, zi = pl.pallas_call(
        functools.partial(_fnet_stage1_kernel, c),
        grid=(bsz, n2 * c // cb),
        in_specs=[blk, blk, pl.BlockSpec((2 * n1, 2 * n1), lambda b, j: (0, 0)), twb, twb],
        out_specs=[blk, blk],
        out_shape=[jax.ShapeDtypeStruct((bsz, n1, n2 * c), F32)] * 2,
        compiler_params=_cparams(("parallel", "parallel")),
        name="fnet_stage1",
    )(gr, gi, const(m1), to_blocks(twc), to_blocks(tws))

    zr = zr.reshape(bsz, seq, c)
    zi = zi.reshape(bsz, seq, c)
    blk2 = pl.BlockSpec((None, n2, c), lambda b, j: (b, j, 0))
    out = pl.pallas_call(
        _fnet_stage2_kernel,
        grid=(bsz, n1),
        in_specs=[blk2, blk2, pl.BlockSpec((n2, 2 * n2), lambda b, j: (0, 0))],
        out_specs=pl.BlockSpec((None, n2, c), lambda b, j: (b, 0, j)),
        out_shape=jax.ShapeDtypeStruct((bsz, n2, n1 * c), F32),
        compiler_params=_cparams(("parallel", "parallel")),
        name="fnet_stage2",
    )(zr, zi, const(m2))
    return out.reshape(t, c)


def _odd_out_kernel(y0_ref, y1_ref, b0_ref, b1_ref, zc_ref, fn_ref, zd_ref, h_ref, lng_ref,
                    lnb_ref, bd_ref, wo_ref, fg_ref, o_ref):
    c = y0_ref.shape[1]
    bd = bd_ref[...]
    inv_n = 1.0 / HEAD_DIM
    ysum = y0_ref[...] + y1_ref[...]
    mean = _segsum(ysum, bd) * inv_n
    cen = ysum - mean
    var = _segsum(cen * cen, bd) * inv_n
    gn = cen * lax.rsqrt(var + GN_EPS) * lng_ref[...] + lnb_ref[...]
    yc = (gn + b0_ref[...] + b1_ref[...]) * _silu(zc_ref[...])
    yd = fn_ref[...] * _silu(zd_ref[...])
    out = h_ref[...]
    out = out + jnp.dot(yc.astype(BF16), wo_ref[0:c, :], preferred_element_type=F32)
    out = out + jnp.dot(yd.astype(BF16), wo_ref[c:, :], preferred_element_type=F32)
    ms = jnp.mean(out * out, axis=-1, keepdims=True)
    o_ref[...] = out * lax.rsqrt(ms + RMS_EPS) * fg_ref[...]


def _odd_out(y0, y1, b0, b1, p2d, zc_blk, fn, zd_blk, h2d, lnx_g, lnx_b, w_out_bf16, final_g, tm):
    t, d = h2d.shape
    c = y0.shape[1]
    cf = fn.shape[1]
    hd = np.arange(2 * HEAD_DIM) // HEAD_DIM
    bd = jnp.asarray(hd[:, None] == hd[None, :], BF16)
    rowblk = lambda w: pl.BlockSpec((tm, w), lambda i: (i, 0))
    full = lambda shape: pl.BlockSpec(shape, lambda i: (0, 0))
    return pl.pallas_call(
        _odd_out_kernel,
        grid=(t // tm,),
        in_specs=[rowblk(c), rowblk(c), rowblk(c), rowblk(c),
                  pl.BlockSpec((tm, c), lambda i: (i, zc_blk)),
                  rowblk(cf),
                  pl.BlockSpec((tm, cf), lambda i: (i, zd_blk)),
                  rowblk(d), full((1, c)), full((1, c)), full(bd.shape),
                  full(w_out_bf16.shape), full((1, d))],
        out_specs=rowblk(d),
        out_shape=jax.ShapeDtypeStruct((t, d), F32),
        compiler_params=_cparams(("parallel",)),
        name="odd_out",
    )(y0, y1, b0, b1, p2d, fn, p2d, h2d, lnx_g.reshape(1, c), lnx_b.reshape(1, c), bd,
      w_out_bf16, final_g.reshape(1, d))


WKV_LANES = 128
WKV_ROWS = 512


def _odd_layer(h2d, bsz, seq, norm_g, w_in, mu, w0, w2, a0, a2, k_k, k_a, r_k, lnx_g, lnx_b,
               fnet_w, w_out, final_g):
    t, d = h2d.shape
    c = w0.shape[-1]
    rs = 3 * c + 2 * LORA
    cf = fnet_w.shape[0] * fnet_w.shape[1]
    w_perm = jnp.concatenate([w_in[:, rs:], w_in[:, :rs]], axis=1).astype(BF16)
    p2 = _norm_proj(h2d, norm_g, w_perm, tm=min(256, t), tn=w_perm.shape[1], out_dtype=F32)
    col0 = c + 2 * cf
    p3 = p2.reshape(bsz, seq, -1)
    lora = _lora_weights(w2, a2, WKV_LANES)
    rws = min(WKV_ROWS, seq)
    y0, b0 = _wkv(p3, 0, col0, mu, w0, a0, lora, k_k, k_a, r_k.reshape(-1), WKV_LANES, rws)
    y1, b1 = _wkv(p3, 1, col0, mu, w0, a0, lora, k_k, k_a, r_k.reshape(-1), WKV_LANES, rws)
    fn = _fnet(p2, c // cf, bsz, seq, fnet_w, tm=min(512, t))
    flat = lambda a: a.reshape(t, c)
    return _odd_out(flat(y0), flat(y1), flat(b0), flat(b1), p2, 0, fn, c // cf + 1, h2d,
                    lnx_g, lnx_b, w_out.astype(BF16), final_g, tm=min(256, t))


def kernel(x, e_norm_g, e_w_in, e_conv_w, e_sgu_ln_g, e_sgu_ln_b, e_sgu_w, e_sgu_b, e_w_out,
           o_norm_g, o_w_in, o_mu, o_w0, o_w2, o_a0, o_a2, o_k_k, o_k_a, o_r_k, o_lnx_g, o_lnx_b,
           o_fnet_w, o_w_out, final_norm_g):
    bsz, seq, d = x.shape
    assert e_norm_g.shape[0] == 1 and o_norm_g.shape[0] == 1, "two-layer trunk: one even, one odd layer"
    h = x.reshape(bsz * seq, d)
    h = _even_layer(h, seq, e_norm_g[0], e_w_in[0], e_conv_w[0], e_sgu_ln_g[0], e_sgu_ln_b[0],
                    e_sgu_w[0], e_sgu_b[0], e_w_out[0])
    out = _odd_layer(h, bsz, seq, o_norm_g[0], o_w_in[0], o_mu[0], o_w0[0], o_w2[0], o_a0[0],
                     o_a2[0], o_k_k[0], o_k_a[0], o_r_k[0], o_lnx_g[0], o_lnx_b[0], o_fnet_w[0],
                     o_w_out[0], final_norm_g)
    return out.reshape(bsz, seq, d)
```

```python
import functools
import math

import numpy as np
import jax
import jax.numpy as jnp
from jax import lax
from jax.experimental import pallas as pl
from jax.experimental.pallas import tpu as pltpu

F32 = jnp.float32
BF16 = jnp.bfloat16

RMS_EPS = 1e-6
SGU_LN_EPS = 1e-5
GN_EPS = 64e-5
SGU_CHUNK = 128
SGU_GROUPS = 8
HEAD_DIM = 64
LORA = 64
FNET_GROUPS = 4
FNET_GROUP_DIM = 128

V7X_LANES = 128
V7X_SUBLANES = 8
V7X_VMEM_BYTES = 64 * 1024 * 1024
VMEM_LIMIT = V7X_VMEM_BYTES - 8 * 1024 * 1024


def _cparams(sem):
    return pltpu.CompilerParams(dimension_semantics=sem, vmem_limit_bytes=VMEM_LIMIT)


def _silu(z):
    return z * (1.0 / (1.0 + jnp.exp(-z)))


def _sigmoid(z):
    return 1.0 / (1.0 + jnp.exp(-z))


def _dot(a, b):
    return jnp.dot(a.astype(BF16), b.astype(BF16), preferred_element_type=F32)


def _split(a, n):
    parts = []
    rem = a
    for _ in range(n):
        p = rem.astype(BF16)
        parts.append(p)
        rem = rem - p.astype(F32)
    return parts


def _dot_exact_rhs(a, b_bf16, n=3):
    acc = None
    for p in _split(a, n):
        t = jnp.dot(p, b_bf16, preferred_element_type=F32)
        acc = t if acc is None else acc + t
    return acc


def _proj_kernel(x_ref, g_ref, w_ref, o_ref, hn_ref):
    @pl.when(pl.program_id(1) == 0)
    def _():
        x = x_ref[...]
        ms = jnp.mean(x * x, axis=-1, keepdims=True)
        hn_ref[...] = (x * lax.rsqrt(ms + RMS_EPS) * g_ref[...]).astype(BF16)

    o_ref[...] = jnp.dot(hn_ref[...], w_ref[...], preferred_element_type=F32).astype(o_ref.dtype)


def _norm_proj(x2d, g, w_bf16, tm, tn, out_dtype):
    t, d = x2d.shape
    n = w_bf16.shape[1]
    assert t % tm == 0 and n % tn == 0
    return pl.pallas_call(
        _proj_kernel,
        grid=(t // tm, n // tn),
        in_specs=[
            pl.BlockSpec((tm, d), lambda i, j: (i, 0)),
            pl.BlockSpec((1, d), lambda i, j: (0, 0)),
            pl.BlockSpec((d, tn), lambda i, j: (0, j)),
        ],
        out_specs=pl.BlockSpec((tm, tn), lambda i, j: (i, j)),
        out_shape=jax.ShapeDtypeStruct((t, n), out_dtype),
        scratch_shapes=[pltpu.VMEM((tm, d), BF16)],
        compiler_params=_cparams(("parallel", "arbitrary")),
        name="norm_proj",
    )(x2d, g.reshape(1, d), w_bf16)


def _even_mix_kernel(seq, p_ref, xap_ref, cap_ref, xan_ref, can_ref, h_ref, cw_ref, lng_ref,
                     lnb_ref, sw_ref, sb_ref, wo_ref, o_ref):
    tm = p_ref.shape[0]
    d = h_ref.shape[1]
    row0 = pl.program_id(0) * tm
    at_seq_start = (row0 % seq) == 0
    at_seq_end = ((row0 + tm) % seq) == 0

    col = lambda j: p_ref[:, j * d:(j + 1) * d].astype(F32)
    xa, ba, ca, za, ub, vb, zb = (col(j) for j in range(7))

    xc = ca * xa
    hr = xap_ref.shape[0]
    halo_prev = xap_ref[hr - 1:hr, :].astype(F32) * cap_ref[hr - 1:hr, :].astype(F32)
    halo_next = xan_ref[0:1, :].astype(F32) * can_ref[0:1, :].astype(F32)
    halo_prev = jnp.where(at_seq_start, 0.0, halo_prev)
    halo_next = jnp.where(at_seq_end, 0.0, halo_next)
    rows = lax.broadcasted_iota(jnp.int32, (tm, 1), 0)
    prev = jnp.where(rows == 0, halo_prev, pltpu.roll(xc, 1, 0))
    nxt = jnp.where(rows == tm - 1, halo_next, pltpu.roll(xc, tm - 1, 0))
    conv = cw_ref[0:1, :] * prev + cw_ref[1:2, :] * xc + cw_ref[2:3, :] * nxt
    ya = ba * conv * _silu(za)

    mu = jnp.mean(vb, axis=-1, keepdims=True)
    cen = vb - mu
    var = jnp.mean(cen * cen, axis=-1, keepdims=True)
    vn = (cen * lax.rsqrt(var + SGU_LN_EPS) * lng_ref[...] + lnb_ref[...]).astype(BF16)
    gw = d // SGU_GROUPS
    chunk_rows = []
    for n in range(tm // SGU_CHUNK):
        r0 = n * SGU_CHUNK
        cols = []
        for g in range(SGU_GROUPS):
            cols.append(jnp.dot(sw_ref[g], vn[r0:r0 + SGU_CHUNK, g * gw:(g + 1) * gw],
                                preferred_element_type=F32))
        chunk_rows.append(jnp.concatenate(cols, axis=1) + sb_ref[...])
    mixed = jnp.concatenate(chunk_rows, axis=0)
    yb = ub * mixed * _silu(zb)

    out = h_ref[...]
    out = out + jnp.dot(ya.astype(BF16), wo_ref[0:d, :], preferred_element_type=F32)
    out = out + jnp.dot(yb.astype(BF16), wo_ref[d:2 * d, :], preferred_element_type=F32)
    o_ref[...] = out


def _even_mix(p, h2d, seq, conv_w, ln_g, ln_b, sgu_w_bf16, sgu_bias_full, w_out_bf16, tm):
    t, d = h2d.shape
    hr = V7X_SUBLANES * (4 // p.dtype.itemsize)
    nblk = t // hr
    rpt = tm // hr
    prev_map = lambda c: (lambda i: (jnp.maximum(i * rpt - 1, 0), c))
    next_map = lambda c: (lambda i: (jnp.minimum((i + 1) * rpt, nblk - 1), c))
    full = lambda shape: pl.BlockSpec(shape, lambda i: (0,) * len(shape))
    return pl.pallas_call(
        functools.partial(_even_mix_kernel, seq),
        grid=(t // tm,),
        in_specs=[
            pl.BlockSpec((tm, 7 * d), lambda i: (i, 0)),
            pl.BlockSpec((hr, d), prev_map(0)),
            pl.BlockSpec((hr, d), prev_map(2)),
            pl.BlockSpec((hr, d), next_map(0)),
            pl.BlockSpec((hr, d), next_map(2)),
            pl.BlockSpec((tm, d), lambda i: (i, 0)),
            full((3, d)),
            full((1, d)),
            full((1, d)),
            full(sgu_w_bf16.shape),
            full(sgu_bias_full.shape),
            full(w_out_bf16.shape),
        ],
        out_specs=pl.BlockSpec((tm, d), lambda i: (i, 0)),
        out_shape=jax.ShapeDtypeStruct((t, d), F32),
        compiler_params=_cparams(("parallel",)),
        name="even_mix",
    )(p, p, p, p, p, h2d, conv_w, ln_g.reshape(1, d), ln_b.reshape(1, d), sgu_w_bf16,
      sgu_bias_full, w_out_bf16)


def _even_layer(h2d, seq, norm_g, w_in, conv_w, ln_g, ln_b, sgu_w, sgu_b, w_out):
    t, d = h2d.shape
    p = _norm_proj(h2d, norm_g, w_in.astype(BF16), tm=min(1024, t), tn=1024, out_dtype=BF16)
    bias_full = jnp.repeat(sgu_b.T, d // SGU_GROUPS, axis=1)
    return _even_mix(p, h2d, seq, conv_w, ln_g, ln_b, sgu_w.astype(BF16), bias_full,
                     w_out.astype(BF16), tm=256)


def _dot_exact_lhs(a_bf16, b, n=3):
    acc = None
    for p in _split(b, n):
        t = jnp.dot(a_bf16, p, preferred_element_type=F32)
        acc = t if acc is None else acc + t
    return acc


def _dot3(a, b):
    ah, al = _split(a, 2)
    bh, bl = _split(b, 2)
    d = lambda x, y: jnp.dot(x, y, preferred_element_type=F32)
    return d(ah, bh) + (d(ah, bl) + d(al, bh))


_NN = (((2,), (1,)), ((0,), (0,)))
_NT = (((2,), (2,)), ((0,), (0,)))


def _bmm(a, b, dims, passes):
    d = lambda x, y: lax.dot_general(x, y, dims, preferred_element_type=F32)
    if passes == 1:
        return d(a.astype(BF16), b.astype(BF16))
    ah, al = _split(a, 2)
    bh, bl = _split(b, 2)
    return d(ah, bh) + (d(ah, bl) + d(al, bh))


def _segsum(x, bd_bf16, n=3):
    w = bd_bf16.shape[0]
    cols = [_dot_exact_rhs(x[:, j:j + w], bd_bf16, n) for j in range(0, x.shape[1], w)]
    return cols[0] if len(cols) == 1 else jnp.concatenate(cols, axis=1)


WKV_CHUNK = 64


def _wkv_kernel(direction, r_ref, k_ref, v_ref, wa_ref, mur_ref, muk_ref, muv_ref, muwa_ref,
                w0_ref, a0_ref, kk_ref, ka_ref, rk_ref, lora_ref, tri_ref, bd_ref,
                y_ref, bon_ref, h_ref, cr_ref, ck_ref, cv_ref, cwa_ref,
                lhs_ref, n_ref, y0_ref, bonp_ref):
    rws, gw = r_ref.shape
    L = WKV_CHUNK
    nch = rws // L
    nheads = gw // HEAD_DIM

    @pl.when(pl.program_id(2) == 0)
    def _():
        for ref in (h_ref, cr_ref, ck_ref, cv_ref, cwa_ref, lhs_ref, n_ref, y0_ref, bonp_ref):
            ref[...] = jnp.zeros_like(ref)

    def recurrence():
        h = h_ref[...]
        ys = [None] * nch
        for c in (range(nch) if direction == 0 else range(nch - 1, -1, -1)):
            both = jnp.dot(lhs_ref[c], h.astype(BF16), preferred_element_type=F32)
            ys[c] = both[:L] + y0_ref[c]
            h = both[L:] + n_ref[c]
            yield
        h_ref[...] = h
        y_ref[...] = jnp.concatenate(ys, axis=0).astype(y_ref.dtype)

    chain = recurrence()
    bon_ref[...] = bonp_ref[...].astype(bon_ref.dtype)

    streams = ((r_ref, cr_ref), (k_ref, ck_ref), (v_ref, cv_ref), (wa_ref, cwa_ref))
    edge = [c_ref[...] for _, c_ref in streams]
    far = rws - 1 if direction == 0 else 0
    for x_ref, c_ref in streams:
        c_ref[...] = x_ref[far:far + 1, :]

    ncg = min(WKV_GROUP, nch)
    groups = [_wkv_local_stages(direction, g0, ncg, edge, r_ref, k_ref, v_ref, wa_ref, mur_ref,
                                muk_ref, muv_ref, muwa_ref, w0_ref, a0_ref, kk_ref, ka_ref, rk_ref,
                                lora_ref, tri_ref, bd_ref, lhs_ref, n_ref, y0_ref, bonp_ref, chain)
              for g0 in range(0, nch, ncg)]
    slot = 0
    while groups:
        for gi, gen in enumerate(list(groups)):
            if slot >= gi * WKV_SKEW and next(gen, "done") == "done":
                groups.remove(gen)
        next(chain, None)
        slot += 1
    for _ in chain:
        pass


WKV_GROUP = 4
WKV_SKEW = 1


def _wkv_local_stages(direction, c0, ncg, edge, r_ref, k_ref, v_ref, wa_ref, mur_ref, muk_ref,
                      muv_ref, muwa_ref, w0_ref, a0_ref, kk_ref, ka_ref, rk_ref, lora_ref, tri_ref,
                      bd_ref, lhs_ref, n_ref, y0_ref, bonp_ref, chain):
    L = WKV_CHUNK
    nch = ncg
    rws, gw = r_ref.shape
    nheads = gw // HEAD_DIM
    rg = ncg * L
    row0 = c0 * L
    rows = lax.broadcasted_iota(jnp.int32, (rg, 1), 0)

    def token_shift(x_ref, edge_row, mu):
        x = x_ref[row0:row0 + rg, :]
        if direction == 0:
            nb = edge_row if row0 == 0 else x_ref[row0 - 1:row0, :]
            sh = jnp.where(rows == 0, nb, pltpu.roll(x, 1, 0))
        else:
            nb = edge_row if row0 + rg == rws else x_ref[row0 + rg:row0 + rg + 1, :]
            sh = jnp.where(rows == rg - 1, nb, pltpu.roll(x, rg - 1, 0))
        return x + mu * (sh - x)

    r = token_shift(r_ref, edge[0], mur_ref[...])
    k = token_shift(k_ref, edge[1], muk_ref[...])
    v = token_shift(v_ref, edge[2], muv_ref[...])
    wa = token_shift(wa_ref, edge[3], muwa_ref[...])
    yield

    lane_wa = lax.broadcasted_iota(jnp.int32, (1, wa.shape[1]), 1)
    wa = jnp.where(lane_wa < LORA, jnp.tanh(wa), wa)
    za = _dot(wa, lora_ref[...])
    zw = w0_ref[...] + za[:, :gw]
    a = _sigmoid(a0_ref[...] + za[:, gw:])
    lw = -math.exp(-0.5) * _sigmoid(zw)

    bd = bd_ref[...]
    kk = k * kk_ref[...]
    kk = kk * lax.rsqrt(jnp.maximum(_segsum(kk * kk, bd, 1), 1e-12))
    k2 = k * (1.0 + (a - 1.0) * ka_ref[...])
    bonp_ref[row0:row0 + rg, :] = _segsum(r * k2 * rk_ref[...], bd, 2) * v
    yield

    cs = _dot_exact_lhs(tri_ref[0:rg, 0:rg], lw, 2)
    yield
    e_inc = jnp.exp(cs)
    e_inv = jnp.exp(-cs)
    e_exc = jnp.exp(cs - lw)
    to3 = lambda x: x.reshape(nch, L, gw)
    rt = to3(r * e_inc)
    kt = to3(k2 * e_inv)
    at = to3(-kk * e_exc)
    bt = to3(kk * a * e_inv)
    v3 = to3(v)
    last = L - 1 if direction == 0 else 0
    e_end = to3(e_inc)[:, last:last + 1, :]
    kh = kt * e_end
    bh = bt * e_end

    lane = lax.broadcasted_iota(jnp.int32, (1, 1, gw), 2)
    head_masks = [(lane // HEAD_DIM) == h for h in range(nheads)]

    def blockdiag(x3):
        return jnp.concatenate([jnp.where(m, x3, 0.0) for m in head_masks], axis=1)

    def hmm(lp, *xps):
        rhs = [blockdiag(xp.astype(BF16)) for xp in xps]
        rhs = rhs[0] if len(rhs) == 1 else jnp.concatenate(rhs, axis=2)
        return _bmm(lp, rhs, _NN, 1)

    x_ar = jnp.concatenate([at, rt], axis=1)
    g = _bmm(x_ar, jnp.concatenate([blockdiag(bt.astype(BF16)), blockdiag(kt.astype(BF16))], axis=1),
             _NT, 1)
    g_b, g_k = g[:, :, :gw], g[:, :, gw:]
    yield

    t_idx = lax.broadcasted_iota(jnp.int32, (1, L, gw), 1)
    s_idx = lax.broadcasted_iota(jnp.int32, (1, L, gw), 2) % L
    if direction == 0:
        strict, incl = s_idx < t_idx, s_idx <= t_idx
    else:
        strict, incl = s_idx > t_idx, s_idx >= t_idx
    a_ab = jnp.where(strict, g_b[:, :L], 0.0)
    a_rb = jnp.where(incl, g_b[:, L:], 0.0)
    a_ak = jnp.where(strict, g_k[:, :L], 0.0)
    a_rk = jnp.where(incl, g_k[:, L:], 0.0)

    tinv = jnp.where(s_idx == t_idx, 1.0, 0.0) + a_ab
    apow = hmm(a_ab, a_ab)
    nsq = int(math.log2(L)) - 1
    for i in range(nsq):
        yield
        if i + 1 < nsq:
            both = hmm(apow, apow, tinv)
            apow, tinv = both[:, :, :gw], tinv + both[:, :, gw:]
        else:
            tinv = tinv + hmm(apow, tinv)

    akv = _bmm(jnp.concatenate([a_ak, a_rk], axis=1), blockdiag(v3.astype(BF16)), _NN, 1)
    wu = hmm(tinv, at, akv[:, :L])
    wt, u0 = wu[:, :, :gw], wu[:, :, gw:]
    yield
    ry = hmm(a_rb, wt, u0)
    rh = rt + ry[:, :, :gw]
    y0 = ry[:, :, gw:] + akv[:, L:]

    bk_t = jnp.swapaxes(jnp.concatenate([bh, kh], axis=1), 1, 2)
    rhs = jnp.concatenate([wu, jnp.concatenate([jnp.zeros_like(v3), v3], axis=2)], axis=1)
    mn = _bmm(bk_t, rhs, _NN, 1)
    ri = lax.broadcasted_iota(jnp.int32, (1, gw, gw), 1)
    ci = lax.broadcasted_iota(jnp.int32, (1, gw, gw), 2)
    same_head = (ri // HEAD_DIM) == (ci // HEAD_DIM)
    m_mat = jnp.where(same_head, mn[:, :, :gw], 0.0) + jnp.where(ri == ci, e_end, 0.0)
    yield
    for _ in chain:
        pass
    lhs_ref[c0:c0 + ncg] = jnp.concatenate([rh, m_mat], axis=1).astype(BF16)
    n_ref[c0:c0 + ncg] = jnp.where(same_head, mn[:, :, gw:], 0.0)
    y0_ref[c0:c0 + ncg] = y0


def _wkv(p3, direction, col0, mu, w0, a0, lora_w, k_k, k_a, r_k, gw, rws):
    bsz, seq, _ = p3.shape
    c = w0.shape[-1]
    ng = c // gw
    nstep = seq // rws
    L = WKV_CHUNK
    t = np.arange(rws)
    same = (t[:, None] // L) == (t[None, :] // L)
    tri = same & ((t[None, :] <= t[:, None]) if direction == 0 else (t[None, :] >= t[:, None]))
    hd = np.arange(gw) // HEAD_DIM
    bd = hd[:, None] == hd[None, :]
    tri = jnp.asarray(tri, BF16)
    bd = jnp.asarray(bd, BF16)

    blk = (lambda s: s) if direction == 0 else (lambda s: nstep - 1 - s)
    rb = lambda s: blk(jnp.minimum(s, nstep - 1))
    ob = lambda s: blk(jnp.maximum(s - 1, 0))
    stream = lambda off: pl.BlockSpec((None, rws, gw), lambda b, g, s: (b, rb(s), off // gw + g))
    vec = pl.BlockSpec((1, gw), lambda b, g, s: (0, g))
    full2 = lambda shape: pl.BlockSpec(shape, lambda b, g, s: (0, 0))
    wa_w = 2 * LORA
    out_spec = pl.BlockSpec((None, rws, gw), lambda b, g, s: (b, ob(s), g))
    row = lambda x: x.reshape(1, -1)
    mu_d = mu[direction]
    nch = rws // L
    return pl.pallas_call(
        functools.partial(_wkv_kernel, direction),
        grid=(bsz, ng, nstep + 1),
        in_specs=[
            stream(col0), stream(col0 + c), stream(col0 + 2 * c),
            pl.BlockSpec((None, rws, wa_w), lambda b, g, s: (b, rb(s), (col0 + 3 * c) // wa_w)),
            vec, vec, vec, full2((1, wa_w)),
            vec, vec, vec, vec, vec,
            pl.BlockSpec((None, wa_w, 2 * gw), lambda b, g, s: (g, 0, 0)),
            full2((rws, rws)), full2((gw, gw)),
        ],
        out_specs=[out_spec, out_spec],
        out_shape=[jax.ShapeDtypeStruct((bsz, seq, c), BF16)] * 2,
        scratch_shapes=[pltpu.VMEM((gw, gw), F32), pltpu.VMEM((1, gw), F32),
                        pltpu.VMEM((1, gw), F32), pltpu.VMEM((1, gw), F32),
                        pltpu.VMEM((1, wa_w), F32),
                        pltpu.VMEM((nch, L + gw, gw), BF16), pltpu.VMEM((nch, gw, gw), F32),
                        pltpu.VMEM((nch, L, gw), F32), pltpu.VMEM((rws, gw), F32)],
        compiler_params=_cparams(("parallel", "parallel", "arbitrary")),
        name="wkv_fwd" if direction == 0 else "wkv_bwd",
    )(p3, p3, p3, p3, row(mu_d[:c]), row(mu_d[c:2 * c]), row(mu_d[2 * c:3 * c]), row(mu_d[3 * c:]),
      row(w0[direction]), row(a0[direction]), row(k_k), row(k_a), row(r_k),
      lora_w[direction], tri, bd)


def _lora_weights(w2, a2, gw):
    nd, lo, c = w2.shape
    ng = c // gw
    w2g = w2.reshape(nd, lo, ng, gw).transpose(0, 2, 1, 3)
    a2g = a2.reshape(nd, lo, ng, gw).transpose(0, 2, 1, 3)
    z = jnp.zeros_like(w2g)
    top = jnp.concatenate([w2g, z], axis=3)
    bot = jnp.concatenate([z, a2g], axis=3)
    return jnp.concatenate([top, bot], axis=2).astype(BF16)


FNET_N2 = 128


FNET_PITCH_PAD = 8
FNET_UNROLL = 8


def _cos_sin(idx, period):
    ang = 2.0 * np.pi * (idx % period) / period
    return np.cos(ang), np.sin(ang)


def _hi_lo(a):
    hi = jnp.asarray(a, F32).astype(BF16)
    lo = (jnp.asarray(a, F32) - hi.astype(F32)).astype(BF16)
    return hi, lo


def _dot3_split(ah, al, b):
    bh, bl = _split(b, 2)
    d = lambda x, y: jnp.dot(x, y, preferred_element_type=F32)
    return d(ah, bh) + (d(ah, bl) + d(al, bh))


def _fnet_weight_kernel(scale, c_ref, s_ref, w_ref, o_ref):
    for g in range(w_ref.shape[0]):
        wc = _dot3(c_ref[...], w_ref[g])
        ws = _dot3(s_ref[...], w_ref[g])
        o_ref[g] = jnp.concatenate([wc, -ws], axis=1) * scale


def _fnet_kernel(n1, n2, f_ref, w_ref, m1h_ref, m1l_ref, m2h_ref, m2l_ref, o_ref,
                 gr_ref, gi_ref, zr_ref, zi_ref):
    gd = f_ref.shape[1]
    pitch = n2 + FNET_PITCH_PAD
    wh, wl = _split(w_ref[...], 2)

    def chan_body(s1, carry):
        src = pl.ds(pl.multiple_of(s1 * n2, n2), n2)
        dst = pl.ds(pl.multiple_of(s1 * pitch, V7X_SUBLANES), n2)
        fh, fl = _split(f_ref[src, :], 2)
        d = lambda x, y: jnp.dot(x, y, preferred_element_type=F32)
        z = d(fh, wh) + (d(fh, wl) + d(fl, wh))
        gr_ref[dst, :] = z[:, :gd]
        gi_ref[dst, :] = z[:, gd:]
        return carry

    lax.fori_loop(0, n1, chan_body, 0, unroll=min(FNET_UNROLL, n1))

    m1h, m1l = m1h_ref[...], m1l_ref[...]

    def stage1_body(s2, carry):
        idx = pl.ds(s2, n1, stride=pitch)
        x = jnp.concatenate([gr_ref[idx, :], gi_ref[idx, :]], axis=0)
        z = _dot3_split(m1h, m1l, x)
        zr_ref[idx, :] = z[:n1]
        zi_ref[idx, :] = z[n1:]
        return carry

    lax.fori_loop(0, n2, stage1_body, 0, unroll=2 * FNET_UNROLL)

    def stage2_body(s1, carry):
        src = pl.ds(pl.multiple_of(s1 * pitch, V7X_SUBLANES), n2)
        zz = jnp.concatenate([zr_ref[src, :], zi_ref[src, :]], axis=0)
        o_ref[pl.ds(s1, n2, stride=n1), :] = _dot3_split(m2h_ref[s1], m2l_ref[s1], zz)
        return carry

    lax.fori_loop(0, n1, stage2_body, 0, unroll=min(FNET_UNROLL, n1))


def _fnet(p3, col0, w_f):
    bsz, seq, _ = p3.shape
    ng, gd, _ = w_f.shape
    n2 = FNET_N2
    n1 = seq // n2
    scale = 1.0 / math.sqrt(seq * gd)
    ar = np.arange
    cd, sd = _cos_sin(ar(gd)[:, None] * ar(gd)[None, :], gd)
    c1, s1 = _cos_sin(ar(n1)[:, None] * ar(n1)[None, :], n1)
    m1 = np.block([[c1, s1], [-s1, c1]])
    c2, s2 = _cos_sin(ar(n2)[None, None, :] * (ar(n1)[:, None, None] + n1 * ar(n2)[None, :, None]), seq)
    m2 = np.concatenate([c2, s2], axis=2)
    m1h, m1l = _hi_lo(m1)
    m2h, m2l = _hi_lo(m2)

    wcat = pl.pallas_call(
        functools.partial(_fnet_weight_kernel, scale),
        out_shape=jax.ShapeDtypeStruct((ng, gd, 2 * gd), F32),
        name="fnet_weights",
    )(jnp.asarray(cd, F32), jnp.asarray(sd, F32), w_f)

    pitch_rows = n1 * (n2 + FNET_PITCH_PAD)
    const2 = lambda a: pl.BlockSpec(a.shape, lambda b, g: (0, 0))
    const3 = lambda a: pl.BlockSpec(a.shape, lambda b, g: (0, 0, 0), pipeline_mode=pl.Buffered(1))
    return pl.pallas_call(
        functools.partial(_fnet_kernel, n1, n2),
        grid=(bsz, ng),
        in_specs=[pl.BlockSpec((None, seq, gd), lambda b, g: (b, 0, col0 // gd + g)),
                  pl.BlockSpec((None, gd, 2 * gd), lambda b, g: (g, 0, 0)),
                  const2(m1h), const2(m1l), const3(m2h), const3(m2l)],
        out_specs=pl.BlockSpec((None, seq, gd), lambda b, g: (b, 0, g)),
        out_shape=jax.ShapeDtypeStruct((bsz, seq, ng * gd), F32),
        scratch_shapes=[pltpu.VMEM((pitch_rows, gd), F32)] * 4,
        compiler_params=_cparams(("parallel", "parallel")),
        name="fnet",
    )(p3, wcat, m1h, m1l, m2h, m2l)


def _odd_out_kernel(y0_ref, y1_ref, b0_ref, b1_ref, zc_ref, fn_ref, zd_ref, h_ref, lng_ref,
                    lnb_ref, bd_ref, wo_ref, fg_ref, o_ref):
    c = y0_ref.shape[1]
    bd = bd_ref[...]
    inv_n = 1.0 / HEAD_DIM
    ysum = y0_ref[...].astype(F32) + y1_ref[...].astype(F32)
    mean = _segsum(ysum, bd) * inv_n
    cen = ysum - mean
    var = _segsum(cen * cen, bd) * inv_n
    gn = cen * lax.rsqrt(var + GN_EPS) * lng_ref[...] + lnb_ref[...]
    yc = (gn + b0_ref[...].astype(F32) + b1_ref[...].astype(F32)) * _silu(zc_ref[...])
    yd = fn_ref[...] * _silu(zd_ref[...])
    out = h_ref[...]
    out = out + jnp.dot(yc.astype(BF16), wo_ref[0:c, :], preferred_element_type=F32)
    out = out + jnp.dot(yd.astype(BF16), wo_ref[c:, :], preferred_element_type=F32)
    ms = jnp.mean(out * out, axis=-1, keepdims=True)
    o_ref[...] = out * lax.rsqrt(ms + RMS_EPS) * fg_ref[...]


def _odd_out(y0, y1, b0, b1, p2d, zc_blk, fn, zd_blk, h2d, lnx_g, lnx_b, w_out_bf16, final_g, tm):
    t, d = h2d.shape
    c = y0.shape[1]
    cf = fn.shape[1]
    hd = np.arange(2 * HEAD_DIM) // HEAD_DIM
    bd = jnp.asarray(hd[:, None] == hd[None, :], BF16)
    rowblk = lambda w: pl.BlockSpec((tm, w), lambda i: (i, 0))
    full = lambda shape: pl.BlockSpec(shape, lambda i: (0, 0))
    return pl.pallas_call(
        _odd_out_kernel,
        grid=(t // tm,),
        in_specs=[rowblk(c), rowblk(c), rowblk(c), rowblk(c),
                  pl.BlockSpec((tm, c), lambda i: (i, zc_blk)),
                  rowblk(cf),
                  pl.BlockSpec((tm, cf), lambda i: (i, zd_blk)),
                  rowblk(d), full((1, c)), full((1, c)), full(bd.shape),
                  full(w_out_bf16.shape), full((1, d))],
        out_specs=rowblk(d),
        out_shape=jax.ShapeDtypeStruct((t, d), F32),
        compiler_params=_cparams(("parallel",)),
        name="odd_out",
    )(y0, y1, b0, b1, p2d, fn, p2d, h2d, lnx_g.reshape(1, c), lnx_b.reshape(1, c), bd,
      w_out_bf16, final_g.reshape(1, d))


WKV_LANES = 128
WKV_ROWS = 512


def _odd_layer(h2d, bsz, seq, norm_g, w_in, mu, w0, w2, a0, a2, k_k, k_a, r_k, lnx_g, lnx_b,
               fnet_w, w_out, final_g):
    t, d = h2d.shape
    c = w0.shape[-1]
    rs = 3 * c + 2 * LORA
    cf = fnet_w.shape[0] * fnet_w.shape[1]
    w_perm = jnp.concatenate([w_in[:, rs:], w_in[:, :rs]], axis=1).astype(BF16)
    p2 = _norm_proj(h2d, norm_g, w_perm, tm=min(256, t), tn=w_perm.shape[1], out_dtype=F32)
    col0 = c + 2 * cf
    p3 = p2.reshape(bsz, seq, -1)
    lora = _lora_weights(w2, a2, WKV_LANES)
    rws = min(WKV_ROWS, seq)
    y0, b0 = _wkv(p3, 0, col0, mu, w0, a0, lora, k_k, k_a, r_k.reshape(-1), WKV_LANES, rws)
    y1, b1 = _wkv(p3, 1, col0, mu, w0, a0, lora, k_k, k_a, r_k.reshape(-1), WKV_LANES, rws)
    fn = _fnet(p3, c, fnet_w).reshape(t, cf)
    flat = lambda a: a.reshape(t, c)
    return _odd_out(flat(y0), flat(y1), flat(b0), flat(b1), p2, 0, fn, c // cf + 1, h2d,
                    lnx_g, lnx_b, w_out.astype(BF16), final_g, tm=min(256, t))


def kernel(x, e_norm_g, e_w_in, e_conv_w, e_sgu_ln_g, e_sgu_ln_b, e_sgu_w, e_sgu_b, e_w_out,
           o_norm_g, o_w_in, o_mu, o_w0, o_w2, o_a0, o_a2, o_k_k, o_k_a, o_r_k, o_lnx_g, o_lnx_b,
           o_fnet_w, o_w_out, final_norm_g):
    bsz, seq, d = x.shape
    assert e_norm_g.shape[0] == 1 and o_norm_g.shape[0] == 1, "two-layer trunk: one even, one odd layer"
    h = x.reshape(bsz * seq, d)
    h = _even_layer(h, seq, e_norm_g[0], e_w_in[0], e_conv_w[0], e_sgu_ln_g[0], e_sgu_ln_b[0],
                    e_sgu_w[0], e_sgu_b[0], e_w_out[0])
    out = _odd_layer(h, bsz, seq, o_norm_g[0], o_w_in[0], o_mu[0], o_w0[0], o_w2[0], o_a0[0],
                     o_a2[0], o_k_k[0], o_k_a[0], o_r_k[0], o_lnx_g[0], o_lnx_b[0], o_fnet_w[0],
                     o_w_out[0], final_norm_g)
    return out.reshape(bsz, seq, d)
```

```python
import functools
import math

import numpy as np
import jax
import jax.numpy as jnp
from jax import lax
from jax.experimental import pallas as pl
from jax.experimental.pallas import tpu as pltpu

F32 = jnp.float32
BF16 = jnp.bfloat16

RMS_EPS = 1e-6
SGU_LN_EPS = 1e-5
GN_EPS = 64e-5
SGU_CHUNK = 128
SGU_GROUPS = 8
HEAD_DIM = 64
LORA = 64
FNET_GROUPS = 4
FNET_GROUP_DIM = 128

V7X_LANES = 128
V7X_SUBLANES = 8
V7X_VMEM_BYTES = 64 * 1024 * 1024
VMEM_LIMIT = V7X_VMEM_BYTES - 8 * 1024 * 1024


def _cparams(sem):
    return pltpu.CompilerParams(dimension_semantics=sem, vmem_limit_bytes=VMEM_LIMIT)


def _silu(z):
    return z * (1.0 / (1.0 + jnp.exp(-z)))


def _sigmoid(z):
    return 1.0 / (1.0 + jnp.exp(-z))


def _dot(a, b):
    return jnp.dot(a.astype(BF16), b.astype(BF16), preferred_element_type=F32)


def _split(a, n):
    parts = []
    rem = a
    for _ in range(n):
        p = rem.astype(BF16)
        parts.append(p)
        rem = rem - p.astype(F32)
    return parts


def _dot_exact_rhs(a, b_bf16, n=3):
    acc = None
    for p in _split(a, n):
        t = jnp.dot(p, b_bf16, preferred_element_type=F32)
        acc = t if acc is None else acc + t
    return acc


def _proj_kernel(x_ref, g_ref, w_ref, o_ref, hn_ref):
    @pl.when(pl.program_id(1) == 0)
    def _():
        x = x_ref[...]
        ms = jnp.mean(x * x, axis=-1, keepdims=True)
        hn_ref[...] = (x * lax.rsqrt(ms + RMS_EPS) * g_ref[...]).astype(BF16)

    o_ref[...] = jnp.dot(hn_ref[...], w_ref[...], preferred_element_type=F32).astype(o_ref.dtype)


def _norm_proj(x2d, g, w_bf16, tm, tn, out_dtype):
    t, d = x2d.shape
    n = w_bf16.shape[1]
    assert t % tm == 0 and n % tn == 0
    return pl.pallas_call(
        _proj_kernel,
        grid=(t // tm, n // tn),
        in_specs=[
            pl.BlockSpec((tm, d), lambda i, j: (i, 0)),
            pl.BlockSpec((1, d), lambda i, j: (0, 0)),
            pl.BlockSpec((d, tn), lambda i, j: (0, j)),
        ],
        out_specs=pl.BlockSpec((tm, tn), lambda i, j: (i, j)),
        out_shape=jax.ShapeDtypeStruct((t, n), out_dtype),
        scratch_shapes=[pltpu.VMEM((tm, d), BF16)],
        compiler_params=_cparams(("parallel", "arbitrary")),
        name="norm_proj",
    )(x2d, g.reshape(1, d), w_bf16)


def _even_kernel(seq, h_ref, hp_ref, hn_ref, ng_ref, wi_ref, cw_ref, lng_ref, lnb_ref, sw_ref,
                 sb_ref, wo_ref, o_ref):
    tm, d = h_ref.shape
    hr = hp_ref.shape[0]
    row0 = pl.program_id(0) * tm
    at_seq_start = (row0 % seq) == 0
    at_seq_end = ((row0 + tm) % seq) == 0

    h = h_ref[...]
    hx = jnp.concatenate([hp_ref[...], h, hn_ref[...]], axis=0)
    ms = jnp.mean(hx * hx, axis=-1, keepdims=True)
    hx = (hx * lax.rsqrt(ms + RMS_EPS) * ng_ref[...]).astype(BF16)
    hn = hx[hr:hr + tm]
    proj = lambda lhs, j: jnp.dot(lhs, wi_ref[:, j * d:(j + 1) * d], preferred_element_type=F32)

    xce = proj(hx, 0) * proj(hx, 2)
    xc = xce[hr:hr + tm]
    halo_prev = xce[hr - 1:hr]
    halo_next = xce[hr + tm:hr + tm + 1]
    ba, za = proj(hn, 1), proj(hn, 3)
    halo_prev = jnp.where(at_seq_start, 0.0, halo_prev)
    halo_next = jnp.where(at_seq_end, 0.0, halo_next)
    rows = lax.broadcasted_iota(jnp.int32, (tm, 1), 0)
    prev = jnp.where(rows == 0, halo_prev, pltpu.roll(xc, 1, 0))
    nxt = jnp.where(rows == tm - 1, halo_next, pltpu.roll(xc, tm - 1, 0))
    conv = cw_ref[0:1, :] * prev + cw_ref[1:2, :] * xc + cw_ref[2:3, :] * nxt
    ya = ba * conv * _silu(za)

    vb = proj(hn, 5)
    mu = jnp.mean(vb, axis=-1, keepdims=True)
    cen = vb - mu
    var = jnp.mean(cen * cen, axis=-1, keepdims=True)
    vn = (cen * lax.rsqrt(var + SGU_LN_EPS) * lng_ref[...] + lnb_ref[...]).astype(BF16)
    gw = d // SGU_GROUPS
    chunk_rows = []
    for n in range(tm // SGU_CHUNK):
        r0 = n * SGU_CHUNK
        cols = []
        for g in range(SGU_GROUPS):
            cols.append(jnp.dot(sw_ref[g], vn[r0:r0 + SGU_CHUNK, g * gw:(g + 1) * gw],
                                preferred_element_type=F32))
        chunk_rows.append(jnp.concatenate(cols, axis=1) + sb_ref[...])
    mixed = jnp.concatenate(chunk_rows, axis=0)
    yb = proj(hn, 4) * mixed * _silu(proj(hn, 6))

    out = h
    out = out + jnp.dot(ya.astype(BF16), wo_ref[0:d, :], preferred_element_type=F32)
    out = out + jnp.dot(yb.astype(BF16), wo_ref[d:2 * d, :], preferred_element_type=F32)
    o_ref[...] = out


EVEN_ROWS = 512


def _even_layer(h2d, seq, norm_g, w_in, conv_w, ln_g, ln_b, sgu_w, sgu_b, w_out):
    t, d = h2d.shape
    tm = min(EVEN_ROWS, t)
    hr = V7X_SUBLANES
    nblk = t // hr
    rpt = tm // hr
    bias_full = jnp.repeat(sgu_b.T, d // SGU_GROUPS, axis=1)
    sgu_w_bf16 = sgu_w.astype(BF16)
    w_in_bf16 = w_in.astype(BF16)
    w_out_bf16 = w_out.astype(BF16)
    const = lambda a: pl.BlockSpec(a.shape, lambda i: (0,) * a.ndim, pipeline_mode=pl.Buffered(1))
    row = lambda a: a.reshape(1, d)
    consts = (row(norm_g), w_in_bf16, conv_w, row(ln_g), row(ln_b), sgu_w_bf16, bias_full, w_out_bf16)
    return pl.pallas_call(
        functools.partial(_even_kernel, seq),
        grid=(t // tm,),
        in_specs=[
            pl.BlockSpec((tm, d), lambda i: (i, 0)),
            pl.BlockSpec((hr, d), lambda i: (jnp.maximum(i * rpt - 1, 0), 0)),
            pl.BlockSpec((hr, d), lambda i: (jnp.minimum((i + 1) * rpt, nblk - 1), 0)),
        ] + [const(a) for a in consts],
        out_specs=pl.BlockSpec((tm, d), lambda i: (i, 0)),
        out_shape=jax.ShapeDtypeStruct((t, d), F32),
        compiler_params=_cparams(("parallel",)),
        name="even_layer",
    )(h2d, h2d, h2d, *consts)


def _dot_exact_lhs(a_bf16, b, n=3):
    acc = None
    for p in _split(b, n):
        t = jnp.dot(a_bf16, p, preferred_element_type=F32)
        acc = t if acc is None else acc + t
    return acc


def _dot3(a, b):
    ah, al = _split(a, 2)
    bh, bl = _split(b, 2)
    d = lambda x, y: jnp.dot(x, y, preferred_element_type=F32)
    return d(ah, bh) + (d(ah, bl) + d(al, bh))


_NN = (((2,), (1,)), ((0,), (0,)))
_NT = (((2,), (2,)), ((0,), (0,)))


def _bmm(a, b, dims, passes):
    d = lambda x, y: lax.dot_general(x, y, dims, preferred_element_type=F32)
    if passes == 1:
        return d(a.astype(BF16), b.astype(BF16))
    ah, al = _split(a, 2)
    bh, bl = _split(b, 2)
    return d(ah, bh) + (d(ah, bl) + d(al, bh))


def _segsum(x, bd_bf16, n=3):
    w = bd_bf16.shape[0]
    cols = [_dot_exact_rhs(x[:, j:j + w], bd_bf16, n) for j in range(0, x.shape[1], w)]
    return cols[0] if len(cols) == 1 else jnp.concatenate(cols, axis=1)


WKV_CHUNK = 64


WKV_DIR_INPUTS = 12
WKV_DIR_SCRATCH = 9


def _wkv_kernel(*refs):
    ni, ns = WKV_DIR_INPUTS, WKV_DIR_SCRATCH
    shared = refs[2 * ni:2 * ni + 4]
    outs = refs[2 * ni + 4:2 * ni + 8]
    scratch = refs[2 * ni + 8:]

    @pl.when(pl.program_id(2) == 0)
    def _():
        for ref in scratch:
            ref[...] = jnp.zeros_like(ref)

    per_dir = [_wkv_direction(d, refs[d * ni:(d + 1) * ni], shared, outs[2 * d:2 * d + 2],
                              scratch[d * ns:(d + 1) * ns]) for d in range(2)]
    chains = [c for c, _ in per_dir]
    groups = [g for pair in zip(*[gs for _, gs in per_dir]) for g in pair]
    start = {id(g): i * WKV_SKEW for i, g in enumerate(groups)}
    slot = 0
    while groups:
        for gen in list(groups):
            if slot >= start[id(gen)] and next(gen, "done") == "done":
                groups.remove(gen)
        for chain in chains:
            next(chain, None)
        slot += 1
    for chain in chains:
        for _ in chain:
            pass


def _wkv_direction(direction, ins, shared, outs, scratch):
    (r_ref, k_ref, v_ref, wa_ref, mur_ref, muk_ref, muv_ref, muwa_ref, w0_ref, a0_ref, lora_ref,
     tri_ref) = ins
    kk_ref, ka_ref, rk_ref, bd_ref = shared
    y_ref, bon_ref = outs
    h_ref, cr_ref, ck_ref, cv_ref, cwa_ref, lhs_ref, n_ref, y0_ref, bonp_ref = scratch
    rws, gw = r_ref.shape
    L = WKV_CHUNK
    nch = rws // L

    def recurrence():
        h = h_ref[...]
        ys = [None] * nch
        for c in (range(nch) if direction == 0 else range(nch - 1, -1, -1)):
            both = jnp.dot(lhs_ref[c], h.astype(BF16), preferred_element_type=F32)
            ys[c] = both[:L] + y0_ref[c]
            h = both[L:] + n_ref[c]
            yield
        h_ref[...] = h
        y_ref[...] = jnp.concatenate(ys, axis=0).astype(y_ref.dtype)

    chain = recurrence()
    bon_ref[...] = bonp_ref[...].astype(bon_ref.dtype)

    streams = ((r_ref, cr_ref), (k_ref, ck_ref), (v_ref, cv_ref), (wa_ref, cwa_ref))
    edge = [c_ref[...] for _, c_ref in streams]
    far = rws - 1 if direction == 0 else 0
    for x_ref, c_ref in streams:
        c_ref[...] = x_ref[far:far + 1, :]

    ncg = min(WKV_GROUP, nch)
    groups = [_wkv_local_stages(direction, g0, ncg, edge, r_ref, k_ref, v_ref, wa_ref, mur_ref,
                                muk_ref, muv_ref, muwa_ref, w0_ref, a0_ref, kk_ref, ka_ref, rk_ref,
                                lora_ref, tri_ref, bd_ref, lhs_ref, n_ref, y0_ref, bonp_ref, chain)
              for g0 in range(0, nch, ncg)]
    return chain, groups


WKV_GROUP = 8
WKV_SKEW = 0


def _wkv_local_stages(direction, c0, ncg, edge, r_ref, k_ref, v_ref, wa_ref, mur_ref, muk_ref,
                      muv_ref, muwa_ref, w0_ref, a0_ref, kk_ref, ka_ref, rk_ref, lora_ref, tri_ref,
                      bd_ref, lhs_ref, n_ref, y0_ref, bonp_ref, chain):
    L = WKV_CHUNK
    nch = ncg
    rws, gw = r_ref.shape
    nheads = gw // HEAD_DIM
    rg = ncg * L
    row0 = c0 * L
    rows = lax.broadcasted_iota(jnp.int32, (rg, 1), 0)

    def token_shift(x_ref, edge_row, mu):
        x = x_ref[row0:row0 + rg, :]
        if direction == 0:
            nb = edge_row if row0 == 0 else x_ref[row0 - 1:row0, :]
            sh = jnp.where(rows == 0, nb, pltpu.roll(x, 1, 0))
        else:
            nb = edge_row if row0 + rg == rws else x_ref[row0 + rg:row0 + rg + 1, :]
            sh = jnp.where(rows == rg - 1, nb, pltpu.roll(x, rg - 1, 0))
        return x + mu * (sh - x)

    r = token_shift(r_ref, edge[0], mur_ref[...])
    k = token_shift(k_ref, edge[1], muk_ref[...])
    v = token_shift(v_ref, edge[2], muv_ref[...])
    wa = token_shift(wa_ref, edge[3], muwa_ref[...])
    yield

    lane_wa = lax.broadcasted_iota(jnp.int32, (1, wa.shape[1]), 1)
    wa = jnp.where(lane_wa < LORA, jnp.tanh(wa), wa)
    za = _dot(wa, lora_ref[...])
    zw = w0_ref[...] + za[:, :gw]
    a = _sigmoid(a0_ref[...] + za[:, gw:])
    lw = -math.exp(-0.5) * _sigmoid(zw)

    bd = bd_ref[...]
    kk = k * kk_ref[...]
    kk = kk * lax.rsqrt(jnp.maximum(_segsum(kk * kk, bd, 1), 1e-12))
    k2 = k * (1.0 + (a - 1.0) * ka_ref[...])
    bonp_ref[row0:row0 + rg, :] = _segsum(r * k2 * rk_ref[...], bd, 2) * v
    yield

    cs = _dot_exact_lhs(tri_ref[0:rg, 0:rg], lw, 2)
    yield
    e_inc = jnp.exp(cs)
    e_inv = jnp.exp(-cs)
    e_exc = jnp.exp(cs - lw)
    to3 = lambda x: x.reshape(nch, L, gw)
    rt = to3(r * e_inc)
    kt = to3(k2 * e_inv)
    at = to3(-kk * e_exc)
    bt = to3(kk * a * e_inv)
    v3 = to3(v)
    last = L - 1 if direction == 0 else 0
    e_end = to3(e_inc)[:, last:last + 1, :]
    kh = kt * e_end
    bh = bt * e_end

    lane = lax.broadcasted_iota(jnp.int32, (1, 1, gw), 2)
    head_masks = [(lane // HEAD_DIM) == h for h in range(nheads)]

    def blockdiag(x3):
        return jnp.concatenate([jnp.where(m, x3, 0.0) for m in head_masks], axis=1)

    def hmm(lp, *xps):
        rhs = [blockdiag(xp.astype(BF16)) for xp in xps]
        rhs = rhs[0] if len(rhs) == 1 else jnp.concatenate(rhs, axis=2)
        return _bmm(lp, rhs, _NN, 1)

    x_ar = jnp.concatenate([at, rt], axis=1)
    g = _bmm(x_ar, jnp.concatenate([blockdiag(bt.astype(BF16)), blockdiag(kt.astype(BF16))], axis=1),
             _NT, 1)
    g_b, g_k = g[:, :, :gw], g[:, :, gw:]
    yield

    t_idx = lax.broadcasted_iota(jnp.int32, (1, L, gw), 1)
    s_idx = lax.broadcasted_iota(jnp.int32, (1, L, gw), 2) % L
    if direction == 0:
        strict, incl = s_idx < t_idx, s_idx <= t_idx
    else:
        strict, incl = s_idx > t_idx, s_idx >= t_idx
    a_ab = jnp.where(strict, g_b[:, :L], 0.0)
    a_rb = jnp.where(incl, g_b[:, L:], 0.0)
    a_ak = jnp.where(strict, g_k[:, :L], 0.0)
    a_rk = jnp.where(incl, g_k[:, L:], 0.0)

    tinv = jnp.where(s_idx == t_idx, 1.0, 0.0) + a_ab
    apow = hmm(a_ab, a_ab)
    nsq = int(math.log2(L)) - 1
    for i in range(nsq):
        yield
        if i + 1 < nsq:
            both = hmm(apow, apow, tinv)
            apow, tinv = both[:, :, :gw], tinv + both[:, :, gw:]
        else:
            tinv = tinv + hmm(apow, tinv)

    akv = _bmm(jnp.concatenate([a_ak, a_rk], axis=1), blockdiag(v3.astype(BF16)), _NN, 1)
    wu = hmm(tinv, at, akv[:, :L])
    wt, u0 = wu[:, :, :gw], wu[:, :, gw:]
    yield
    ry = hmm(a_rb, wt, u0)
    rh = rt + ry[:, :, :gw]
    y0 = ry[:, :, gw:] + akv[:, L:]

    bk_t = jnp.swapaxes(jnp.concatenate([bh, kh], axis=1), 1, 2)
    rhs = jnp.concatenate([wu, jnp.concatenate([jnp.zeros_like(v3), v3], axis=2)], axis=1)
    mn = _bmm(bk_t, rhs, _NN, 1)
    ri = lax.broadcasted_iota(jnp.int32, (1, gw, gw), 1)
    ci = lax.broadcasted_iota(jnp.int32, (1, gw, gw), 2)
    same_head = (ri // HEAD_DIM) == (ci // HEAD_DIM)
    m_mat = jnp.where(same_head, mn[:, :, :gw], 0.0) + jnp.where(ri == ci, e_end, 0.0)
    yield
    for _ in chain:
        pass
    lhs_ref[c0:c0 + ncg] = jnp.concatenate([rh, m_mat], axis=1).astype(BF16)
    n_ref[c0:c0 + ncg] = jnp.where(same_head, mn[:, :, gw:], 0.0)
    y0_ref[c0:c0 + ncg] = y0


def _wkv(p3, col0, mu, w0, a0, lora_w, k_k, k_a, r_k, gw, rws):
    bsz, seq, _ = p3.shape
    c = w0.shape[-1]
    ng = c // gw
    nstep = seq // rws
    L = WKV_CHUNK
    nch = rws // L
    wa_w = 2 * LORA
    t = np.arange(rws)
    same = (t[:, None] // L) == (t[None, :] // L)
    hd = np.arange(gw) // HEAD_DIM
    bd = jnp.asarray(hd[:, None] == hd[None, :], BF16)
    row = lambda x: x.reshape(1, -1)
    vec = pl.BlockSpec((1, gw), lambda b, g, s: (0, g))
    full2 = lambda shape: pl.BlockSpec(shape, lambda b, g, s: (0, 0))

    in_specs, args, out_specs, scratch = [], [], [], []
    for direction in range(2):
        blk = (lambda s: s) if direction == 0 else (lambda s: nstep - 1 - s)
        rb = lambda s, blk=blk: blk(jnp.minimum(s, nstep - 1))
        ob = lambda s, blk=blk: blk(jnp.maximum(s - 1, 0))
        stream = lambda off, rb=rb: pl.BlockSpec((None, rws, gw),
                                                 lambda b, g, s: (b, rb(s), off // gw + g))
        tri = same & ((t[None, :] <= t[:, None]) if direction == 0 else (t[None, :] >= t[:, None]))
        mu_d = mu[direction]
        in_specs += [
            stream(col0), stream(col0 + c), stream(col0 + 2 * c),
            pl.BlockSpec((None, rws, wa_w),
                         lambda b, g, s, rb=rb: (b, rb(s), (col0 + 3 * c) // wa_w)),
            vec, vec, vec, full2((1, wa_w)), vec, vec,
            pl.BlockSpec((None, wa_w, 2 * gw), lambda b, g, s: (g, 0, 0)),
            full2((rws, rws)),
        ]
        args += [p3, p3, p3, p3, row(mu_d[:c]), row(mu_d[c:2 * c]), row(mu_d[2 * c:3 * c]),
                 row(mu_d[3 * c:]), row(w0[direction]), row(a0[direction]), lora_w[direction],
                 jnp.asarray(tri, BF16)]
        out_specs += [pl.BlockSpec((None, rws, gw), lambda b, g, s, ob=ob: (b, ob(s), g))] * 2
        scratch += [pltpu.VMEM((gw, gw), F32), pltpu.VMEM((1, gw), F32), pltpu.VMEM((1, gw), F32),
                    pltpu.VMEM((1, gw), F32), pltpu.VMEM((1, wa_w), F32),
                    pltpu.VMEM((nch, L + gw, gw), BF16), pltpu.VMEM((nch, gw, gw), F32),
                    pltpu.VMEM((nch, L, gw), F32), pltpu.VMEM((rws, gw), F32)]
    assert len(in_specs) == 2 * WKV_DIR_INPUTS and len(scratch) == 2 * WKV_DIR_SCRATCH
    in_specs += [vec, vec, vec, full2((gw, gw))]
    args += [row(k_k), row(k_a), row(r_k), bd]
    return pl.pallas_call(
        _wkv_kernel,
        grid=(bsz, ng, nstep + 1),
        in_specs=in_specs,
        out_specs=out_specs,
        out_shape=[jax.ShapeDtypeStruct((bsz, seq, c), BF16)] * 4,
        scratch_shapes=scratch,
        compiler_params=_cparams(("parallel", "parallel", "arbitrary")),
        name="wkv",
    )(*args)


def _lora_weights(w2, a2, gw):
    nd, lo, c = w2.shape
    ng = c // gw
    w2g = w2.reshape(nd, lo, ng, gw).transpose(0, 2, 1, 3)
    a2g = a2.reshape(nd, lo, ng, gw).transpose(0, 2, 1, 3)
    z = jnp.zeros_like(w2g)
    top = jnp.concatenate([w2g, z], axis=3)
    bot = jnp.concatenate([z, a2g], axis=3)
    return jnp.concatenate([top, bot], axis=2).astype(BF16)


FNET_N2 = 128


FNET_PITCH_PAD = 8
FNET_UNROLL = 8


def _cos_sin(idx, period):
    ang = 2.0 * np.pi * (idx % period) / period
    return np.cos(ang), np.sin(ang)


def _hi_lo(a):
    hi = jnp.asarray(a, F32).astype(BF16)
    lo = (jnp.asarray(a, F32) - hi.astype(F32)).astype(BF16)
    return hi, lo


def _dot3_split(ah, al, b):
    bh, bl = _split(b, 2)
    d = lambda x, y: jnp.dot(x, y, preferred_element_type=F32)
    return d(ah, bh) + (d(ah, bl) + d(al, bh))


def _fnet_weight_kernel(scale, c_ref, s_ref, w_ref, o_ref):
    for g in range(w_ref.shape[0]):
        wc = _dot3(c_ref[...], w_ref[g])
        ws = _dot3(s_ref[...], w_ref[g])
        o_ref[g] = jnp.concatenate([wc, -ws], axis=1) * scale


def _fnet_kernel(n1, n2, f_ref, w_ref, m1h_ref, m1l_ref, m2h_ref, m2l_ref, o_ref,
                 gr_ref, gi_ref, zr_ref, zi_ref):
    gd = f_ref.shape[1]
    pitch = n2 + FNET_PITCH_PAD
    wh, wl = _split(w_ref[...], 2)

    def chan_body(s1, carry):
        src = pl.ds(pl.multiple_of(s1 * n2, n2), n2)
        dst = pl.ds(pl.multiple_of(s1 * pitch, V7X_SUBLANES), n2)
        fh, fl = _split(f_ref[src, :], 2)
        d = lambda x, y: jnp.dot(x, y, preferred_element_type=F32)
        z = d(fh, wh) + (d(fh, wl) + d(fl, wh))
        gr_ref[dst, :] = z[:, :gd]
        gi_ref[dst, :] = z[:, gd:]
        return carry

    lax.fori_loop(0, n1, chan_body, 0, unroll=min(FNET_UNROLL, n1))

    m1h, m1l = m1h_ref[...], m1l_ref[...]

    def stage1_body(s2, carry):
        idx = pl.ds(s2, n1, stride=pitch)
        x = jnp.concatenate([gr_ref[idx, :], gi_ref[idx, :]], axis=0)
        z = _dot3_split(m1h, m1l, x)
        zr_ref[idx, :] = z[:n1]
        zi_ref[idx, :] = z[n1:]
        return carry

    lax.fori_loop(0, n2, stage1_body, 0, unroll=2 * FNET_UNROLL)

    def stage2_body(s1, carry):
        src = pl.ds(pl.multiple_of(s1 * pitch, V7X_SUBLANES), n2)
        zz = jnp.concatenate([zr_ref[src, :], zi_ref[src, :]], axis=0)
        o_ref[pl.ds(s1, n2, stride=n1), :] = _dot3_split(m2h_ref[s1], m2l_ref[s1], zz)
        return carry

    lax.fori_loop(0, n1, stage2_body, 0, unroll=min(FNET_UNROLL, n1))


def _fnet(p3, col0, w_f):
    bsz, seq, _ = p3.shape
    ng, gd, _ = w_f.shape
    n2 = FNET_N2
    n1 = seq // n2
    scale = 1.0 / math.sqrt(seq * gd)
    ar = np.arange
    cd, sd = _cos_sin(ar(gd)[:, None] * ar(gd)[None, :], gd)
    c1, s1 = _cos_sin(ar(n1)[:, None] * ar(n1)[None, :], n1)
    m1 = np.block([[c1, s1], [-s1, c1]])
    c2, s2 = _cos_sin(ar(n2)[None, None, :] * (ar(n1)[:, None, None] + n1 * ar(n2)[None, :, None]), seq)
    m2 = np.concatenate([c2, s2], axis=2)
    m1h, m1l = _hi_lo(m1)
    m2h, m2l = _hi_lo(m2)

    wcat = pl.pallas_call(
        functools.partial(_fnet_weight_kernel, scale),
        out_shape=jax.ShapeDtypeStruct((ng, gd, 2 * gd), F32),
        name="fnet_weights",
    )(jnp.asarray(cd, F32), jnp.asarray(sd, F32), w_f)

    pitch_rows = n1 * (n2 + FNET_PITCH_PAD)
    const2 = lambda a: pl.BlockSpec(a.shape, lambda b, g: (0, 0))
    const3 = lambda a: pl.BlockSpec(a.shape, lambda b, g: (0, 0, 0), pipeline_mode=pl.Buffered(1))
    return pl.pallas_call(
        functools.partial(_fnet_kernel, n1, n2),
        grid=(bsz, ng),
        in_specs=[pl.BlockSpec((None, seq, gd), lambda b, g: (b, 0, col0 // gd + g)),
                  pl.BlockSpec((None, gd, 2 * gd), lambda b, g: (g, 0, 0)),
                  const2(m1h), const2(m1l), const3(m2h), const3(m2l)],
        out_specs=pl.BlockSpec((None, seq, gd), lambda b, g: (b, 0, g)),
        out_shape=jax.ShapeDtypeStruct((bsz, seq, ng * gd), F32),
        scratch_shapes=[pltpu.VMEM((pitch_rows, gd), F32)] * 4,
        compiler_params=_cparams(("parallel", "parallel")),
        name="fnet",
    )(p3, wcat, m1h, m1l, m2h, m2l)


def _odd_out_kernel(y0_ref, y1_ref, b0_ref, b1_ref, zc_ref, fn_ref, zd_ref, h_ref, lng_ref,
                    lnb_ref, bd_ref, wo_ref, fg_ref, o_ref):
    c = y0_ref.shape[1]
    bd = bd_ref[...]
    inv_n = 1.0 / HEAD_DIM
    ysum = y0_ref[...].astype(F32) + y1_ref[...].astype(F32)
    mean = _segsum(ysum, bd) * inv_n
    cen = ysum - mean
    var = _segsum(cen * cen, bd) * inv_n
    gn = cen * lax.rsqrt(var + GN_EPS) * lng_ref[...] + lnb_ref[...]
    yc = (gn + b0_ref[...].astype(F32) + b1_ref[...].astype(F32)) * _silu(zc_ref[...])
    yd = fn_ref[...] * _silu(zd_ref[...])
    out = h_ref[...]
    out = out + jnp.dot(yc.astype(BF16), wo_ref[0:c, :], preferred_element_type=F32)
    out = out + jnp.dot(yd.astype(BF16), wo_ref[c:, :], preferred_element_type=F32)
    ms = jnp.mean(out * out, axis=-1, keepdims=True)
    o_ref[...] = out * lax.rsqrt(ms + RMS_EPS) * fg_ref[...]


def _odd_out(y0, y1, b0, b1, p2d, zc_blk, fn, zd_blk, h2d, lnx_g, lnx_b, w_out_bf16, final_g, tm):
    t, d = h2d.shape
    c = y0.shape[1]
    cf = fn.shape[1]
    hd = np.arange(2 * HEAD_DIM) // HEAD_DIM
    bd = jnp.asarray(hd[:, None] == hd[None, :], BF16)
    rowblk = lambda w: pl.BlockSpec((tm, w), lambda i: (i, 0))
    full = lambda shape: pl.BlockSpec(shape, lambda i: (0, 0))
    return pl.pallas_call(
        _odd_out_kernel,
        grid=(t // tm,),
        in_specs=[rowblk(c), rowblk(c), rowblk(c), rowblk(c),
                  pl.BlockSpec((tm, c), lambda i: (i, zc_blk)),
                  rowblk(cf),
                  pl.BlockSpec((tm, cf), lambda i: (i, zd_blk)),
                  rowblk(d), full((1, c)), full((1, c)), full(bd.shape),
                  full(w_out_bf16.shape), full((1, d))],
        out_specs=rowblk(d),
        out_shape=jax.ShapeDtypeStruct((t, d), F32),
        compiler_params=_cparams(("parallel",)),
        name="odd_out",
    )(y0, y1, b0, b1, p2d, fn, p2d, h2d, lnx_g.reshape(1, c), lnx_b.reshape(1, c), bd,
      w_out_bf16, final_g.reshape(1, d))


WKV_LANES = 128
WKV_ROWS = 512


def _odd_layer(h2d, bsz, seq, norm_g, w_in, mu, w0, w2, a0, a2, k_k, k_a, r_k, lnx_g, lnx_b,
               fnet_w, w_out, final_g):
    t, d = h2d.shape
    c = w0.shape[-1]
    rs = 3 * c + 2 * LORA
    cf = fnet_w.shape[0] * fnet_w.shape[1]
    w_perm = jnp.concatenate([w_in[:, rs:], w_in[:, :rs]], axis=1).astype(BF16)
    p2 = _norm_proj(h2d, norm_g, w_perm, tm=min(256, t), tn=w_perm.shape[1], out_dtype=F32)
    col0 = c + 2 * cf
    p3 = p2.reshape(bsz, seq, -1)
    lora = _lora_weights(w2, a2, WKV_LANES)
    rws = min(WKV_ROWS, seq)
    y0, b0, y1, b1 = _wkv(p3, col0, mu, w0, a0, lora, k_k, k_a, r_k.reshape(-1), WKV_LANES, rws)
    fn = _fnet(p3, c, fnet_w).reshape(t, cf)
    flat = lambda a: a.reshape(t, c)
    return _odd_out(flat(y0), flat(y1), flat(b0), flat(b1), p2, 0, fn, c // cf + 1, h2d,
                    lnx_g, lnx_b, w_out.astype(BF16), final_g, tm=min(256, t))


def kernel(x, e_norm_g, e_w_in, e_conv_w, e_sgu_ln_g, e_sgu_ln_b, e_sgu_w, e_sgu_b, e_w_out,
           o_norm_g, o_w_in, o_mu, o_w0, o_w2, o_a0, o_a2, o_k_k, o_k_a, o_r_k, o_lnx_g, o_lnx_b,
           o_fnet_w, o_w_out, final_norm_g):
    bsz, seq, d = x.shape
    assert e_norm_g.shape[0] == 1 and o_norm_g.shape[0] == 1, "two-layer trunk: one even, one odd layer"
    h = x.reshape(bsz * seq, d)
    h = _even_layer(h, seq, e_norm_g[0], e_w_in[0], e_conv_w[0], e_sgu_ln_g[0], e_sgu_ln_b[0],
                    e_sgu_w[0], e_sgu_b[0], e_w_out[0])
    out = _odd_layer(h, bsz, seq, o_norm_g[0], o_w_in[0], o_mu[0], o_w0[0], o_w2[0], o_a0[0],
                     o_a2[0], o_k_k[0], o_k_a[0], o_r_k[0], o_lnx_g[0], o_lnx_b[0], o_fnet_w[0],
                     o_w_out[0], final_norm_g)
    return out.reshape(bsz, seq, d)
```

```python
import functools
import math

import numpy as np
import jax
import jax.numpy as jnp
from jax import lax
from jax.experimental import pallas as pl
from jax.experimental.pallas import tpu as pltpu

F32 = jnp.float32
BF16 = jnp.bfloat16

RMS_EPS = 1e-6
SGU_LN_EPS = 1e-5
GN_EPS = 64e-5
SGU_CHUNK = 128
SGU_GROUPS = 8
HEAD_DIM = 64
LORA = 64
FNET_GROUPS = 4
FNET_GROUP_DIM = 128

V7X_LANES = 128
V7X_SUBLANES = 8
V7X_VMEM_BYTES = 64 * 1024 * 1024
VMEM_LIMIT = V7X_VMEM_BYTES - 8 * 1024 * 1024


def _cparams(sem):
    return pltpu.CompilerParams(dimension_semantics=sem, vmem_limit_bytes=VMEM_LIMIT)


def _silu(z):
    return z * (1.0 / (1.0 + jnp.exp(-z)))


def _sigmoid(z):
    return 1.0 / (1.0 + jnp.exp(-z))


def _dot(a, b):
    return jnp.dot(a.astype(BF16), b.astype(BF16), preferred_element_type=F32)


def _split(a, n):
    parts = []
    rem = a
    for _ in range(n):
        p = rem.astype(BF16)
        parts.append(p)
        rem = rem - p.astype(F32)
    return parts


def _dot_exact_rhs(a, b_bf16, n=3):
    acc = None
    for p in _split(a, n):
        t = jnp.dot(p, b_bf16, preferred_element_type=F32)
        acc = t if acc is None else acc + t
    return acc


def _proj_kernel(x_ref, g_ref, w_ref, o_ref, hn_ref):
    @pl.when(pl.program_id(1) == 0)
    def _():
        x = x_ref[...]
        ms = jnp.mean(x * x, axis=-1, keepdims=True)
        hn_ref[...] = (x * lax.rsqrt(ms + RMS_EPS) * g_ref[...]).astype(BF16)

    o_ref[...] = jnp.dot(hn_ref[...], w_ref[...], preferred_element_type=F32).astype(o_ref.dtype)


def _norm_proj(x2d, g, w_bf16, tm, tn, out_dtype):
    t, d = x2d.shape
    n = w_bf16.shape[1]
    assert t % tm == 0 and n % tn == 0
    w_mode = dict(pipeline_mode=pl.Buffered(1)) if n == tn else {}
    return pl.pallas_call(
        _proj_kernel,
        grid=(t // tm, n // tn),
        in_specs=[
            pl.BlockSpec((tm, d), lambda i, j: (i, 0)),
            pl.BlockSpec((1, d), lambda i, j: (0, 0)),
            pl.BlockSpec((d, tn), lambda i, j: (0, j), **w_mode),
        ],
        out_specs=pl.BlockSpec((tm, tn), lambda i, j: (i, j)),
        out_shape=jax.ShapeDtypeStruct((t, n), out_dtype),
        scratch_shapes=[pltpu.VMEM((tm, d), BF16)],
        compiler_params=_cparams(("parallel", "arbitrary")),
        name="norm_proj",
    )(x2d, g.reshape(1, d), w_bf16)


def _even_kernel(seq, h_ref, hp_ref, hn_ref, ng_ref, wi_ref, cw_ref, lng_ref, lnb_ref, sw_ref,
                 sb_ref, wo_ref, o_ref):
    tm, d = h_ref.shape
    hr = hp_ref.shape[0]
    row0 = pl.program_id(0) * tm
    at_seq_start = (row0 % seq) == 0
    at_seq_end = ((row0 + tm) % seq) == 0

    h = h_ref[...]
    hx = jnp.concatenate([hp_ref[...], h, hn_ref[...]], axis=0)
    ms = jnp.mean(hx * hx, axis=-1, keepdims=True)
    hx = (hx * lax.rsqrt(ms + RMS_EPS) * ng_ref[...]).astype(BF16)
    hn = hx[hr:hr + tm]
    proj = lambda lhs, j: jnp.dot(lhs, wi_ref[:, j * d:(j + 1) * d], preferred_element_type=F32)

    xce = proj(hx, 0) * proj(hx, 2)
    xc = xce[hr:hr + tm]
    halo_prev = xce[hr - 1:hr]
    halo_next = xce[hr + tm:hr + tm + 1]
    ba, za = proj(hn, 1), proj(hn, 3)
    halo_prev = jnp.where(at_seq_start, 0.0, halo_prev)
    halo_next = jnp.where(at_seq_end, 0.0, halo_next)
    rows = lax.broadcasted_iota(jnp.int32, (tm, 1), 0)
    prev = jnp.where(rows == 0, halo_prev, pltpu.roll(xc, 1, 0))
    nxt = jnp.where(rows == tm - 1, halo_next, pltpu.roll(xc, tm - 1, 0))
    conv = cw_ref[0:1, :] * prev + cw_ref[1:2, :] * xc + cw_ref[2:3, :] * nxt
    ya = ba * conv * _silu(za)

    vb = proj(hn, 5)
    mu = jnp.mean(vb, axis=-1, keepdims=True)
    cen = vb - mu
    var = jnp.mean(cen * cen, axis=-1, keepdims=True)
    vn = (cen * lax.rsqrt(var + SGU_LN_EPS) * lng_ref[...] + lnb_ref[...]).astype(BF16)
    gw = d // SGU_GROUPS
    chunk_rows = []
    for n in range(tm // SGU_CHUNK):
        r0 = n * SGU_CHUNK
        cols = []
        for g in range(SGU_GROUPS):
            cols.append(jnp.dot(sw_ref[g], vn[r0:r0 + SGU_CHUNK, g * gw:(g + 1) * gw],
                                preferred_element_type=F32))
        chunk_rows.append(jnp.concatenate(cols, axis=1) + sb_ref[...])
    mixed = jnp.concatenate(chunk_rows, axis=0)
    yb = proj(hn, 4) * mixed * _silu(proj(hn, 6))

    out = h
    out = out + jnp.dot(ya.astype(BF16), wo_ref[0:d, :], preferred_element_type=F32)
    out = out + jnp.dot(yb.astype(BF16), wo_ref[d:2 * d, :], preferred_element_type=F32)
    o_ref[...] = out


EVEN_ROWS = 512


def _even_layer(h2d, seq, norm_g, w_in, conv_w, ln_g, ln_b, sgu_w, sgu_b, w_out):
    t, d = h2d.shape
    tm = min(EVEN_ROWS, t)
    hr = V7X_SUBLANES
    nblk = t // hr
    rpt = tm // hr
    bias_full = jnp.repeat(sgu_b.T, d // SGU_GROUPS, axis=1)
    sgu_w_bf16 = sgu_w.astype(BF16)
    w_in_bf16 = w_in.astype(BF16)
    w_out_bf16 = w_out.astype(BF16)
    const = lambda a: pl.BlockSpec(a.shape, lambda i: (0,) * a.ndim, pipeline_mode=pl.Buffered(1))
    row = lambda a: a.reshape(1, d)
    consts = (row(norm_g), w_in_bf16, conv_w, row(ln_g), row(ln_b), sgu_w_bf16, bias_full, w_out_bf16)
    return pl.pallas_call(
        functools.partial(_even_kernel, seq),
        grid=(t // tm,),
        in_specs=[
            pl.BlockSpec((tm, d), lambda i: (i, 0)),
            pl.BlockSpec((hr, d), lambda i: (jnp.maximum(i * rpt - 1, 0), 0)),
            pl.BlockSpec((hr, d), lambda i: (jnp.minimum((i + 1) * rpt, nblk - 1), 0)),
        ] + [const(a) for a in consts],
        out_specs=pl.BlockSpec((tm, d), lambda i: (i, 0)),
        out_shape=jax.ShapeDtypeStruct((t, d), F32),
        compiler_params=_cparams(("parallel",)),
        name="even_layer",
    )(h2d, h2d, h2d, *consts)


def _dot_exact_lhs(a_bf16, b, n=3):
    acc = None
    for p in _split(b, n):
        t = jnp.dot(a_bf16, p, preferred_element_type=F32)
        acc = t if acc is None else acc + t
    return acc


def _dot3(a, b):
    ah, al = _split(a, 2)
    bh, bl = _split(b, 2)
    d = lambda x, y: jnp.dot(x, y, preferred_element_type=F32)
    return d(ah, bh) + (d(ah, bl) + d(al, bh))


_NN = (((2,), (1,)), ((0,), (0,)))
_NT = (((2,), (2,)), ((0,), (0,)))


def _bmm(a, b, dims, passes):
    d = lambda x, y: lax.dot_general(x, y, dims, preferred_element_type=F32)
    if passes == 1:
        return d(a.astype(BF16), b.astype(BF16))
    ah, al = _split(a, 2)
    bh, bl = _split(b, 2)
    return d(ah, bh) + (d(ah, bl) + d(al, bh))


def _segsum(x, bd_bf16, n=3):
    w = bd_bf16.shape[0]
    cols = [_dot_exact_rhs(x[:, j:j + w], bd_bf16, n) for j in range(0, x.shape[1], w)]
    return cols[0] if len(cols) == 1 else jnp.concatenate(cols, axis=1)


WKV_CHUNK = 64


WKV_DIR_INPUTS = 12
WKV_DIR_SCRATCH = 9


def _wkv_kernel(*refs):
    ni, ns = WKV_DIR_INPUTS, WKV_DIR_SCRATCH
    shared = refs[2 * ni:2 * ni + 4]
    outs = refs[2 * ni + 4:2 * ni + 8]
    scratch = refs[2 * ni + 8:]

    step = pl.program_id(2)
    last = pl.num_programs(2) - 1

    @pl.when(step == 0)
    def _():
        for ref in scratch:
            ref[...] = jnp.zeros_like(ref)

    def body(with_local):
        per_dir = [_wkv_direction(d, refs[d * ni:(d + 1) * ni], shared, outs[2 * d:2 * d + 2],
                                  scratch[d * ns:(d + 1) * ns], with_local) for d in range(2)]
        chains = [c for c, _ in per_dir]
        groups = [g for pair in zip(*[gs for _, gs in per_dir]) for g in pair]
        start = {id(g): i * WKV_SKEW for i, g in enumerate(groups)}
        slot = 0
        while groups:
            for gen in list(groups):
                if slot >= start[id(gen)] and next(gen, "done") == "done":
                    groups.remove(gen)
            for chain in chains:
                next(chain, None)
            slot += 1
        for chain in chains:
            for _ in chain:
                pass

    pl.when(step < last)(functools.partial(body, True))
    pl.when(step == last)(functools.partial(body, False))


def _wkv_direction(direction, ins, shared, outs, scratch, with_local):
    (r_ref, k_ref, v_ref, wa_ref, mur_ref, muk_ref, muv_ref, muwa_ref, w0_ref, a0_ref, lora_ref,
     tri_ref) = ins
    kk_ref, ka_ref, rk_ref, bd_ref = shared
    y_ref, bon_ref = outs
    h_ref, cr_ref, ck_ref, cv_ref, cwa_ref, lhs_ref, n_ref, y0_ref, bonp_ref = scratch
    rws, gw = r_ref.shape
    L = WKV_CHUNK
    nch = rws // L

    ri = lax.broadcasted_iota(jnp.int32, (gw, gw), 0)
    ci = lax.broadcasted_iota(jnp.int32, (gw, gw), 1)
    same_head = (ri // HEAD_DIM) == (ci // HEAD_DIM)

    def recurrence():
        h = h_ref[...]
        ys = [None] * nch
        for c in (range(nch) if direction == 0 else range(nch - 1, -1, -1)):
            both = jnp.dot(lhs_ref[c], h.astype(BF16), preferred_element_type=F32)
            ys[c] = both[:L] + y0_ref[c]
            h = jnp.where(same_head, both[L:] + n_ref[c], 0.0)
            yield
        h_ref[...] = h
        y_ref[...] = jnp.concatenate(ys, axis=0).astype(y_ref.dtype)

    chain = recurrence()
    bon_ref[...] = bonp_ref[...].astype(bon_ref.dtype)
    if not with_local:
        return chain, []

    streams = ((r_ref, cr_ref), (k_ref, ck_ref), (v_ref, cv_ref), (wa_ref, cwa_ref))
    edge = [c_ref[...] for _, c_ref in streams]
    far = rws - 1 if direction == 0 else 0
    for x_ref, c_ref in streams:
        c_ref[...] = x_ref[far:far + 1, :]

    ncg = min(WKV_GROUP, nch)
    groups = [_wkv_local_stages(direction, g0, ncg, edge, r_ref, k_ref, v_ref, wa_ref, mur_ref,
                                muk_ref, muv_ref, muwa_ref, w0_ref, a0_ref, kk_ref, ka_ref, rk_ref,
                                lora_ref, tri_ref, bd_ref, lhs_ref, n_ref, y0_ref, bonp_ref, chain)
              for g0 in range(0, nch, ncg)]
    return chain, groups


WKV_GROUP = 8
WKV_SKEW = 0


def _wkv_local_stages(direction, c0, ncg, edge, r_ref, k_ref, v_ref, wa_ref, mur_ref, muk_ref,
                      muv_ref, muwa_ref, w0_ref, a0_ref, kk_ref, ka_ref, rk_ref, lora_ref, tri_ref,
                      bd_ref, lhs_ref, n_ref, y0_ref, bonp_ref, chain):
    L = WKV_CHUNK
    nch = ncg
    rws, gw = r_ref.shape
    nheads = gw // HEAD_DIM
    rg = ncg * L
    row0 = c0 * L
    rows = lax.broadcasted_iota(jnp.int32, (rg, 1), 0)

    def token_shift(x_ref, edge_row, mu):
        x = x_ref[row0:row0 + rg, :]
        if direction == 0:
            nb = edge_row if row0 == 0 else x_ref[row0 - 1:row0, :]
            sh = jnp.where(rows == 0, nb, pltpu.roll(x, 1, 0))
        else:
            nb = edge_row if row0 + rg == rws else x_ref[row0 + rg:row0 + rg + 1, :]
            sh = jnp.where(rows == rg - 1, nb, pltpu.roll(x, rg - 1, 0))
        return x + mu * (sh - x)

    r = token_shift(r_ref, edge[0], mur_ref[...])
    k = token_shift(k_ref, edge[1], muk_ref[...])
    v = token_shift(v_ref, edge[2], muv_ref[...])
    wa = token_shift(wa_ref, edge[3], muwa_ref[...])
    yield

    lane_wa = lax.broadcasted_iota(jnp.int32, (1, wa.shape[1]), 1)
    wa = jnp.where(lane_wa < LORA, jnp.tanh(wa), wa)
    za = _dot(wa, lora_ref[...])
    zw = w0_ref[...] + za[:, :gw]
    a = _sigmoid(a0_ref[...] + za[:, gw:])
    lw = -math.exp(-0.5) * _sigmoid(zw)

    bd = bd_ref[...]
    kk = k * kk_ref[...]
    kk = kk * lax.rsqrt(jnp.maximum(_segsum(kk * kk, bd, 1), 1e-12))
    k2 = k * (1.0 + (a - 1.0) * ka_ref[...])
    bonp_ref[row0:row0 + rg, :] = _segsum(r * k2 * rk_ref[...], bd, 1) * v
    yield

    tri = jnp.broadcast_to(tri_ref[...], (nch, L, L))
    cs = sum(lax.dot_general(tri, part.reshape(nch, L, gw), _NN, preferred_element_type=F32)
             for part in _split(lw, 2)).reshape(rg, gw)
    yield
    e_inc = jnp.exp(cs)
    e_inv = jnp.exp(-cs)
    e_exc = jnp.exp(cs - lw)
    to3 = lambda x: x.reshape(nch, L, gw)
    rt = to3(r * e_inc)
    kt = to3(k2 * e_inv)
    at = to3(-kk * e_exc)
    bt = to3(kk * a * e_inv)
    v3 = to3(v)
    last = L - 1 if direction == 0 else 0
    e_end = to3(e_inc)[:, last:last + 1, :]
    kh = kt * e_end
    bh = bt * e_end

    lane = lax.broadcasted_iota(jnp.int32, (1, 1, gw), 2)
    head_masks = [(lane // HEAD_DIM) == h for h in range(nheads)]

    def blockdiag(x3):
        return jnp.concatenate([jnp.where(m, x3, 0.0) for m in head_masks], axis=1)

    def hmm(lp, *xps):
        rhs = [blockdiag(xp.astype(BF16)) for xp in xps]
        rhs = rhs[0] if len(rhs) == 1 else jnp.concatenate(rhs, axis=2)
        return _bmm(lp, rhs, _NN, 1)

    x_ar = jnp.concatenate([at, rt], axis=1)
    g = _bmm(x_ar, jnp.concatenate([blockdiag(bt.astype(BF16)), blockdiag(kt.astype(BF16))], axis=1),
             _NT, 1)
    g_b, g_k = g[:, :, :gw], g[:, :, gw:]
    yield

    t_idx = lax.broadcasted_iota(jnp.int32, (1, L, gw), 1)
    s_idx = lax.broadcasted_iota(jnp.int32, (1, L, gw), 2) % L
    if direction == 0:
        strict, incl = s_idx < t_idx, s_idx <= t_idx
    else:
        strict, incl = s_idx > t_idx, s_idx >= t_idx
    a_ab = jnp.where(strict, g_b[:, :L], 0.0)
    a_rb = jnp.where(incl, g_b[:, L:], 0.0)
    a_ak = jnp.where(strict, g_k[:, :L], 0.0)
    a_rk = jnp.where(incl, g_k[:, L:], 0.0)

    tinv = jnp.where(s_idx == t_idx, 1.0, 0.0) + a_ab
    apow = hmm(a_ab, a_ab)
    nsq = int(math.log2(L)) - 1
    for i in range(nsq):
        yield
        if i + 1 < nsq:
            both = hmm(apow, apow, tinv)
            apow, tinv = both[:, :, :gw], tinv + both[:, :, gw:]
        else:
            tinv = tinv + hmm(apow, tinv)

    akv = _bmm(jnp.concatenate([a_ak, a_rk], axis=1), blockdiag(v3.astype(BF16)), _NN, 1)
    wu = hmm(tinv, at, akv[:, :L])
    wt, u0 = wu[:, :, :gw], wu[:, :, gw:]
    yield
    ry = hmm(a_rb, wt, u0)
    rh = rt + ry[:, :, :gw]
    y0 = ry[:, :, gw:] + akv[:, L:]

    bk_t = jnp.swapaxes(jnp.concatenate([bh, kh], axis=1), 1, 2)
    rhs = jnp.concatenate([wu, jnp.concatenate([jnp.zeros_like(v3), v3], axis=2)], axis=1)
    mn = _bmm(bk_t, rhs, _NN, 1)
    ri = lax.broadcasted_iota(jnp.int32, (1, gw, gw), 1)
    ci = lax.broadcasted_iota(jnp.int32, (1, gw, gw), 2)
    m_mat = mn[:, :, :gw] + jnp.where(ri == ci, e_end, 0.0)
    yield
    for _ in chain:
        pass
    lhs_ref[c0:c0 + ncg] = jnp.concatenate([rh, m_mat], axis=1).astype(BF16)
    n_ref[c0:c0 + ncg] = mn[:, :, gw:]
    y0_ref[c0:c0 + ncg] = y0


def _wkv(p3, col0, mu, w0, a0, lora_w, k_k, k_a, r_k, gw, rws):
    bsz, seq, _ = p3.shape
    c = w0.shape[-1]
    ng = c // gw
    nstep = seq // rws
    L = WKV_CHUNK
    nch = rws // L
    wa_w = 2 * LORA
    t = np.arange(L)
    hd = np.arange(gw) // HEAD_DIM
    bd = jnp.asarray(hd[:, None] == hd[None, :], BF16)
    row = lambda x: x.reshape(1, -1)
    vec = pl.BlockSpec((1, gw), lambda b, g, s: (0, g))
    full2 = lambda shape: pl.BlockSpec(shape, lambda b, g, s: (0, 0))

    in_specs, args, out_specs, scratch = [], [], [], []
    for direction in range(2):
        blk = (lambda s: s) if direction == 0 else (lambda s: nstep - 1 - s)
        rb = lambda s, blk=blk: blk(jnp.minimum(s, nstep - 1))
        ob = lambda s, blk=blk: blk(jnp.maximum(s - 1, 0))
        stream = lambda off, rb=rb: pl.BlockSpec((None, rws, gw),
                                                 lambda b, g, s: (b, rb(s), off // gw + g))
        tri = (t[None, :] <= t[:, None]) if direction == 0 else (t[None, :] >= t[:, None])
        mu_d = mu[direction]
        in_specs += [
            stream(col0), stream(col0 + c), stream(col0 + 2 * c),
            pl.BlockSpec((None, rws, wa_w),
                         lambda b, g, s, rb=rb: (b, rb(s), (col0 + 3 * c) // wa_w)),
            vec, vec, vec, full2((1, wa_w)), vec, vec,
            pl.BlockSpec((None, wa_w, 2 * gw), lambda b, g, s: (g, 0, 0)),
            full2((L, L)),
        ]
        args += [p3, p3, p3, p3, row(mu_d[:c]), row(mu_d[c:2 * c]), row(mu_d[2 * c:3 * c]),
                 row(mu_d[3 * c:]), row(w0[direction]), row(a0[direction]), lora_w[direction],
                 jnp.asarray(tri, BF16)]
        out_specs += [pl.BlockSpec((None, rws, gw), lambda b, g, s, ob=ob: (b, ob(s), g))] * 2
        scratch += [pltpu.VMEM((gw, gw), F32), pltpu.VMEM((1, gw), F32), pltpu.VMEM((1, gw), F32),
                    pltpu.VMEM((1, gw), F32), pltpu.VMEM((1, wa_w), F32),
                    pltpu.VMEM((nch, L + gw, gw), BF16), pltpu.VMEM((nch, gw, gw), F32),
                    pltpu.VMEM((nch, L, gw), F32), pltpu.VMEM((rws, gw), F32)]
    assert len(in_specs) == 2 * WKV_DIR_INPUTS and len(scratch) == 2 * WKV_DIR_SCRATCH
    in_specs += [vec, vec, vec, full2((gw, gw))]
    args += [row(k_k), row(k_a), row(r_k), bd]
    return pl.pallas_call(
        _wkv_kernel,
        grid=(bsz, ng, nstep + 1),
        in_specs=in_specs,
        out_specs=out_specs,
        out_shape=[jax.ShapeDtypeStruct((bsz, seq, c), BF16)] * 4,
        scratch_shapes=scratch,
        compiler_params=_cparams(("parallel", "parallel", "arbitrary")),
        name="wkv",
    )(*args)


def _lora_weights(w2, a2, gw):
    nd, lo, c = w2.shape
    ng = c // gw
    w2g = w2.reshape(nd, lo, ng, gw).transpose(0, 2, 1, 3)
    a2g = a2.reshape(nd, lo, ng, gw).transpose(0, 2, 1, 3)
    z = jnp.zeros_like(w2g)
    top = jnp.concatenate([w2g, z], axis=3)
    bot = jnp.concatenate([z, a2g], axis=3)
    return jnp.concatenate([top, bot], axis=2).astype(BF16)


FNET_N2 = 128


FNET_PITCH_PAD = 8
FNET_UNROLL = 8


def _cos_sin(idx, period):
    ang = 2.0 * np.pi * (idx % period) / period
    return np.cos(ang), np.sin(ang)


def _hi_lo(a):
    hi = jnp.asarray(a, F32).astype(BF16)
    lo = (jnp.asarray(a, F32) - hi.astype(F32)).astype(BF16)
    return hi, lo


def _dot3_split(ah, al, b):
    bh, bl = _split(b, 2)
    d = lambda x, y: jnp.dot(x, y, preferred_element_type=F32)
    return d(ah, bh) + (d(ah, bl) + d(al, bh))


def _fnet_weight_kernel(scale, c_ref, s_ref, w_ref, o_ref):
    for g in range(w_ref.shape[0]):
        wc = _dot3(c_ref[...], w_ref[g])
        ws = _dot3(s_ref[...], w_ref[g])
        o_ref[g] = jnp.concatenate([wc, -ws], axis=1) * scale


def _fnet_kernel(n1, n2, f_ref, w_ref, m1h_ref, m1l_ref, m2h_ref, m2l_ref, o_ref,
                 gr_ref, gi_ref, zr_ref, zi_ref):
    gd = f_ref.shape[1]
    pitch = n2 + FNET_PITCH_PAD
    w = w_ref[...].astype(BF16)

    def chan_body(s1, carry):
        src = pl.ds(pl.multiple_of(s1 * n2, n2), n2)
        dst = pl.ds(pl.multiple_of(s1 * pitch, V7X_SUBLANES), n2)
        z = jnp.dot(f_ref[src, :].astype(BF16), w, preferred_element_type=F32)
        gr_ref[dst, :] = z[:, :gd]
        gi_ref[dst, :] = z[:, gd:]
        return carry

    lax.fori_loop(0, n1, chan_body, 0, unroll=min(FNET_UNROLL, n1))

    m1h, m1l = m1h_ref[...], m1l_ref[...]

    def stage1_body(s2, carry):
        idx = pl.ds(s2, n1, stride=pitch)
        x = jnp.concatenate([gr_ref[idx, :], gi_ref[idx, :]], axis=0)
        z = _dot3_split(m1h, m1l, x)
        zr_ref[idx, :] = z[:n1]
        zi_ref[idx, :] = z[n1:]
        return carry

    lax.fori_loop(0, n2, stage1_body, 0, unroll=2 * FNET_UNROLL)

    def stage2_body(s1, carry):
        src = pl.ds(pl.multiple_of(s1 * pitch, V7X_SUBLANES), n2)
        zz = jnp.concatenate([zr_ref[src, :], zi_ref[src, :]], axis=0)
        o_ref[pl.ds(s1, n2, stride=n1), :] = _dot3_split(m2h_ref[s1], m2l_ref[s1], zz)
        return carry

    lax.fori_loop(0, n1, stage2_body, 0, unroll=min(FNET_UNROLL, n1))


def _fnet(p3, col0, w_f):
    bsz, seq, _ = p3.shape
    ng, gd, _ = w_f.shape
    n2 = FNET_N2
    n1 = seq // n2
    scale = 1.0 / math.sqrt(seq * gd)
    ar = np.arange
    cd, sd = _cos_sin(ar(gd)[:, None] * ar(gd)[None, :], gd)
    c1, s1 = _cos_sin(ar(n1)[:, None] * ar(n1)[None, :], n1)
    m1 = np.block([[c1, s1], [-s1, c1]])
    c2, s2 = _cos_sin(ar(n2)[None, None, :] * (ar(n1)[:, None, None] + n1 * ar(n2)[None, :, None]), seq)
    m2 = np.concatenate([c2, s2], axis=2)
    m1h, m1l = _hi_lo(m1)
    m2h, m2l = _hi_lo(m2)

    wcat = pl.pallas_call(
        functools.partial(_fnet_weight_kernel, scale),
        out_shape=jax.ShapeDtypeStruct((ng, gd, 2 * gd), F32),
        name="fnet_weights",
    )(jnp.asarray(cd, F32), jnp.asarray(sd, F32), w_f)

    pitch_rows = n1 * (n2 + FNET_PITCH_PAD)
    const2 = lambda a: pl.BlockSpec(a.shape, lambda b, g: (0, 0))
    const3 = lambda a: pl.BlockSpec(a.shape, lambda b, g: (0, 0, 0), pipeline_mode=pl.Buffered(1))
    return pl.pallas_call(
        functools.partial(_fnet_kernel, n1, n2),
        grid=(bsz, ng),
        in_specs=[pl.BlockSpec((None, seq, gd), lambda b, g: (b, 0, col0 // gd + g)),
                  pl.BlockSpec((None, gd, 2 * gd), lambda b, g: (g, 0, 0)),
                  const2(m1h), const2(m1l), const3(m2h), const3(m2l)],
        out_specs=pl.BlockSpec((None, seq, gd), lambda b, g: (b, 0, g)),
        out_shape=jax.ShapeDtypeStruct((bsz, seq, ng * gd), F32),
        scratch_shapes=[pltpu.VMEM((pitch_rows, gd), F32)] * 4,
        compiler_params=_cparams(("parallel", "parallel")),
        name="fnet",
    )(p3, wcat, m1h, m1l, m2h, m2l)


def _odd_out_kernel(y0_ref, y1_ref, b0_ref, b1_ref, zc_ref, fn_ref, zd_ref, h_ref, lng_ref,
                    lnb_ref, bd_ref, wo_ref, fg_ref, o_ref):
    c = y0_ref.shape[1]
    bd = bd_ref[...]
    inv_n = 1.0 / HEAD_DIM
    ysum = y0_ref[...].astype(F32) + y1_ref[...].astype(F32)
    mean = _segsum(ysum, bd, 2) * inv_n
    cen = ysum - mean
    var = _segsum(cen * cen, bd, 1) * inv_n
    gn = cen * lax.rsqrt(var + GN_EPS) * lng_ref[...] + lnb_ref[...]
    yc = (gn + b0_ref[...].astype(F32) + b1_ref[...].astype(F32)) * _silu(zc_ref[...])
    yd = fn_ref[...] * _silu(zd_ref[...])
    out = h_ref[...]
    out = out + jnp.dot(yc.astype(BF16), wo_ref[0:c, :], preferred_element_type=F32)
    out = out + jnp.dot(yd.astype(BF16), wo_ref[c:, :], preferred_element_type=F32)
    ms = jnp.mean(out * out, axis=-1, keepdims=True)
    o_ref[...] = out * lax.rsqrt(ms + RMS_EPS) * fg_ref[...]


def _odd_out(y0, y1, b0, b1, p2d, zc_blk, fn, zd_blk, h2d, lnx_g, lnx_b, w_out_bf16, final_g, tm):
    t, d = h2d.shape
    c = y0.shape[1]
    cf = fn.shape[1]
    hd = np.arange(2 * HEAD_DIM) // HEAD_DIM
    bd = jnp.asarray(hd[:, None] == hd[None, :], BF16)
    rowblk = lambda w: pl.BlockSpec((tm, w), lambda i: (i, 0))
    full = lambda shape: pl.BlockSpec(shape, lambda i: (0, 0))
    return pl.pallas_call(
        _odd_out_kernel,
        grid=(t // tm,),
        in_specs=[rowblk(c), rowblk(c), rowblk(c), rowblk(c),
                  pl.BlockSpec((tm, c), lambda i: (i, zc_blk)),
                  rowblk(cf),
                  pl.BlockSpec((tm, cf), lambda i: (i, zd_blk)),
                  rowblk(d), full((1, c)), full((1, c)), full(bd.shape),
                  full(w_out_bf16.shape), full((1, d))],
        out_specs=rowblk(d),
        out_shape=jax.ShapeDtypeStruct((t, d), F32),
        compiler_params=_cparams(("parallel",)),
        name="odd_out",
    )(y0, y1, b0, b1, p2d, fn, p2d, h2d, lnx_g.reshape(1, c), lnx_b.reshape(1, c), bd,
      w_out_bf16, final_g.reshape(1, d))


WKV_LANES = 128
WKV_ROWS = 512


def _odd_layer(h2d, bsz, seq, norm_g, w_in, mu, w0, w2, a0, a2, k_k, k_a, r_k, lnx_g, lnx_b,
               fnet_w, w_out, final_g):
    t, d = h2d.shape
    c = w0.shape[-1]
    rs = 3 * c + 2 * LORA
    cf = fnet_w.shape[0] * fnet_w.shape[1]
    w_perm = jnp.concatenate([w_in[:, rs:], w_in[:, :rs]], axis=1).astype(BF16)
    p2 = _norm_proj(h2d, norm_g, w_perm, tm=min(512, t), tn=w_perm.shape[1], out_dtype=F32)
    col0 = c + 2 * cf
    p3 = p2.reshape(bsz, seq, -1)
    lora = _lora_weights(w2, a2, WKV_LANES)
    rws = min(WKV_ROWS, seq)
    y0, b0, y1, b1 = _wkv(p3, col0, mu, w0, a0, lora, k_k, k_a, r_k.reshape(-1), WKV_LANES, rws)
    fn = _fnet(p3, c, fnet_w).reshape(t, cf)
    flat = lambda a: a.reshape(t, c)
    return _odd_out(flat(y0), flat(y1), flat(b0), flat(b1), p2, 0, fn, c // cf + 1, h2d,
                    lnx_g, lnx_b, w_out.astype(BF16), final_g, tm=min(256, t))


def kernel(x, e_norm_g, e_w_in, e_conv_w, e_sgu_ln_g, e_sgu_ln_b, e_sgu_w, e_sgu_b, e_w_out,
           o_norm_g, o_w_in, o_mu, o_w0, o_w2, o_a0, o_a2, o_k_k, o_k_a, o_r_k, o_lnx_g, o_lnx_b,
           o_fnet_w, o_w_out, final_norm_g):
    bsz, seq, d = x.shape
    assert e_norm_g.shape[0] == 1 and o_norm_g.shape[0] == 1, "two-layer trunk: one even, one odd layer"
    h = x.reshape(bsz * seq, d)
    h = _even_layer(h, seq, e_norm_g[0], e_w_in[0], e_conv_w[0], e_sgu_ln_g[0], e_sgu_ln_b[0],
                    e_sgu_w[0], e_sgu_b[0], e_w_out[0])
    out = _odd_layer(h, bsz, seq, o_norm_g[0], o_w_in[0], o_mu[0], o_w0[0], o_w2[0], o_a0[0],
                     o_a2[0], o_k_k[0], o_k_a[0], o_r_k[0], o_lnx_g[0], o_lnx_b[0], o_fnet_w[0],
                     o_w_out[0], final_norm_g)
    return out.reshape(bsz, seq, d)
```

```python
import functools
import math

import numpy as np
import jax
import jax.numpy as jnp
from jax import lax
from jax.experimental import pallas as pl
from jax.experimental.pallas import tpu as pltpu

F32 = jnp.float32
BF16 = jnp.bfloat16

RMS_EPS = 1e-6
SGU_LN_EPS = 1e-5
GN_EPS = 64e-5
SGU_CHUNK = 128
SGU_GROUPS = 8
HEAD_DIM = 64
LORA = 64
FNET_GROUPS = 4
FNET_GROUP_DIM = 128

V7X_LANES = 128
V7X_SUBLANES = 8
V7X_VMEM_BYTES = 64 * 1024 * 1024
VMEM_LIMIT = V7X_VMEM_BYTES - 8 * 1024 * 1024


def _cparams(sem):
    return pltpu.CompilerParams(dimension_semantics=sem, vmem_limit_bytes=VMEM_LIMIT)


def _silu(z):
    return z * (1.0 / (1.0 + jnp.exp(-z)))


def _sigmoid(z):
    return 1.0 / (1.0 + jnp.exp(-z))


def _dot(a, b):
    return jnp.dot(a.astype(BF16), b.astype(BF16), preferred_element_type=F32)


def _split(a, n):
    parts = []
    rem = a
    for _ in range(n):
        p = rem.astype(BF16)
        parts.append(p)
        rem = rem - p.astype(F32)
    return parts


def _dot_exact_rhs(a, b_bf16, n=3):
    acc = None
    for p in _split(a, n):
        t = jnp.dot(p, b_bf16, preferred_element_type=F32)
        acc = t if acc is None else acc + t
    return acc


def _proj_kernel(x_ref, g_ref, w_ref, o_ref, hn_ref):
    @pl.when(pl.program_id(1) == 0)
    def _():
        x = x_ref[...]
        ms = jnp.mean(x * x, axis=-1, keepdims=True)
        hn_ref[...] = (x * lax.rsqrt(ms + RMS_EPS) * g_ref[...]).astype(BF16)

    o_ref[...] = jnp.dot(hn_ref[...], w_ref[...], preferred_element_type=F32).astype(o_ref.dtype)


def _norm_proj(x2d, g, w_bf16, tm, tn, out_dtype):
    t, d = x2d.shape
    n = w_bf16.shape[1]
    assert t % tm == 0 and n % tn == 0
    w_mode = dict(pipeline_mode=pl.Buffered(1)) if n == tn else {}
    return pl.pallas_call(
        _proj_kernel,
        grid=(t // tm, n // tn),
        in_specs=[
            pl.BlockSpec((tm, d), lambda i, j: (i, 0)),
            pl.BlockSpec((1, d), lambda i, j: (0, 0)),
            pl.BlockSpec((d, tn), lambda i, j: (0, j), **w_mode),
        ],
        out_specs=pl.BlockSpec((tm, tn), lambda i, j: (i, j)),
        out_shape=jax.ShapeDtypeStruct((t, n), out_dtype),
        scratch_shapes=[pltpu.VMEM((tm, d), BF16)],
        compiler_params=_cparams(("parallel", "arbitrary")),
        name="norm_proj",
    )(x2d, g.reshape(1, d), w_bf16)


def _even_kernel(seq, h_ref, hp_ref, hn_ref, ng_ref, wi_ref, cw_ref, lng_ref, lnb_ref, sw_ref,
                 sb_ref, wo_ref, o_ref):
    tm, d = h_ref.shape
    hr = hp_ref.shape[0]
    row0 = pl.program_id(0) * tm
    at_seq_start = (row0 % seq) == 0
    at_seq_end = ((row0 + tm) % seq) == 0

    h = h_ref[...]
    hx = jnp.concatenate([hp_ref[...], h, hn_ref[...]], axis=0)
    ms = jnp.mean(hx * hx, axis=-1, keepdims=True)
    hx = (hx * lax.rsqrt(ms + RMS_EPS) * ng_ref[...]).astype(BF16)
    hn = hx[hr:hr + tm]
    proj = lambda lhs, j: jnp.dot(lhs, wi_ref[:, j * d:(j + 1) * d], preferred_element_type=F32)

    xce = proj(hx, 0) * proj(hx, 2)
    xc = xce[hr:hr + tm]
    halo_prev = xce[hr - 1:hr]
    halo_next = xce[hr + tm:hr + tm + 1]
    ba, za = proj(hn, 1), proj(hn, 3)
    halo_prev = jnp.where(at_seq_start, 0.0, halo_prev)
    halo_next = jnp.where(at_seq_end, 0.0, halo_next)
    rows = lax.broadcasted_iota(jnp.int32, (tm, 1), 0)
    prev = jnp.where(rows == 0, halo_prev, pltpu.roll(xc, 1, 0))
    nxt = jnp.where(rows == tm - 1, halo_next, pltpu.roll(xc, tm - 1, 0))
    conv = cw_ref[0:1, :] * prev + cw_ref[1:2, :] * xc + cw_ref[2:3, :] * nxt
    ya = ba * conv * _silu(za)

    vb = proj(hn, 5)
    mu = jnp.mean(vb, axis=-1, keepdims=True)
    cen = vb - mu
    var = jnp.mean(cen * cen, axis=-1, keepdims=True)
    vn = (cen * lax.rsqrt(var + SGU_LN_EPS) * lng_ref[...] + lnb_ref[...]).astype(BF16)
    gw = d // SGU_GROUPS
    chunk_rows = []
    for n in range(tm // SGU_CHUNK):
        r0 = n * SGU_CHUNK
        cols = []
        for g in range(SGU_GROUPS):
            cols.append(jnp.dot(sw_ref[g], vn[r0:r0 + SGU_CHUNK, g * gw:(g + 1) * gw],
                                preferred_element_type=F32))
        chunk_rows.append(jnp.concatenate(cols, axis=1) + sb_ref[...])
    mixed = jnp.concatenate(chunk_rows, axis=0)
    yb = proj(hn, 4) * mixed * _silu(proj(hn, 6))

    out = h
    out = out + jnp.dot(ya.astype(BF16), wo_ref[0:d, :], preferred_element_type=F32)
    out = out + jnp.dot(yb.astype(BF16), wo_ref[d:2 * d, :], preferred_element_type=F32)
    o_ref[...] = out


EVEN_ROWS = 512


def _even_layer(h2d, seq, norm_g, w_in, conv_w, ln_g, ln_b, sgu_w, sgu_b, w_out):
    t, d = h2d.shape
    tm = min(EVEN_ROWS, t)
    hr = V7X_SUBLANES
    nblk = t // hr
    rpt = tm // hr
    bias_full = jnp.repeat(sgu_b.T, d // SGU_GROUPS, axis=1)
    sgu_w_bf16 = sgu_w.astype(BF16)
    w_in_bf16 = w_in.astype(BF16)
    w_out_bf16 = w_out.astype(BF16)
    const = lambda a: pl.BlockSpec(a.shape, lambda i: (0,) * a.ndim, pipeline_mode=pl.Buffered(1))
    row = lambda a: a.reshape(1, d)
    consts = (row(norm_g), w_in_bf16, conv_w, row(ln_g), row(ln_b), sgu_w_bf16, bias_full, w_out_bf16)
    return pl.pallas_call(
        functools.partial(_even_kernel, seq),
        grid=(t // tm,),
        in_specs=[
            pl.BlockSpec((tm, d), lambda i: (i, 0)),
            pl.BlockSpec((hr, d), lambda i: (jnp.maximum(i * rpt - 1, 0), 0)),
            pl.BlockSpec((hr, d), lambda i: (jnp.minimum((i + 1) * rpt, nblk - 1), 0)),
        ] + [const(a) for a in consts],
        out_specs=pl.BlockSpec((tm, d), lambda i: (i, 0)),
        out_shape=jax.ShapeDtypeStruct((t, d), F32),
        compiler_params=_cparams(("parallel",)),
        name="even_layer",
    )(h2d, h2d, h2d, *consts)


def _dot_exact_lhs(a_bf16, b, n=3):
    acc = None
    for p in _split(b, n):
        t = jnp.dot(a_bf16, p, preferred_element_type=F32)
        acc = t if acc is None else acc + t
    return acc


def _dot3(a, b):
    ah, al = _split(a, 2)
    bh, bl = _split(b, 2)
    d = lambda x, y: jnp.dot(x, y, preferred_element_type=F32)
    return d(ah, bh) + (d(ah, bl) + d(al, bh))


_NN = (((2,), (1,)), ((0,), (0,)))
_NT = (((2,), (2,)), ((0,), (0,)))


def _bmm(a, b, dims, passes):
    d = lambda x, y: lax.dot_general(x, y, dims, preferred_element_type=F32)
    if passes == 1:
        return d(a.astype(BF16), b.astype(BF16))
    ah, al = _split(a, 2)
    bh, bl = _split(b, 2)
    return d(ah, bh) + (d(ah, bl) + d(al, bh))


def _segsum(x, bd_bf16, n=3):
    w = bd_bf16.shape[0]
    cols = [_dot_exact_rhs(x[:, j:j + w], bd_bf16, n) for j in range(0, x.shape[1], w)]
    return cols[0] if len(cols) == 1 else jnp.concatenate(cols, axis=1)


WKV_CHUNK = 64


WKV_DIR_INPUTS = 12
WKV_DIR_SCRATCH = 9


def _wkv_kernel(*refs):
    ni, ns = WKV_DIR_INPUTS, WKV_DIR_SCRATCH
    shared = refs[2 * ni:2 * ni + 4]
    outs = refs[2 * ni + 4:2 * ni + 8]
    scratch = refs[2 * ni + 8:]

    step = pl.program_id(2)
    last = pl.num_programs(2) - 1

    @pl.when(step == 0)
    def _():
        for ref in scratch:
            ref[...] = jnp.zeros_like(ref)

    def body(with_local):
        per_dir = [_wkv_direction(d, refs[d * ni:(d + 1) * ni], shared, outs[2 * d:2 * d + 2],
                                  scratch[d * ns:(d + 1) * ns], with_local) for d in range(2)]
        chains = [c for c, _ in per_dir]
        groups = [g for pair in zip(*[gs for _, gs in per_dir]) for g in pair]
        start = {id(g): i * WKV_SKEW for i, g in enumerate(groups)}
        slot = 0
        while groups:
            for gen in list(groups):
                if slot >= start[id(gen)] and next(gen, "done") == "done":
                    groups.remove(gen)
            for chain in chains:
                next(chain, None)
            slot += 1
        for chain in chains:
            for _ in chain:
                pass

    pl.when(step < last)(functools.partial(body, True))
    pl.when(step == last)(functools.partial(body, False))


def _wkv_direction(direction, ins, shared, outs, scratch, with_local):
    (r_ref, k_ref, v_ref, wa_ref, mur_ref, muk_ref, muv_ref, muwa_ref, w0_ref, a0_ref, lora_ref,
     tri_ref) = ins
    kk_ref, ka_ref, rk_ref, bd_ref = shared
    y_ref, bon_ref = outs
    h_ref, cr_ref, ck_ref, cv_ref, cwa_ref, lhs_ref, n_ref, y0_ref, bonp_ref = scratch
    rws, gw = r_ref.shape
    L = WKV_CHUNK
    nch = rws // L

    ri = lax.broadcasted_iota(jnp.int32, (gw, gw), 0)
    ci = lax.broadcasted_iota(jnp.int32, (gw, gw), 1)
    same_head = (ri // HEAD_DIM) == (ci // HEAD_DIM)

    def recurrence():
        h = h_ref[...]
        ys = [None] * nch
        for c in (range(nch) if direction == 0 else range(nch - 1, -1, -1)):
            both = jnp.dot(lhs_ref[c], h.astype(BF16), preferred_element_type=F32)
            ys[c] = both[:L] + y0_ref[c]
            h = jnp.where(same_head, both[L:] + n_ref[c], 0.0)
            yield
        h_ref[...] = h
        y_ref[...] = jnp.concatenate(ys, axis=0).astype(y_ref.dtype)

    chain = recurrence()
    bon_ref[...] = bonp_ref[...].astype(bon_ref.dtype)
    if not with_local:
        return chain, []

    streams = ((r_ref, cr_ref), (k_ref, ck_ref), (v_ref, cv_ref), (wa_ref, cwa_ref))
    edge = [c_ref[...] for _, c_ref in streams]
    far = rws - 1 if direction == 0 else 0
    for x_ref, c_ref in streams:
        c_ref[...] = x_ref[far:far + 1, :]

    ncg = min(WKV_GROUP, nch)
    groups = [_wkv_local_stages(direction, g0, ncg, edge, r_ref, k_ref, v_ref, wa_ref, mur_ref,
                                muk_ref, muv_ref, muwa_ref, w0_ref, a0_ref, kk_ref, ka_ref, rk_ref,
                                lora_ref, tri_ref, bd_ref, lhs_ref, n_ref, y0_ref, bonp_ref, chain)
              for g0 in range(0, nch, ncg)]
    return chain, groups


WKV_GROUP = 8
WKV_SKEW = 0


def _wkv_local_stages(direction, c0, ncg, edge, r_ref, k_ref, v_ref, wa_ref, mur_ref, muk_ref,
                      muv_ref, muwa_ref, w0_ref, a0_ref, kk_ref, ka_ref, rk_ref, lora_ref, tri_ref,
                      bd_ref, lhs_ref, n_ref, y0_ref, bonp_ref, chain):
    L = WKV_CHUNK
    nch = ncg
    rws, gw = r_ref.shape
    nheads = gw // HEAD_DIM
    rg = ncg * L
    row0 = c0 * L
    rows = lax.broadcasted_iota(jnp.int32, (rg, 1), 0)

    def token_shift(x_ref, edge_row, mu):
        x = x_ref[row0:row0 + rg, :]
        if direction == 0:
            nb = edge_row if row0 == 0 else x_ref[row0 - 1:row0, :]
            sh = jnp.where(rows == 0, nb, pltpu.roll(x, 1, 0))
        else:
            nb = edge_row if row0 + rg == rws else x_ref[row0 + rg:row0 + rg + 1, :]
            sh = jnp.where(rows == rg - 1, nb, pltpu.roll(x, rg - 1, 0))
        return x + mu * (sh - x)

    r = token_shift(r_ref, edge[0], mur_ref[...])
    k = token_shift(k_ref, edge[1], muk_ref[...])
    v = token_shift(v_ref, edge[2], muv_ref[...])
    wa = token_shift(wa_ref, edge[3], muwa_ref[...])
    yield

    lane_wa = lax.broadcasted_iota(jnp.int32, (1, wa.shape[1]), 1)
    wa = jnp.where(lane_wa < LORA, jnp.tanh(wa), wa)
    za = _dot(wa, lora_ref[...])
    zw = w0_ref[...] + za[:, :gw]
    a = _sigmoid(a0_ref[...] + za[:, gw:])
    lw = -math.exp(-0.5) * _sigmoid(zw)

    bd = bd_ref[...]
    kk = k * kk_ref[...]
    kk = kk * lax.rsqrt(jnp.maximum(_segsum(kk * kk, bd, 1), 1e-12))
    k2 = k * (1.0 + (a - 1.0) * ka_ref[...])
    bonp_ref[row0:row0 + rg, :] = _segsum(r * k2 * rk_ref[...], bd, 1) * v
    yield

    tri2 = jnp.broadcast_to(jnp.concatenate([tri_ref[...]] * 2, axis=1), (nch, L, 2 * L))
    parts = jnp.concatenate([p.reshape(nch, L, gw) for p in _split(lw, 2)], axis=1)
    cs = lax.dot_general(tri2, parts, _NN, preferred_element_type=F32).reshape(rg, gw)
    yield
    e_inc = jnp.exp(cs)
    e_inv = jnp.exp(-cs)
    e_exc = jnp.exp(cs - lw)
    to3 = lambda x: x.reshape(nch, L, gw)
    rt = to3(r * e_inc)
    kt = to3(k2 * e_inv)
    at = to3(-kk * e_exc)
    bt = to3(kk * a * e_inv)
    v3 = to3(v)
    last = L - 1 if direction == 0 else 0
    e_end = to3(e_inc)[:, last:last + 1, :]
    kh = kt * e_end
    bh = bt * e_end

    lane = lax.broadcasted_iota(jnp.int32, (1, 1, gw), 2)
    head_masks = [(lane // HEAD_DIM) == h for h in range(nheads)]

    def blockdiag(x3):
        return jnp.concatenate([jnp.where(m, x3, 0.0) for m in head_masks], axis=1)

    def hmm(lp, *xps):
        rhs = [blockdiag(xp.astype(BF16)) for xp in xps]
        rhs = rhs[0] if len(rhs) == 1 else jnp.concatenate(rhs, axis=2)
        return _bmm(lp, rhs, _NN, 1)

    x_ar = jnp.concatenate([at, rt], axis=1)
    g = _bmm(x_ar, jnp.concatenate([blockdiag(bt.astype(BF16)), blockdiag(kt.astype(BF16))], axis=1),
             _NT, 1)
    g_b, g_k = g[:, :, :gw], g[:, :, gw:]
    yield

    t_idx = lax.broadcasted_iota(jnp.int32, (1, L, gw), 1)
    s_idx = lax.broadcasted_iota(jnp.int32, (1, L, gw), 2) % L
    if direction == 0:
        strict, incl = s_idx < t_idx, s_idx <= t_idx
    else:
        strict, incl = s_idx > t_idx, s_idx >= t_idx
    a_ab = jnp.where(strict, g_b[:, :L], 0.0)
    a_rb = jnp.where(incl, g_b[:, L:], 0.0)
    a_ak = jnp.where(strict, g_k[:, :L], 0.0)
    a_rk = jnp.where(incl, g_k[:, L:], 0.0)

    tinv = jnp.where(s_idx == t_idx, 1.0, 0.0) + a_ab
    apow = hmm(a_ab, a_ab)
    nsq = int(math.log2(L)) - 1
    for i in range(nsq):
        yield
        if i + 1 < nsq:
            both = hmm(apow, apow, tinv)
            apow, tinv = both[:, :, :gw], tinv + both[:, :, gw:]
        else:
            tinv = tinv + hmm(apow, tinv)

    akv = _bmm(jnp.concatenate([a_ak, a_rk], axis=1), blockdiag(v3.astype(BF16)), _NN, 1)
    wu = hmm(tinv, at, akv[:, :L])
    wt, u0 = wu[:, :, :gw], wu[:, :, gw:]
    yield
    ry = hmm(a_rb, wt, u0)
    rh = rt + ry[:, :, :gw]
    y0 = ry[:, :, gw:] + akv[:, L:]

    bk_t = jnp.swapaxes(jnp.concatenate([bh, kh], axis=1), 1, 2)
    rhs = jnp.concatenate([wu, jnp.concatenate([jnp.zeros_like(v3), v3], axis=2)], axis=1)
    mn = _bmm(bk_t, rhs, _NN, 1)
    ri = lax.broadcasted_iota(jnp.int32, (1, gw, gw), 1)
    ci = lax.broadcasted_iota(jnp.int32, (1, gw, gw), 2)
    m_mat = mn[:, :, :gw] + jnp.where(ri == ci, e_end, 0.0)
    yield
    for _ in chain:
        pass
    lhs_ref[c0:c0 + ncg] = jnp.concatenate([rh, m_mat], axis=1).astype(BF16)
    n_ref[c0:c0 + ncg] = mn[:, :, gw:]
    y0_ref[c0:c0 + ncg] = y0


def _wkv(p3, col0, mu, w0, a0, lora_w, k_k, k_a, r_k, gw, rws):
    bsz, seq, _ = p3.shape
    c = w0.shape[-1]
    ng = c // gw
    nstep = seq // rws
    L = WKV_CHUNK
    nch = rws // L
    wa_w = 2 * LORA
    t = np.arange(L)
    hd = np.arange(gw) // HEAD_DIM
    bd = jnp.asarray(hd[:, None] == hd[None, :], BF16)
    row = lambda x: x.reshape(1, -1)
    vec = pl.BlockSpec((1, gw), lambda b, g, s: (0, g))
    full2 = lambda shape: pl.BlockSpec(shape, lambda b, g, s: (0, 0))

    in_specs, args, out_specs, scratch = [], [], [], []
    for direction in range(2):
        blk = (lambda s: s) if direction == 0 else (lambda s: nstep - 1 - s)
        rb = lambda s, blk=blk: blk(jnp.minimum(s, nstep - 1))
        ob = lambda s, blk=blk: blk(jnp.maximum(s - 1, 0))
        stream = lambda off, rb=rb: pl.BlockSpec((None, rws, gw),
                                                 lambda b, g, s: (b, rb(s), off // gw + g))
        tri = (t[None, :] <= t[:, None]) if direction == 0 else (t[None, :] >= t[:, None])
        mu_d = mu[direction]
        in_specs += [
            stream(col0), stream(col0 + c), stream(col0 + 2 * c),
            pl.BlockSpec((None, rws, wa_w),
                         lambda b, g, s, rb=rb: (b, rb(s), (col0 + 3 * c) // wa_w)),
            vec, vec, vec, full2((1, wa_w)), vec, vec,
            pl.BlockSpec((None, wa_w, 2 * gw), lambda b, g, s: (g, 0, 0)),
            full2((L, L)),
        ]
        args += [p3, p3, p3, p3, row(mu_d[:c]), row(mu_d[c:2 * c]), row(mu_d[2 * c:3 * c]),
                 row(mu_d[3 * c:]), row(w0[direction]), row(a0[direction]), lora_w[direction],
                 jnp.asarray(tri, BF16)]
        out_specs += [pl.BlockSpec((None, rws, gw), lambda b, g, s, ob=ob: (b, ob(s), g))] * 2
        scratch += [pltpu.VMEM((gw, gw), F32), pltpu.VMEM((1, gw), F32), pltpu.VMEM((1, gw), F32),
                    pltpu.VMEM((1, gw), F32), pltpu.VMEM((1, wa_w), F32),
                    pltpu.VMEM((nch, L + gw, gw), BF16), pltpu.VMEM((nch, gw, gw), F32),
                    pltpu.VMEM((nch, L, gw), F32), pltpu.VMEM((rws, gw), F32)]
    assert len(in_specs) == 2 * WKV_DIR_INPUTS and len(scratch) == 2 * WKV_DIR_SCRATCH
    in_specs += [vec, vec, vec, full2((gw, gw))]
    args += [row(k_k), row(k_a), row(r_k), bd]
    return pl.pallas_call(
        _wkv_kernel,
        grid=(bsz, ng, nstep + 1),
        in_specs=in_specs,
        out_specs=out_specs,
        out_shape=[jax.ShapeDtypeStruct((bsz, seq, c), BF16)] * 4,
        scratch_shapes=scratch,
        compiler_params=_cparams(("parallel", "parallel", "arbitrary")),
        name="wkv",
    )(*args)


def _lora_weights(w2, a2, gw):
    nd, lo, c = w2.shape
    ng = c // gw
    w2g = w2.reshape(nd, lo, ng, gw).transpose(0, 2, 1, 3)
    a2g = a2.reshape(nd, lo, ng, gw).transpose(0, 2, 1, 3)
    z = jnp.zeros_like(w2g)
    top = jnp.concatenate([w2g, z], axis=3)
    bot = jnp.concatenate([z, a2g], axis=3)
    return jnp.concatenate([top, bot], axis=2).astype(BF16)


FNET_N2 = 128


FNET_PITCH_PAD = 8
FNET_UNROLL = 8


def _cos_sin(idx, period):
    ang = 2.0 * np.pi * (idx % period) / period
    return np.cos(ang), np.sin(ang)


def _fnet_weight_kernel(scale, c_ref, s_ref, w_ref, o_ref):
    for g in range(w_ref.shape[0]):
        wc = _dot3(c_ref[...], w_ref[g])
        ws = _dot3(s_ref[...], w_ref[g])
        o_ref[g] = jnp.concatenate([wc, -ws], axis=1) * scale


def _fnet_kernel(n1, n2, f_ref, w_ref, m1_ref, m2_ref, o_ref, gr_ref, gi_ref, zr_ref, zi_ref):
    gd = f_ref.shape[1]
    pitch = n2 + FNET_PITCH_PAD
    w = w_ref[...].astype(BF16)

    def chan_body(s1, carry):
        src = pl.ds(pl.multiple_of(s1 * n2, n2), n2)
        dst = pl.ds(pl.multiple_of(s1 * pitch, V7X_SUBLANES), n2)
        z = jnp.dot(f_ref[src, :].astype(BF16), w, preferred_element_type=F32)
        gr_ref[dst, :] = z[:, :gd]
        gi_ref[dst, :] = z[:, gd:]
        return carry

    lax.fori_loop(0, n1, chan_body, 0, unroll=min(FNET_UNROLL, n1))

    m1 = m1_ref[...].astype(BF16)

    def stage1_body(s2, carry):
        idx = pl.ds(s2, n1, stride=pitch)
        x = jnp.concatenate([gr_ref[idx, :], gi_ref[idx, :]], axis=0)
        z = jnp.dot(m1, x.astype(BF16), preferred_element_type=F32)
        zr_ref[idx, :] = z[:n1]
        zi_ref[idx, :] = z[n1:]
        return carry

    lax.fori_loop(0, n2, stage1_body, 0, unroll=2 * FNET_UNROLL)

    def stage2_body(s1, carry):
        src = pl.ds(pl.multiple_of(s1 * pitch, V7X_SUBLANES), n2)
        zz = jnp.concatenate([zr_ref[src, :], zi_ref[src, :]], axis=0)
        o_ref[pl.ds(s1, n2, stride=n1), :] = jnp.dot(m2_ref[s1].astype(BF16), zz.astype(BF16),
                                                     preferred_element_type=F32)
        return carry

    lax.fori_loop(0, n1, stage2_body, 0, unroll=min(FNET_UNROLL, n1))


def _fnet(p3, col0, w_f):
    bsz, seq, _ = p3.shape
    ng, gd, _ = w_f.shape
    n2 = FNET_N2
    n1 = seq // n2
    scale = 1.0 / math.sqrt(seq * gd)
    ar = np.arange
    cd, sd = _cos_sin(ar(gd)[:, None] * ar(gd)[None, :], gd)
    c1, s1 = _cos_sin(ar(n1)[:, None] * ar(n1)[None, :], n1)
    m1 = np.block([[c1, s1], [-s1, c1]])
    c2, s2 = _cos_sin(ar(n2)[None, None, :] * (ar(n1)[:, None, None] + n1 * ar(n2)[None, :, None]), seq)
    m1 = jnp.asarray(m1, F32)
    m2 = jnp.asarray(np.concatenate([c2, s2], axis=2), F32)

    wcat = pl.pallas_call(
        functools.partial(_fnet_weight_kernel, scale),
        out_shape=jax.ShapeDtypeStruct((ng, gd, 2 * gd), F32),
        name="fnet_weights",
    )(jnp.asarray(cd, F32), jnp.asarray(sd, F32), w_f)

    pitch_rows = n1 * (n2 + FNET_PITCH_PAD)
    const2 = lambda a: pl.BlockSpec(a.shape, lambda b, g: (0, 0))
    const3 = lambda a: pl.BlockSpec(a.shape, lambda b, g: (0, 0, 0), pipeline_mode=pl.Buffered(1))
    return pl.pallas_call(
        functools.partial(_fnet_kernel, n1, n2),
        grid=(bsz, ng),
        in_specs=[pl.BlockSpec((None, seq, gd), lambda b, g: (b, 0, col0 // gd + g)),
                  pl.BlockSpec((None, gd, 2 * gd), lambda b, g: (g, 0, 0)),
                  const2(m1), const3(m2)],
        out_specs=pl.BlockSpec((None, seq, gd), lambda b, g: (b, 0, g)),
        out_shape=jax.ShapeDtypeStruct((bsz, seq, ng * gd), F32),
        scratch_shapes=[pltpu.VMEM((pitch_rows, gd), F32)] * 4,
        compiler_params=_cparams(("parallel", "parallel")),
        name="fnet",
    )(p3, wcat, m1, m2)


def _odd_out_kernel(y0_ref, y1_ref, b0_ref, b1_ref, zc_ref, fn_ref, zd_ref, h_ref, lng_ref,
                    lnb_ref, bd_ref, wo_ref, fg_ref, o_ref):
    c = y0_ref.shape[1]
    bd = bd_ref[...]
    inv_n = 1.0 / HEAD_DIM
    ysum = y0_ref[...].astype(F32) + y1_ref[...].astype(F32)
    mean = _segsum(ysum, bd, 2) * inv_n
    cen = ysum - mean
    var = _segsum(cen * cen, bd, 1) * inv_n
    gn = cen * lax.rsqrt(var + GN_EPS) * lng_ref[...] + lnb_ref[...]
    yc = (gn + b0_ref[...].astype(F32) + b1_ref[...].astype(F32)) * _silu(zc_ref[...])
    yd = fn_ref[...] * _silu(zd_ref[...])
    out = h_ref[...]
    out = out + jnp.dot(yc.astype(BF16), wo_ref[0:c, :], preferred_element_type=F32)
    out = out + jnp.dot(yd.astype(BF16), wo_ref[c:, :], preferred_element_type=F32)
    ms = jnp.mean(out * out, axis=-1, keepdims=True)
    o_ref[...] = out * lax.rsqrt(ms + RMS_EPS) * fg_ref[...]


def _odd_out(y0, y1, b0, b1, p2d, zc_blk, fn, zd_blk, h2d, lnx_g, lnx_b, w_out_bf16, final_g, tm):
    t, d = h2d.shape
    c = y0.shape[1]
    cf = fn.shape[1]
    hd = np.arange(2 * HEAD_DIM) // HEAD_DIM
    bd = jnp.asarray(hd[:, None] == hd[None, :], BF16)
    rowblk = lambda w: pl.BlockSpec((tm, w), lambda i: (i, 0))
    full = lambda shape: pl.BlockSpec(shape, lambda i: (0, 0))
    return pl.pallas_call(
        _odd_out_kernel,
        grid=(t // tm,),
        in_specs=[rowblk(c), rowblk(c), rowblk(c), rowblk(c),
                  pl.BlockSpec((tm, c), lambda i: (i, zc_blk)),
                  rowblk(cf),
                  pl.BlockSpec((tm, cf), lambda i: (i, zd_blk)),
                  rowblk(d), full((1, c)), full((1, c)), full(bd.shape),
                  full(w_out_bf16.shape), full((1, d))],
        out_specs=rowblk(d),
        out_shape=jax.ShapeDtypeStruct((t, d), F32),
        compiler_params=_cparams(("parallel",)),
        name="odd_out",
    )(y0, y1, b0, b1, p2d, fn, p2d, h2d, lnx_g.reshape(1, c), lnx_b.reshape(1, c), bd,
      w_out_bf16, final_g.reshape(1, d))


WKV_LANES = 128
WKV_ROWS = 512


def _odd_layer(h2d, bsz, seq, norm_g, w_in, mu, w0, w2, a0, a2, k_k, k_a, r_k, lnx_g, lnx_b,
               fnet_w, w_out, final_g):
    t, d = h2d.shape
    c = w0.shape[-1]
    rs = 3 * c + 2 * LORA
    cf = fnet_w.shape[0] * fnet_w.shape[1]
    w_perm = jnp.concatenate([w_in[:, rs:], w_in[:, :rs]], axis=1).astype(BF16)
    p2 = _norm_proj(h2d, norm_g, w_perm, tm=min(512, t), tn=w_perm.shape[1], out_dtype=F32)
    col0 = c + 2 * cf
    p3 = p2.reshape(bsz, seq, -1)
    lora = _lora_weights(w2, a2, WKV_LANES)
    rws = min(WKV_ROWS, seq)
    y0, b0, y1, b1 = _wkv(p3, col0, mu, w0, a0, lora, k_k, k_a, r_k.reshape(-1), WKV_LANES, rws)
    fn = _fnet(p3, c, fnet_w).reshape(t, cf)
    flat = lambda a: a.reshape(t, c)
    return _odd_out(flat(y0), flat(y1), flat(b0), flat(b1), p2, 0, fn, c // cf + 1, h2d,
                    lnx_g, lnx_b, w_out.astype(BF16), final_g, tm=min(256, t))


def kernel(x, e_norm_g, e_w_in, e_conv_w, e_sgu_ln_g, e_sgu_ln_b, e_sgu_w, e_sgu_b, e_w_out,
           o_norm_g, o_w_in, o_mu, o_w0, o_w2, o_a0, o_a2, o_k_k, o_k_a, o_r_k, o_lnx_g, o_lnx_b,
           o_fnet_w, o_w_out, final_norm_g):
    bsz, seq, d = x.shape
    assert e_norm_g.shape[0] == 1 and o_norm_g.shape[0] == 1, "two-layer trunk: one even, one odd layer"
    h = x.reshape(bsz * seq, d)
    h = _even_layer(h, seq, e_norm_g[0], e_w_in[0], e_conv_w[0], e_sgu_ln_g[0], e_sgu_ln_b[0],
                    e_sgu_w[0], e_sgu_b[0], e_w_out[0])
    out = _odd_layer(h, bsz, seq, o_norm_g[0], o_w_in[0], o_mu[0], o_w0[0], o_w2[0], o_a0[0],
                     o_a2[0], o_k_k[0], o_k_a[0], o_r_k[0], o_lnx_g[0], o_lnx_b[0], o_fnet_w[0],
                     o_w_out[0], final_norm_g)
    return out.reshape(bsz, seq, d)
```

```python
import functools
import math

import numpy as np
import jax
import jax.numpy as jnp
from jax import lax
from jax.experimental import pallas as pl
from jax.experimental.pallas import tpu as pltpu

F32 = jnp.float32
BF16 = jnp.bfloat16

RMS_EPS = 1e-6
SGU_LN_EPS = 1e-5
GN_EPS = 64e-5
SGU_CHUNK = 128
SGU_GROUPS = 8
HEAD_DIM = 64
LORA = 64

V7X_SUBLANES = 8
V7X_VMEM_BYTES = 64 * 1024 * 1024
VMEM_LIMIT = V7X_VMEM_BYTES - 8 * 1024 * 1024


def _cparams(sem):
    return pltpu.CompilerParams(dimension_semantics=sem, vmem_limit_bytes=VMEM_LIMIT)


def _silu(z):
    return z * (1.0 / (1.0 + jnp.exp(-z)))


def _sigmoid(z):
    return 1.0 / (1.0 + jnp.exp(-z))


def _dot(a, b):
    return jnp.dot(a.astype(BF16), b.astype(BF16), preferred_element_type=F32)


def _split(a, n):
    parts = []
    rem = a
    for _ in range(n):
        p = rem.astype(BF16)
        parts.append(p)
        rem = rem - p.astype(F32)
    return parts


def _dot_exact_rhs(a, b_bf16, n=3):
    acc = None
    for p in _split(a, n):
        t = jnp.dot(p, b_bf16, preferred_element_type=F32)
        acc = t if acc is None else acc + t
    return acc


def _proj_kernel(x_ref, g_ref, w_ref, o_ref, hn_ref):
    @pl.when(pl.program_id(1) == 0)
    def _():
        x = x_ref[...]
        ms = jnp.mean(x * x, axis=-1, keepdims=True)
        hn_ref[...] = (x * lax.rsqrt(ms + RMS_EPS) * g_ref[...]).astype(BF16)

    o_ref[...] = jnp.dot(hn_ref[...], w_ref[...], preferred_element_type=F32).astype(o_ref.dtype)


def _norm_proj(x2d, g, w_bf16, tm, tn, out_dtype):
    t, d = x2d.shape
    n = w_bf16.shape[1]
    assert t % tm == 0 and n % tn == 0
    w_mode = dict(pipeline_mode=pl.Buffered(1)) if n == tn else {}
    return pl.pallas_call(
        _proj_kernel,
        grid=(t // tm, n // tn),
        in_specs=[
            pl.BlockSpec((tm, d), lambda i, j: (i, 0)),
            pl.BlockSpec((1, d), lambda i, j: (0, 0)),
            pl.BlockSpec((d, tn), lambda i, j: (0, j), **w_mode),
        ],
        out_specs=pl.BlockSpec((tm, tn), lambda i, j: (i, j)),
        out_shape=jax.ShapeDtypeStruct((t, n), out_dtype),
        scratch_shapes=[pltpu.VMEM((tm, d), BF16)],
        compiler_params=_cparams(("parallel", "arbitrary")),
        name="norm_proj",
    )(x2d, g.reshape(1, d), w_bf16)


def _even_kernel(seq, h_ref, hp_ref, hn_ref, ng_ref, wi_ref, cw_ref, lng_ref, lnb_ref, sw_ref,
                 sb_ref, wo_ref, o_ref):
    tm, d = h_ref.shape
    hr = hp_ref.shape[0]
    row0 = pl.program_id(0) * tm
    at_seq_start = (row0 % seq) == 0
    at_seq_end = ((row0 + tm) % seq) == 0

    h = h_ref[...]
    hx = jnp.concatenate([hp_ref[...], h, hn_ref[...]], axis=0)
    ms = jnp.mean(hx * hx, axis=-1, keepdims=True)
    hx = (hx * lax.rsqrt(ms + RMS_EPS) * ng_ref[...]).astype(BF16)
    hn = hx[hr:hr + tm]
    proj = lambda lhs, j: jnp.dot(lhs, wi_ref[:, j * d:(j + 1) * d], preferred_element_type=F32)

    xce = proj(hx, 0) * proj(hx, 2)
    xc = xce[hr:hr + tm]
    halo_prev = xce[hr - 1:hr]
    halo_next = xce[hr + tm:hr + tm + 1]
    ba, za = proj(hn, 1), proj(hn, 3)
    halo_prev = jnp.where(at_seq_start, 0.0, halo_prev)
    halo_next = jnp.where(at_seq_end, 0.0, halo_next)
    rows = lax.broadcasted_iota(jnp.int32, (tm, 1), 0)
    prev = jnp.where(rows == 0, halo_prev, pltpu.roll(xc, 1, 0))
    nxt = jnp.where(rows == tm - 1, halo_next, pltpu.roll(xc, tm - 1, 0))
    conv = cw_ref[0:1, :] * prev + cw_ref[1:2, :] * xc + cw_ref[2:3, :] * nxt
    ya = ba * conv * _silu(za)

    vb = proj(hn, 5)
    mu = jnp.mean(vb, axis=-1, keepdims=True)
    cen = vb - mu
    var = jnp.mean(cen * cen, axis=-1, keepdims=True)
    vn = (cen * lax.rsqrt(var + SGU_LN_EPS) * lng_ref[...] + lnb_ref[...]).astype(BF16)
    gw = d // SGU_GROUPS
    chunk_rows = []
    for n in range(tm // SGU_CHUNK):
        r0 = n * SGU_CHUNK
        cols = []
        for g in range(SGU_GROUPS):
            cols.append(jnp.dot(sw_ref[g], vn[r0:r0 + SGU_CHUNK, g * gw:(g + 1) * gw],
                                preferred_element_type=F32))
        chunk_rows.append(jnp.concatenate(cols, axis=1) + sb_ref[...])
    mixed = jnp.concatenate(chunk_rows, axis=0)
    yb = proj(hn, 4) * mixed * _silu(proj(hn, 6))

    out = h
    out = out + jnp.dot(ya.astype(BF16), wo_ref[0:d, :], preferred_element_type=F32)
    out = out + jnp.dot(yb.astype(BF16), wo_ref[d:2 * d, :], preferred_element_type=F32)
    o_ref[...] = out


EVEN_ROWS = 512


def _even_layer(h2d, seq, norm_g, w_in, conv_w, ln_g, ln_b, sgu_w, sgu_b, w_out):
    t, d = h2d.shape
    tm = min(EVEN_ROWS, t)
    hr = V7X_SUBLANES
    nblk = t // hr
    rpt = tm // hr
    bias_full = jnp.repeat(sgu_b.T, d // SGU_GROUPS, axis=1)
    sgu_w_bf16 = sgu_w.astype(BF16)
    w_in_bf16 = w_in.astype(BF16)
    w_out_bf16 = w_out.astype(BF16)
    const = lambda a: pl.BlockSpec(a.shape, lambda i: (0,) * a.ndim, pipeline_mode=pl.Buffered(1))
    row = lambda a: a.reshape(1, d)
    consts = (row(norm_g), w_in_bf16, conv_w, row(ln_g), row(ln_b), sgu_w_bf16, bias_full, w_out_bf16)
    return pl.pallas_call(
        functools.partial(_even_kernel, seq),
        grid=(t // tm,),
        in_specs=[
            pl.BlockSpec((tm, d), lambda i: (i, 0)),
            pl.BlockSpec((hr, d), lambda i: (jnp.maximum(i * rpt - 1, 0), 0)),
            pl.BlockSpec((hr, d), lambda i: (jnp.minimum((i + 1) * rpt, nblk - 1), 0)),
        ] + [const(a) for a in consts],
        out_specs=pl.BlockSpec((tm, d), lambda i: (i, 0)),
        out_shape=jax.ShapeDtypeStruct((t, d), F32),
        compiler_params=_cparams(("parallel",)),
        name="even_layer",
    )(h2d, h2d, h2d, *consts)


def _dot3(a, b):
    ah, al = _split(a, 2)
    bh, bl = _split(b, 2)
    d = lambda x, y: jnp.dot(x, y, preferred_element_type=F32)
    return d(ah, bh) + (d(ah, bl) + d(al, bh))


_NN = (((2,), (1,)), ((0,), (0,)))
_NT = (((2,), (2,)), ((0,), (0,)))


def _bmm(a, b, dims):
    return lax.dot_general(a.astype(BF16), b.astype(BF16), dims, preferred_element_type=F32)


def _segsum(x, bd_bf16, n=3):
    w = bd_bf16.shape[0]
    cols = [_dot_exact_rhs(x[:, j:j + w], bd_bf16, n) for j in range(0, x.shape[1], w)]
    return cols[0] if len(cols) == 1 else jnp.concatenate(cols, axis=1)


WKV_CHUNK = 64


WKV_DIR_INPUTS = 12
WKV_DIR_SCRATCH = 9


def _wkv_kernel(*refs):
    ni, ns = WKV_DIR_INPUTS, WKV_DIR_SCRATCH
    shared = refs[2 * ni:2 * ni + 4]
    outs = refs[2 * ni + 4:2 * ni + 8]
    scratch = refs[2 * ni + 8:]

    step = pl.program_id(2)
    last = pl.num_programs(2) - 1

    @pl.when(step == 0)
    def _():
        for ref in scratch:
            ref[...] = jnp.zeros_like(ref)

    def body(with_local):
        per_dir = [_wkv_direction(d, refs[d * ni:(d + 1) * ni], shared, outs[2 * d:2 * d + 2],
                                  scratch[d * ns:(d + 1) * ns], with_local) for d in range(2)]
        chains = [c for c, _ in per_dir]
        groups = [g for pair in zip(*[gs for _, gs in per_dir]) for g in pair]
        start = {id(g): i * WKV_SKEW for i, g in enumerate(groups)}
        slot = 0
        while groups:
            for gen in list(groups):
                if slot >= start[id(gen)] and next(gen, "done") == "done":
                    groups.remove(gen)
            for chain in chains:
                next(chain, None)
            slot += 1
        for chain in chains:
            for _ in chain:
                pass

    pl.when(step < last)(functools.partial(body, True))
    pl.when(step == last)(functools.partial(body, False))


def _wkv_direction(direction, ins, shared, outs, scratch, with_local):
    (r_ref, k_ref, v_ref, wa_ref, mur_ref, muk_ref, muv_ref, muwa_ref, w0_ref, a0_ref, lora_ref,
     tri_ref) = ins
    kk_ref, ka_ref, rk_ref, bd_ref = shared
    y_ref, bon_ref = outs
    h_ref, cr_ref, ck_ref, cv_ref, cwa_ref, lhs_ref, n_ref, y0_ref, bonp_ref = scratch
    rws, gw = r_ref.shape
    L = WKV_CHUNK
    nch = rws // L

    lane_head = lax.broadcasted_iota(jnp.int32, (1, gw), 1) // HEAD_DIM

    def recurrence():
        h = h_ref[...]
        ys = [None] * nch
        for c in (range(nch) if direction == 0 else range(nch - 1, -1, -1)):
            hb = h.astype(BF16)
            hbd = jnp.concatenate([jnp.where(lane_head == i, hb, 0.0)
                                   for i in range(gw // HEAD_DIM)], axis=0)
            both = jnp.dot(lhs_ref[c], hbd, preferred_element_type=F32)
            ys[c] = both[:L] + y0_ref[c]
            h = both[L:] + n_ref[c]
            yield
        h_ref[...] = h
        y_ref[...] = jnp.concatenate(ys, axis=0).astype(y_ref.dtype)

    chain = recurrence()
    bon_ref[...] = bonp_ref[...].astype(bon_ref.dtype)
    if not with_local:
        return chain, []

    streams = ((r_ref, cr_ref), (k_ref, ck_ref), (v_ref, cv_ref), (wa_ref, cwa_ref))
    edge = [c_ref[...] for _, c_ref in streams]
    far = rws - 1 if direction == 0 else 0
    for x_ref, c_ref in streams:
        c_ref[...] = x_ref[far:far + 1, :]

    ncg = min(WKV_GROUP, nch)
    groups = [_wkv_local_stages(direction, g0, ncg, edge, r_ref, k_ref, v_ref, wa_ref, mur_ref,
                                muk_ref, muv_ref, muwa_ref, w0_ref, a0_ref, kk_ref, ka_ref, rk_ref,
                                lora_ref, tri_ref, bd_ref, lhs_ref, n_ref, y0_ref, bonp_ref, chain)
              for g0 in range(0, nch, ncg)]
    return chain, groups


WKV_GROUP = 8
WKV_SKEW = 0


def _wkv_local_stages(direction, c0, ncg, edge, r_ref, k_ref, v_ref, wa_ref, mur_ref, muk_ref,
                      muv_ref, muwa_ref, w0_ref, a0_ref, kk_ref, ka_ref, rk_ref, lora_ref, tri_ref,
                      bd_ref, lhs_ref, n_ref, y0_ref, bonp_ref, chain):
    L = WKV_CHUNK
    nch = ncg
    rws, gw = r_ref.shape
    nheads = gw // HEAD_DIM
    rg = ncg * L
    row0 = c0 * L
    rows = lax.broadcasted_iota(jnp.int32, (rg, 1), 0)

    def token_shift(x_ref, edge_row, mu):
        x = x_ref[row0:row0 + rg, :]
        if direction == 0:
            nb = edge_row if row0 == 0 else x_ref[row0 - 1:row0, :]
            sh = jnp.where(rows == 0, nb, pltpu.roll(x, 1, 0))
        else:
            nb = edge_row if row0 + rg == rws else x_ref[row0 + rg:row0 + rg + 1, :]
            sh = jnp.where(rows == rg - 1, nb, pltpu.roll(x, rg - 1, 0))
        return x + mu * (sh - x)

    r = token_shift(r_ref, edge[0], mur_ref[...])
    k = token_shift(k_ref, edge[1], muk_ref[...])
    v = token_shift(v_ref, edge[2], muv_ref[...])
    wa = token_shift(wa_ref, edge[3], muwa_ref[...])
    yield

    lane_wa = lax.broadcasted_iota(jnp.int32, (1, wa.shape[1]), 1)
    wa = jnp.where(lane_wa < LORA, jnp.tanh(wa), wa)
    za = _dot(wa, lora_ref[...])
    zw = w0_ref[...] + za[:, :gw]
    a = _sigmoid(a0_ref[...] + za[:, gw:])
    lw = -math.exp(-0.5) * _sigmoid(zw)

    bd = bd_ref[...]
    kk = k * kk_ref[...]
    kk = kk * lax.rsqrt(jnp.maximum(_segsum(kk * kk, bd, 1), 1e-12))
    k2 = k * (1.0 + (a - 1.0) * ka_ref[...])
    bonp_ref[row0:row0 + rg, :] = _segsum(r * k2 * rk_ref[...], bd, 1) * v
    yield

    tri2 = jnp.broadcast_to(jnp.concatenate([tri_ref[...]] * 2, axis=1), (nch, L, 2 * L))
    parts = jnp.concatenate([p.reshape(nch, L, gw) for p in _split(lw, 2)], axis=1)
    cs = lax.dot_general(tri2, parts, _NN, preferred_element_type=F32).reshape(rg, gw)
    yield
    e_inc = jnp.exp(cs)
    e_inv = jnp.exp(-cs)
    e_exc = jnp.exp(cs - lw)
    to3 = lambda x: x.reshape(nch, L, gw)
    rt = to3(r * e_inc)
    kt = to3(k2 * e_inv)
    at = to3(-kk * e_exc)
    bt = to3(kk * a * e_inv)
    v3 = to3(v)
    last = L - 1 if direction == 0 else 0
    e_end = to3(e_inc)[:, last:last + 1, :]
    kh = kt * e_end
    bh = bt * e_end

    lane = lax.broadcasted_iota(jnp.int32, (1, 1, gw), 2)
    head_masks = [(lane // HEAD_DIM) == h for h in range(nheads)]

    def blockdiag(x3):
        return jnp.concatenate([jnp.where(m, x3, 0.0) for m in head_masks], axis=1)

    def hmm(lp, *xps):
        rhs = [blockdiag(xp.astype(BF16)) for xp in xps]
        rhs = rhs[0] if len(rhs) == 1 else jnp.concatenate(rhs, axis=2)
        return _bmm(lp, rhs, _NN)

    x_ar = jnp.concatenate([at, rt], axis=1)
    g = _bmm(x_ar, jnp.concatenate([blockdiag(bt.astype(BF16)), blockdiag(kt.astype(BF16))], axis=1),
             _NT)
    g_b, g_k = g[:, :, :gw], g[:, :, gw:]
    yield

    t_idx = lax.broadcasted_iota(jnp.int32, (1, L, gw), 1)
    s_idx = lax.broadcasted_iota(jnp.int32, (1, L, gw), 2) % L
    if direction == 0:
        strict, incl = s_idx < t_idx, s_idx <= t_idx
    else:
        strict, incl = s_idx > t_idx, s_idx >= t_idx
    a_ab = jnp.where(strict, g_b[:, :L], 0.0)
    a_rb = jnp.where(incl, g_b[:, L:], 0.0)
    a_ak = jnp.where(strict, g_k[:, :L], 0.0)
    a_rk = jnp.where(incl, g_k[:, L:], 0.0)

    tinv = jnp.where(s_idx == t_idx, 1.0, 0.0) + a_ab
    apow = hmm(a_ab, a_ab)
    nsq = int(math.log2(L)) - 1
    for i in range(nsq):
        yield
        if i + 1 < nsq:
            both = hmm(apow, apow, tinv)
            apow, tinv = both[:, :, :gw], tinv + both[:, :, gw:]
        else:
            tinv = tinv + hmm(apow, tinv)

    akv = _bmm(jnp.concatenate([a_ak, a_rk], axis=1), blockdiag(v3.astype(BF16)), _NN)
    wu = hmm(tinv, at, akv[:, :L])
    wt, u0 = wu[:, :, :gw], wu[:, :, gw:]
    yield
    ry = hmm(a_rb, wt, u0)
    rh = rt + ry[:, :, :gw]
    y0 = ry[:, :, gw:] + akv[:, L:]

    bk_t = jnp.swapaxes(jnp.concatenate([bh, kh], axis=1), 1, 2)
    bk_pack = jnp.concatenate([bk_t[:, i * HEAD_DIM:(i + 1) * HEAD_DIM] for i in range(nheads)],
                              axis=2).astype(BF16)
    rhs = jnp.concatenate([wu, jnp.concatenate([jnp.zeros_like(v3), v3], axis=2)],
                          axis=1).astype(BF16)
    lane2 = lax.broadcasted_iota(jnp.int32, (1, 1, 2 * gw), 2) % gw // HEAD_DIM
    rhs_heads = jnp.concatenate([jnp.where(lane2 == i, rhs, 0.0) for i in range(nheads)], axis=1)
    mn = _bmm(bk_pack, rhs_heads, _NN)
    ri = lax.broadcasted_iota(jnp.int32, (1, HEAD_DIM, gw), 1)
    ci = lax.broadcasted_iota(jnp.int32, (1, HEAD_DIM, gw), 2) % HEAD_DIM
    m_mat = mn[:, :, :gw] + jnp.where(ri == ci, e_end, 0.0)
    yield
    for _ in chain:
        pass
    lhs_ref[c0:c0 + ncg] = jnp.concatenate([rh, m_mat], axis=1).astype(BF16)
    n_ref[c0:c0 + ncg] = mn[:, :, gw:]
    y0_ref[c0:c0 + ncg] = y0


def _wkv(p3, col0, mu, w0, a0, lora_w, k_k, k_a, r_k, gw, rws):
    bsz, seq, _ = p3.shape
    c = w0.shape[-1]
    ng = c // gw
    nstep = seq // rws
    L = WKV_CHUNK
    nch = rws // L
    wa_w = 2 * LORA
    t = np.arange(L)
    hd = np.arange(gw) // HEAD_DIM
    bd = jnp.asarray(hd[:, None] == hd[None, :], BF16)
    row = lambda x: x.reshape(1, -1)
    vec = pl.BlockSpec((1, gw), lambda b, g, s: (0, g))
    full2 = lambda shape: pl.BlockSpec(shape, lambda b, g, s: (0, 0))

    in_specs, args, out_specs, scratch = [], [], [], []
    for direction in range(2):
        blk = (lambda s: s) if direction == 0 else (lambda s: nstep - 1 - s)
        rb = lambda s, blk=blk: blk(jnp.minimum(s, nstep - 1))
        ob = lambda s, blk=blk: blk(jnp.maximum(s - 1, 0))
        stream = lambda off, rb=rb: pl.BlockSpec((None, rws, gw),
                                                 lambda b, g, s: (b, rb(s), off // gw + g))
        tri = (t[None, :] <= t[:, None]) if direction == 0 else (t[None, :] >= t[:, None])
        mu_d = mu[direction]
        in_specs += [
            stream(col0), stream(col0 + c), stream(col0 + 2 * c),
            pl.BlockSpec((None, rws, wa_w),
                         lambda b, g, s, rb=rb: (b, rb(s), (col0 + 3 * c) // wa_w)),
            vec, vec, vec, full2((1, wa_w)), vec, vec,
            pl.BlockSpec((None, wa_w, 2 * gw), lambda b, g, s: (g, 0, 0)),
            full2((L, L)),
        ]
        args += [p3, p3, p3, p3, row(mu_d[:c]), row(mu_d[c:2 * c]), row(mu_d[2 * c:3 * c]),
                 row(mu_d[3 * c:]), row(w0[direction]), row(a0[direction]), lora_w[direction],
                 jnp.asarray(tri, BF16)]
        out_specs += [pl.BlockSpec((None, rws, gw), lambda b, g, s, ob=ob: (b, ob(s), g))] * 2
        scratch += [pltpu.VMEM((HEAD_DIM, gw), F32), pltpu.VMEM((1, gw), F32), pltpu.VMEM((1, gw), F32),
                    pltpu.VMEM((1, gw), F32), pltpu.VMEM((1, wa_w), F32),
                    pltpu.VMEM((nch, L + HEAD_DIM, gw), BF16), pltpu.VMEM((nch, HEAD_DIM, gw), F32),
                    pltpu.VMEM((nch, L, gw), F32), pltpu.VMEM((rws, gw), F32)]
    assert len(in_specs) == 2 * WKV_DIR_INPUTS and len(scratch) == 2 * WKV_DIR_SCRATCH
    in_specs += [vec, vec, vec, full2((gw, gw))]
    args += [row(k_k), row(k_a), row(r_k), bd]
    return pl.pallas_call(
        _wkv_kernel,
        grid=(bsz, ng, nstep + 1),
        in_specs=in_specs,
        out_specs=out_specs,
        out_shape=[jax.ShapeDtypeStruct((bsz, seq, c), BF16)] * 4,
        scratch_shapes=scratch,
        compiler_params=_cparams(("parallel", "parallel", "arbitrary")),
        name="wkv",
    )(*args)


def _lora_weights(w2, a2, gw):
    nd, lo, c = w2.shape
    ng = c // gw
    w2g = w2.reshape(nd, lo, ng, gw).transpose(0, 2, 1, 3)
    a2g = a2.reshape(nd, lo, ng, gw).transpose(0, 2, 1, 3)
    z = jnp.zeros_like(w2g)
    top = jnp.concatenate([w2g, z], axis=3)
    bot = jnp.concatenate([z, a2g], axis=3)
    return jnp.concatenate([top, bot], axis=2).astype(BF16)


FNET_N2 = 128


FNET_PITCH_PAD = 8
FNET_UNROLL = 8


def _cos_sin(idx, period):
    ang = 2.0 * np.pi * (idx % period) / period
    return np.cos(ang), np.sin(ang)


def _fnet_weight_kernel(scale, c_ref, s_ref, w_ref, o_ref):
    for g in range(w_ref.shape[0]):
        wc = _dot3(c_ref[...], w_ref[g])
        ws = _dot3(s_ref[...], w_ref[g])
        o_ref[g] = jnp.concatenate([wc, -ws], axis=1) * scale


def _fnet_kernel(n1, n2, f_ref, w_ref, m1_ref, m2_ref, o_ref, gr_ref, gi_ref, zr_ref, zi_ref):
    gd = f_ref.shape[1]
    pitch = n2 + FNET_PITCH_PAD
    w = w_ref[...].astype(BF16)

    def chan_body(s1, carry):
        src = pl.ds(pl.multiple_of(s1 * n2, n2), n2)
        dst = pl.ds(pl.multiple_of(s1 * pitch, V7X_SUBLANES), n2)
        z = jnp.dot(f_ref[src, :].astype(BF16), w, preferred_element_type=F32)
        gr_ref[dst, :] = z[:, :gd]
        gi_ref[dst, :] = z[:, gd:]
        return carry

    lax.fori_loop(0, n1, chan_body, 0, unroll=min(FNET_UNROLL, n1))

    m1 = m1_ref[...].astype(BF16)

    def stage1_body(s2, carry):
        idx = pl.ds(s2, n1, stride=pitch)
        x = jnp.concatenate([gr_ref[idx, :], gi_ref[idx, :]], axis=0)
        z = jnp.dot(m1, x.astype(BF16), preferred_element_type=F32)
        zr_ref[idx, :] = z[:n1]
        zi_ref[idx, :] = z[n1:]
        return carry

    lax.fori_loop(0, n2, stage1_body, 0, unroll=2 * FNET_UNROLL)

    def stage2_body(s1, carry):
        src = pl.ds(pl.multiple_of(s1 * pitch, V7X_SUBLANES), n2)
        zz = jnp.concatenate([zr_ref[src, :], zi_ref[src, :]], axis=0)
        o_ref[pl.ds(s1, n2, stride=n1), :] = jnp.dot(m2_ref[s1].astype(BF16), zz.astype(BF16),
                                                     preferred_element_type=F32)
        return carry

    lax.fori_loop(0, n1, stage2_body, 0, unroll=min(FNET_UNROLL, n1))


def _fnet(p3, col0, w_f):
    bsz, seq, _ = p3.shape
    ng, gd, _ = w_f.shape
    n2 = FNET_N2
    n1 = seq // n2
    scale = 1.0 / math.sqrt(seq * gd)
    ar = np.arange
    cd, sd = _cos_sin(ar(gd)[:, None] * ar(gd)[None, :], gd)
    c1, s1 = _cos_sin(ar(n1)[:, None] * ar(n1)[None, :], n1)
    m1 = np.block([[c1, s1], [-s1, c1]])
    c2, s2 = _cos_sin(ar(n2)[None, None, :] * (ar(n1)[:, None, None] + n1 * ar(n2)[None, :, None]), seq)
    m1 = jnp.asarray(m1, F32)
    m2 = jnp.asarray(np.concatenate([c2, s2], axis=2), F32)

    wcat = pl.pallas_call(
        functools.partial(_fnet_weight_kernel, scale),
        out_shape=jax.ShapeDtypeStruct((ng, gd, 2 * gd), F32),
        name="fnet_weights",
    )(jnp.asarray(cd, F32), jnp.asarray(sd, F32), w_f)

    pitch_rows = n1 * (n2 + FNET_PITCH_PAD)
    const2 = lambda a: pl.BlockSpec(a.shape, lambda b, g: (0, 0))
    const3 = lambda a: pl.BlockSpec(a.shape, lambda b, g: (0, 0, 0), pipeline_mode=pl.Buffered(1))
    return pl.pallas_call(
        functools.partial(_fnet_kernel, n1, n2),
        grid=(bsz, ng),
        in_specs=[pl.BlockSpec((None, seq, gd), lambda b, g: (b, 0, col0 // gd + g)),
                  pl.BlockSpec((None, gd, 2 * gd), lambda b, g: (g, 0, 0)),
                  const2(m1), const3(m2)],
        out_specs=pl.BlockSpec((None, seq, gd), lambda b, g: (b, 0, g)),
        out_shape=jax.ShapeDtypeStruct((bsz, seq, ng * gd), F32),
        scratch_shapes=[pltpu.VMEM((pitch_rows, gd), F32)] * 4,
        compiler_params=_cparams(("parallel", "parallel")),
        name="fnet",
    )(p3, wcat, m1, m2)


def _odd_out_kernel(y0_ref, y1_ref, b0_ref, b1_ref, zc_ref, fn_ref, zd_ref, h_ref, lng_ref,
                    lnb_ref, bd_ref, wo_ref, fg_ref, o_ref):
    c = y0_ref.shape[1]
    bd = bd_ref[...]
    inv_n = 1.0 / HEAD_DIM
    ysum = y0_ref[...].astype(F32) + y1_ref[...].astype(F32)
    mean = _segsum(ysum, bd, 2) * inv_n
    cen = ysum - mean
    var = _segsum(cen * cen, bd, 1) * inv_n
    gn = cen * lax.rsqrt(var + GN_EPS) * lng_ref[...] + lnb_ref[...]
    yc = (gn + b0_ref[...].astype(F32) + b1_ref[...].astype(F32)) * _silu(zc_ref[...])
    yd = fn_ref[...] * _silu(zd_ref[...])
    out = h_ref[...]
    out = out + jnp.dot(yc.astype(BF16), wo_ref[0:c, :], preferred_element_type=F32)
    out = out + jnp.dot(yd.astype(BF16), wo_ref[c:, :], preferred_element_type=F32)
    ms = jnp.mean(out * out, axis=-1, keepdims=True)
    o_ref[...] = out * lax.rsqrt(ms + RMS_EPS) * fg_ref[...]


def _odd_out(y0, y1, b0, b1, p2d, zc_blk, fn, zd_blk, h2d, lnx_g, lnx_b, w_out_bf16, final_g, tm):
    t, d = h2d.shape
    c = y0.shape[1]
    cf = fn.shape[1]
    hd = np.arange(2 * HEAD_DIM) // HEAD_DIM
    bd = jnp.asarray(hd[:, None] == hd[None, :], BF16)
    rowblk = lambda w: pl.BlockSpec((tm, w), lambda i: (i, 0))
    full = lambda shape: pl.BlockSpec(shape, lambda i: (0, 0))
    return pl.pallas_call(
        _odd_out_kernel,
        grid=(t // tm,),
        in_specs=[rowblk(c), rowblk(c), rowblk(c), rowblk(c),
                  pl.BlockSpec((tm, c), lambda i: (i, zc_blk)),
                  rowblk(cf),
                  pl.BlockSpec((tm, cf), lambda i: (i, zd_blk)),
                  rowblk(d), full((1, c)), full((1, c)), full(bd.shape),
                  full(w_out_bf16.shape), full((1, d))],
        out_specs=rowblk(d),
        out_shape=jax.ShapeDtypeStruct((t, d), F32),
        compiler_params=_cparams(("parallel",)),
        name="odd_out",
    )(y0, y1, b0, b1, p2d, fn, p2d, h2d, lnx_g.reshape(1, c), lnx_b.reshape(1, c), bd,
      w_out_bf16, final_g.reshape(1, d))


WKV_LANES = 128
WKV_ROWS = 512
ODD_PROJ_ROWS = 512
ODD_OUT_ROWS = 256


def _odd_layer(h2d, bsz, seq, norm_g, w_in, mu, w0, w2, a0, a2, k_k, k_a, r_k, lnx_g, lnx_b,
               fnet_w, w_out, final_g):
    t, d = h2d.shape
    c = w0.shape[-1]
    rs = 3 * c + 2 * LORA
    cf = fnet_w.shape[0] * fnet_w.shape[1]
    w_perm = jnp.concatenate([w_in[:, rs:], w_in[:, :rs]], axis=1).astype(BF16)
    p2 = _norm_proj(h2d, norm_g, w_perm, tm=min(ODD_PROJ_ROWS, t), tn=w_perm.shape[1], out_dtype=F32)
    col0 = c + 2 * cf
    p3 = p2.reshape(bsz, seq, -1)
    lora = _lora_weights(w2, a2, WKV_LANES)
    rws = min(WKV_ROWS, seq)
    y0, b0, y1, b1 = _wkv(p3, col0, mu, w0, a0, lora, k_k, k_a, r_k.reshape(-1), WKV_LANES, rws)
    fn = _fnet(p3, c, fnet_w).reshape(t, cf)
    flat = lambda a: a.reshape(t, c)
    return _odd_out(flat(y0), flat(y1), flat(b0), flat(b1), p2, 0, fn, c // cf + 1, h2d,
                    lnx_g, lnx_b, w_out.astype(BF16), final_g, tm=min(ODD_OUT_ROWS, t))


def kernel(x, e_norm_g, e_w_in, e_conv_w, e_sgu_ln_g, e_sgu_ln_b, e_sgu_w, e_sgu_b, e_w_out,
           o_norm_g, o_w_in, o_mu, o_w0, o_w2, o_a0, o_a2, o_k_k, o_k_a, o_r_k, o_lnx_g, o_lnx_b,
           o_fnet_w, o_w_out, final_norm_g):
    bsz, seq, d = x.shape
    assert e_norm_g.shape[0] == 1 and o_norm_g.shape[0] == 1, "two-layer trunk: one even, one odd layer"
    h = x.reshape(bsz * seq, d)
    h = _even_layer(h, seq, e_norm_g[0], e_w_in[0], e_conv_w[0], e_sgu_ln_g[0], e_sgu_ln_b[0],
                    e_sgu_w[0], e_sgu_b[0], e_w_out[0])
    out = _odd_layer(h, bsz, seq, o_norm_g[0], o_w_in[0], o_mu[0], o_w0[0], o_w2[0], o_a0[0],
                     o_a2[0], o_k_k[0], o_k_a[0], o_r_k[0], o_lnx_g[0], o_lnx_b[0], o_fnet_w[0],
                     o_w_out[0], final_norm_g)
    return out.reshape(bsz, seq, d)
```

```python
import functools
import math

import numpy as np
import jax
import jax.numpy as jnp
from jax import lax
from jax.experimental import pallas as pl
from jax.experimental.pallas import tpu as pltpu

F32 = jnp.float32
BF16 = jnp.bfloat16

RMS_EPS = 1e-6
SGU_LN_EPS = 1e-5
GN_EPS = 64e-5
SGU_CHUNK = 128
SGU_GROUPS = 8
HEAD_DIM = 64
LORA = 64

V7X_SUBLANES = 8
V7X_VMEM_BYTES = 64 * 1024 * 1024
VMEM_LIMIT = V7X_VMEM_BYTES - 8 * 1024 * 1024


def _cparams(sem):
    return pltpu.CompilerParams(dimension_semantics=sem, vmem_limit_bytes=VMEM_LIMIT)


def _silu(z):
    return z * (1.0 / (1.0 + jnp.exp(-z)))


def _sigmoid(z):
    return 1.0 / (1.0 + jnp.exp(-z))


def _dot(a, b):
    return jnp.dot(a.astype(BF16), b.astype(BF16), preferred_element_type=F32)


def _split(a, n):
    parts = []
    rem = a
    for _ in range(n):
        p = rem.astype(BF16)
        parts.append(p)
        rem = rem - p.astype(F32)
    return parts


def _dot_exact_rhs(a, b_bf16, n=3):
    acc = None
    for p in _split(a, n):
        t = jnp.dot(p, b_bf16, preferred_element_type=F32)
        acc = t if acc is None else acc + t
    return acc


def _proj_kernel(x_ref, g_ref, w_ref, o_ref, hn_ref):
    @pl.when(pl.program_id(1) == 0)
    def _():
        x = x_ref[...]
        ms = jnp.mean(x * x, axis=-1, keepdims=True)
        hn_ref[...] = (x * lax.rsqrt(ms + RMS_EPS) * g_ref[...]).astype(BF16)

    o_ref[...] = jnp.dot(hn_ref[...], w_ref[...], preferred_element_type=F32).astype(o_ref.dtype)


def _norm_proj(x2d, g, w_bf16, tm, tn, out_dtype):
    t, d = x2d.shape
    n = w_bf16.shape[1]
    assert t % tm == 0 and n % tn == 0
    w_mode = dict(pipeline_mode=pl.Buffered(1)) if n == tn else {}
    return pl.pallas_call(
        _proj_kernel,
        grid=(t // tm, n // tn),
        in_specs=[
            pl.BlockSpec((tm, d), lambda i, j: (i, 0)),
            pl.BlockSpec((1, d), lambda i, j: (0, 0)),
            pl.BlockSpec((d, tn), lambda i, j: (0, j), **w_mode),
        ],
        out_specs=pl.BlockSpec((tm, tn), lambda i, j: (i, j)),
        out_shape=jax.ShapeDtypeStruct((t, n), out_dtype),
        scratch_shapes=[pltpu.VMEM((tm, d), BF16)],
        compiler_params=_cparams(("parallel", "arbitrary")),
        name="norm_proj",
    )(x2d, g.reshape(1, d), w_bf16)


def _even_kernel(seq, h_ref, hp_ref, hn_ref, ng_ref, wi_ref, cw_ref, lng_ref, lnb_ref, sw_ref,
                 sb_ref, wo_ref, o_ref):
    tm, d = h_ref.shape
    hr = hp_ref.shape[0]
    row0 = pl.program_id(0) * tm
    at_seq_start = (row0 % seq) == 0
    at_seq_end = ((row0 + tm) % seq) == 0

    h = h_ref[...]
    hx = jnp.concatenate([hp_ref[...], h, hn_ref[...]], axis=0)
    ms = jnp.mean(hx * hx, axis=-1, keepdims=True)
    hx = (hx * lax.rsqrt(ms + RMS_EPS) * ng_ref[...]).astype(BF16)
    hn = hx[hr:hr + tm]
    proj = lambda lhs, j: jnp.dot(lhs, wi_ref[:, j * d:(j + 1) * d], preferred_element_type=F32)

    xce = proj(hx, 0) * proj(hx, 2)
    xc = xce[hr:hr + tm]
    halo_prev = xce[hr - 1:hr]
    halo_next = xce[hr + tm:hr + tm + 1]
    ba, za = proj(hn, 1), proj(hn, 3)
    halo_prev = jnp.where(at_seq_start, 0.0, halo_prev)
    halo_next = jnp.where(at_seq_end, 0.0, halo_next)
    rows = lax.broadcasted_iota(jnp.int32, (tm, 1), 0)
    prev = jnp.where(rows == 0, halo_prev, pltpu.roll(xc, 1, 0))
    nxt = jnp.where(rows == tm - 1, halo_next, pltpu.roll(xc, tm - 1, 0))
    conv = cw_ref[0:1, :] * prev + cw_ref[1:2, :] * xc + cw_ref[2:3, :] * nxt
    ya = ba * conv * _silu(za)

    vb = proj(hn, 5)
    mu = jnp.mean(vb, axis=-1, keepdims=True)
    cen = vb - mu
    var = jnp.mean(cen * cen, axis=-1, keepdims=True)
    vn = (cen * lax.rsqrt(var + SGU_LN_EPS) * lng_ref[...] + lnb_ref[...]).astype(BF16)
    gw = d // SGU_GROUPS
    chunk_rows = []
    for n in range(tm // SGU_CHUNK):
        r0 = n * SGU_CHUNK
        cols = []
        for g in range(SGU_GROUPS):
            cols.append(jnp.dot(sw_ref[g], vn[r0:r0 + SGU_CHUNK, g * gw:(g + 1) * gw],
                                preferred_element_type=F32))
        chunk_rows.append(jnp.concatenate(cols, axis=1) + sb_ref[...])
    mixed = jnp.concatenate(chunk_rows, axis=0)
    yb = proj(hn, 4) * mixed * _silu(proj(hn, 6))

    out = h
    out = out + jnp.dot(ya.astype(BF16), wo_ref[0:d, :], preferred_element_type=F32)
    out = out + jnp.dot(yb.astype(BF16), wo_ref[d:2 * d, :], preferred_element_type=F32)
    o_ref[...] = out


EVEN_ROWS = 512


def _even_layer(h2d, seq, norm_g, w_in, conv_w, ln_g, ln_b, sgu_w, sgu_b, w_out):
    t, d = h2d.shape
    tm = min(EVEN_ROWS, t)
    hr = V7X_SUBLANES
    nblk = t // hr
    rpt = tm // hr
    bias_full = jnp.repeat(sgu_b.T, d // SGU_GROUPS, axis=1)
    sgu_w_bf16 = sgu_w.astype(BF16)
    w_in_bf16 = w_in.astype(BF16)
    w_out_bf16 = w_out.astype(BF16)
    const = lambda a: pl.BlockSpec(a.shape, lambda i: (0,) * a.ndim, pipeline_mode=pl.Buffered(1))
    row = lambda a: a.reshape(1, d)
    consts = (row(norm_g), w_in_bf16, conv_w, row(ln_g), row(ln_b), sgu_w_bf16, bias_full, w_out_bf16)
    return pl.pallas_call(
        functools.partial(_even_kernel, seq),
        grid=(t // tm,),
        in_specs=[
            pl.BlockSpec((tm, d), lambda i: (i, 0)),
            pl.BlockSpec((hr, d), lambda i: (jnp.maximum(i * rpt - 1, 0), 0)),
            pl.BlockSpec((hr, d), lambda i: (jnp.minimum((i + 1) * rpt, nblk - 1), 0)),
        ] + [const(a) for a in consts],
        out_specs=pl.BlockSpec((tm, d), lambda i: (i, 0)),
        out_shape=jax.ShapeDtypeStruct((t, d), F32),
        compiler_params=_cparams(("parallel",)),
        name="even_layer",
    )(h2d, h2d, h2d, *consts)


def _dot3(a, b):
    ah, al = _split(a, 2)
    bh, bl = _split(b, 2)
    d = lambda x, y: jnp.dot(x, y, preferred_element_type=F32)
    return d(ah, bh) + (d(ah, bl) + d(al, bh))


_NN = (((2,), (1,)), ((0,), (0,)))
_NT = (((2,), (2,)), ((0,), (0,)))


def _bmm(a, b, dims):
    return lax.dot_general(a.astype(BF16), b.astype(BF16), dims, preferred_element_type=F32)


def _segsum(x, bd_bf16, n=3):
    w = bd_bf16.shape[0]
    cols = [_dot_exact_rhs(x[:, j:j + w], bd_bf16, n) for j in range(0, x.shape[1], w)]
    return cols[0] if len(cols) == 1 else jnp.concatenate(cols, axis=1)


WKV_CHUNK = 64


WKV_DIR_INPUTS = 12
WKV_DIR_SCRATCH = 9


def _wkv_kernel(*refs):
    ni, ns = WKV_DIR_INPUTS, WKV_DIR_SCRATCH
    shared = refs[2 * ni:2 * ni + 4]
    outs = refs[2 * ni + 4:2 * ni + 8]
    scratch = refs[2 * ni + 8:]

    step = pl.program_id(2)
    last = pl.num_programs(2) - 1

    @pl.when(step == 0)
    def _():
        for ref in scratch:
            ref[...] = jnp.zeros_like(ref)

    def body(with_local):
        per_dir = [_wkv_direction(d, refs[d * ni:(d + 1) * ni], shared, outs[2 * d:2 * d + 2],
                                  scratch[d * ns:(d + 1) * ns], with_local) for d in range(2)]
        chains = [c for c, _ in per_dir]
        groups = [g for pair in zip(*[gs for _, gs in per_dir]) for g in pair]
        start = {id(g): i * WKV_SKEW for i, g in enumerate(groups)}
        slot = 0
        while groups:
            for gen in list(groups):
                if slot >= start[id(gen)] and next(gen, "done") == "done":
                    groups.remove(gen)
            for chain in chains:
                next(chain, None)
            slot += 1
        for chain in chains:
            for _ in chain:
                pass

    pl.when(step < last)(functools.partial(body, True))
    pl.when(step == last)(functools.partial(body, False))


def _wkv_direction(direction, ins, shared, outs, scratch, with_local):
    (r_ref, k_ref, v_ref, wa_ref, mur_ref, muk_ref, muv_ref, muwa_ref, w0_ref, a0_ref, lora_ref,
     tri_ref) = ins
    kk_ref, ka_ref, rk_ref, bd_ref = shared
    y_ref, bon_ref = outs
    h_ref, cr_ref, ck_ref, cv_ref, cwa_ref, lhs_ref, n_ref, y0_ref, bonp_ref = scratch
    rws, gw = r_ref.shape
    L = WKV_CHUNK
    nch = rws // L

    lane_head = lax.broadcasted_iota(jnp.int32, (1, gw), 1) // HEAD_DIM

    def recurrence():
        h = h_ref[...]
        ys = [None] * nch
        for c in (range(nch) if direction == 0 else range(nch - 1, -1, -1)):
            hb = h.astype(BF16)
            hbd = jnp.concatenate([jnp.where(lane_head == i, hb, 0.0)
                                   for i in range(gw // HEAD_DIM)], axis=0)
            both = jnp.dot(lhs_ref[c], hbd, preferred_element_type=F32)
            ys[c] = both[:L] + y0_ref[c]
            h = both[L:] + n_ref[c]
            yield
        h_ref[...] = h
        y_ref[...] = jnp.concatenate(ys, axis=0).astype(y_ref.dtype)

    chain = recurrence()
    bon_ref[...] = bonp_ref[...].astype(bon_ref.dtype)
    if not with_local:
        return chain, []

    streams = ((r_ref, cr_ref), (k_ref, ck_ref), (v_ref, cv_ref), (wa_ref, cwa_ref))
    edge = [c_ref[...] for _, c_ref in streams]
    far = rws - 1 if direction == 0 else 0
    for x_ref, c_ref in streams:
        c_ref[...] = x_ref[far:far + 1, :]

    ncg = min(WKV_GROUP, nch)
    groups = [_wkv_local_stages(direction, g0, ncg, edge, r_ref, k_ref, v_ref, wa_ref, mur_ref,
                                muk_ref, muv_ref, muwa_ref, w0_ref, a0_ref, kk_ref, ka_ref, rk_ref,
                                lora_ref, tri_ref, bd_ref, lhs_ref, n_ref, y0_ref, bonp_ref, chain)
              for g0 in range(0, nch, ncg)]
    return chain, groups


WKV_GROUP = 8
WKV_SKEW = 0


def _wkv_local_stages(direction, c0, ncg, edge, r_ref, k_ref, v_ref, wa_ref, mur_ref, muk_ref,
                      muv_ref, muwa_ref, w0_ref, a0_ref, kk_ref, ka_ref, rk_ref, lora_ref, tri_ref,
                      bd_ref, lhs_ref, n_ref, y0_ref, bonp_ref, chain):
    L = WKV_CHUNK
    nch = ncg
    rws, gw = r_ref.shape
    nheads = gw // HEAD_DIM
    rg = ncg * L
    row0 = c0 * L
    rows = lax.broadcasted_iota(jnp.int32, (rg, 1), 0)

    def token_shift(x_ref, edge_row, mu):
        x = x_ref[row0:row0 + rg, :]
        if direction == 0:
            nb = edge_row if row0 == 0 else x_ref[row0 - 1:row0, :]
            sh = jnp.where(rows == 0, nb, pltpu.roll(x, 1, 0))
        else:
            nb = edge_row if row0 + rg == rws else x_ref[row0 + rg:row0 + rg + 1, :]
            sh = jnp.where(rows == rg - 1, nb, pltpu.roll(x, rg - 1, 0))
        return x + mu * (sh - x)

    r = token_shift(r_ref, edge[0], mur_ref[...])
    k = token_shift(k_ref, edge[1], muk_ref[...])
    v = token_shift(v_ref, edge[2], muv_ref[...])
    wa = token_shift(wa_ref, edge[3], muwa_ref[...])
    yield

    lane_wa = lax.broadcasted_iota(jnp.int32, (1, wa.shape[1]), 1)
    wa = jnp.where(lane_wa < LORA, jnp.tanh(wa), wa)
    za = _dot(wa, lora_ref[...])
    zw = w0_ref[...] + za[:, :gw]
    a = _sigmoid(a0_ref[...] + za[:, gw:])
    lw = -math.exp(-0.5) * _sigmoid(zw)

    bd = bd_ref[...]
    kk = k * kk_ref[...]
    kk = kk * lax.rsqrt(jnp.maximum(_segsum(kk * kk, bd, 1), 1e-12))
    k2 = k * (1.0 + (a - 1.0) * ka_ref[...])
    bonp_ref[row0:row0 + rg, :] = _segsum(r * k2 * rk_ref[...], bd, 1) * v
    yield

    tri2 = jnp.broadcast_to(jnp.concatenate([tri_ref[...]] * 2, axis=1), (nch, L, 2 * L))
    parts = jnp.concatenate([p.reshape(nch, L, gw) for p in _split(lw, 2)], axis=1)
    cs = lax.dot_general(tri2, parts, _NN, preferred_element_type=F32).reshape(rg, gw)
    yield
    e_inc = jnp.exp(cs)
    e_inv = jnp.exp(-cs)
    e_exc = jnp.exp(cs - lw)
    to3 = lambda x: x.reshape(nch, L, gw)
    rt = to3(r * e_inc)
    kt = to3(k2 * e_inv)
    at = to3(-kk * e_exc)
    bt = to3(kk * a * e_inv)
    v3 = to3(v)
    last = L - 1 if direction == 0 else 0
    e_end = to3(e_inc)[:, last:last + 1, :]
    kh = kt * e_end
    bh = bt * e_end

    lane = lax.broadcasted_iota(jnp.int32, (1, 1, gw), 2)
    head_masks = [(lane // HEAD_DIM) == h for h in range(nheads)]

    def blockdiag(x3):
        return jnp.concatenate([jnp.where(m, x3, 0.0) for m in head_masks], axis=1)

    def hmm(lp, *xps):
        rhs = [blockdiag(xp.astype(BF16)) for xp in xps]
        rhs = rhs[0] if len(rhs) == 1 else jnp.concatenate(rhs, axis=2)
        return _bmm(lp, rhs, _NN)

    x_ar = jnp.concatenate([at, rt], axis=1)
    g = _bmm(x_ar, jnp.concatenate([blockdiag(bt.astype(BF16)), blockdiag(kt.astype(BF16))], axis=1),
             _NT)
    g_b, g_k = g[:, :, :gw], g[:, :, gw:]
    yield

    t_idx = lax.broadcasted_iota(jnp.int32, (1, L, gw), 1)
    s_idx = lax.broadcasted_iota(jnp.int32, (1, L, gw), 2) % L
    if direction == 0:
        strict, incl = s_idx < t_idx, s_idx <= t_idx
    else:
        strict, incl = s_idx > t_idx, s_idx >= t_idx
    a_ab = jnp.where(strict, g_b[:, :L], 0.0)
    a_rb = jnp.where(incl, g_b[:, L:], 0.0)
    a_ak = jnp.where(strict, g_k[:, :L], 0.0)
    a_rk = jnp.where(incl, g_k[:, L:], 0.0)

    tinv = jnp.where(s_idx == t_idx, 1.0, 0.0) + a_ab
    apow = hmm(a_ab, a_ab)
    nsq = int(math.log2(L)) - 1
    for i in range(nsq):
        yield
        if i + 1 < nsq:
            both = hmm(apow, apow, tinv)
            apow, tinv = both[:, :, :gw], tinv + both[:, :, gw:]
        else:
            tinv = tinv + hmm(apow, tinv)

    akv = _bmm(jnp.concatenate([a_ak, a_rk], axis=1), blockdiag(v3.astype(BF16)), _NN)
    wu = hmm(tinv, at, akv[:, :L])
    wt, u0 = wu[:, :, :gw], wu[:, :, gw:]
    yield
    ry = hmm(a_rb, wt, u0)
    rh = rt + ry[:, :, :gw]
    y0 = ry[:, :, gw:] + akv[:, L:]

    bk_t = jnp.swapaxes(jnp.concatenate([bh, kh], axis=1), 1, 2)
    bk_pack = jnp.concatenate([bk_t[:, i * HEAD_DIM:(i + 1) * HEAD_DIM] for i in range(nheads)],
                              axis=2).astype(BF16)
    rhs = jnp.concatenate([wu, jnp.concatenate([jnp.zeros_like(v3), v3], axis=2)],
                          axis=1).astype(BF16)
    lane2 = lax.broadcasted_iota(jnp.int32, (1, 1, 2 * gw), 2) % gw // HEAD_DIM
    rhs_heads = jnp.concatenate([jnp.where(lane2 == i, rhs, 0.0) for i in range(nheads)], axis=1)
    mn = _bmm(bk_pack, rhs_heads, _NN)
    ri = lax.broadcasted_iota(jnp.int32, (1, HEAD_DIM, gw), 1)
    ci = lax.broadcasted_iota(jnp.int32, (1, HEAD_DIM, gw), 2) % HEAD_DIM
    m_mat = mn[:, :, :gw] + jnp.where(ri == ci, e_end, 0.0)
    yield
    for _ in chain:
        pass
    lhs_ref[c0:c0 + ncg] = jnp.concatenate([rh, m_mat], axis=1).astype(BF16)
    n_ref[c0:c0 + ncg] = mn[:, :, gw:]
    y0_ref[c0:c0 + ncg] = y0


def _wkv(p3, col0, mu, w0, a0, lora_w, k_k, k_a, r_k, gw, rws):
    bsz, seq, _ = p3.shape
    c = w0.shape[-1]
    ng = c // gw
    nstep = seq // rws
    L = WKV_CHUNK
    nch = rws // L
    wa_w = 2 * LORA
    t = np.arange(L)
    hd = np.arange(gw) // HEAD_DIM
    bd = jnp.asarray(hd[:, None] == hd[None, :], BF16)
    row = lambda x: x.reshape(1, -1)
    vec = pl.BlockSpec((1, gw), lambda b, g, s: (0, g))
    full2 = lambda shape: pl.BlockSpec(shape, lambda b, g, s: (0, 0))

    in_specs, args, out_specs, scratch = [], [], [], []
    for direction in range(2):
        blk = (lambda s: s) if direction == 0 else (lambda s: nstep - 1 - s)
        rb = lambda s, blk=blk: blk(jnp.minimum(s, nstep - 1))
        ob = lambda s, blk=blk: blk(jnp.maximum(s - 1, 0))
        stream = lambda off, rb=rb: pl.BlockSpec((None, rws, gw),
                                                 lambda b, g, s: (b, rb(s), off // gw + g))
        tri = (t[None, :] <= t[:, None]) if direction == 0 else (t[None, :] >= t[:, None])
        mu_d = mu[direction]
        in_specs += [
            stream(col0), stream(col0 + c), stream(col0 + 2 * c),
            pl.BlockSpec((None, rws, wa_w),
                         lambda b, g, s, rb=rb: (b, rb(s), (col0 + 3 * c) // wa_w)),
            vec, vec, vec, full2((1, wa_w)), vec, vec,
            pl.BlockSpec((None, wa_w, 2 * gw), lambda b, g, s: (g, 0, 0)),
            full2((L, L)),
        ]
        args += [p3, p3, p3, p3, row(mu_d[:c]), row(mu_d[c:2 * c]), row(mu_d[2 * c:3 * c]),
                 row(mu_d[3 * c:]), row(w0[direction]), row(a0[direction]), lora_w[direction],
                 jnp.asarray(tri, BF16)]
        out_specs += [pl.BlockSpec((None, rws, gw), lambda b, g, s, ob=ob: (b, ob(s), g))] * 2
        scratch += [pltpu.VMEM((HEAD_DIM, gw), F32), pltpu.VMEM((1, gw), F32), pltpu.VMEM((1, gw), F32),
                    pltpu.VMEM((1, gw), F32), pltpu.VMEM((1, wa_w), F32),
                    pltpu.VMEM((nch, L + HEAD_DIM, gw), BF16), pltpu.VMEM((nch, HEAD_DIM, gw), F32),
                    pltpu.VMEM((nch, L, gw), F32), pltpu.VMEM((rws, gw), F32)]
    assert len(in_specs) == 2 * WKV_DIR_INPUTS and len(scratch) == 2 * WKV_DIR_SCRATCH
    in_specs += [vec, vec, vec, full2((gw, gw))]
    args += [row(k_k), row(k_a), row(r_k), bd]
    return pl.pallas_call(
        _wkv_kernel,
        grid=(bsz, ng, nstep + 1),
        in_specs=in_specs,
        out_specs=out_specs,
        out_shape=[jax.ShapeDtypeStruct((bsz, seq, c), BF16)] * 4,
        scratch_shapes=scratch,
        compiler_params=_cparams(("parallel", "parallel", "arbitrary")),
        name="wkv",
    )(*args)


def _lora_weights(w2, a2, gw):
    nd, lo, c = w2.shape
    ng = c // gw
    w2g = w2.reshape(nd, lo, ng, gw).transpose(0, 2, 1, 3)
    a2g = a2.reshape(nd, lo, ng, gw).transpose(0, 2, 1, 3)
    z = jnp.zeros_like(w2g)
    top = jnp.concatenate([w2g, z], axis=3)
    bot = jnp.concatenate([z, a2g], axis=3)
    return jnp.concatenate([top, bot], axis=2).astype(BF16)


FNET_N2 = 128


FNET_PITCH_PAD = 8
FNET_UNROLL = 8


def _cos_sin(idx, period):
    ang = 2.0 * np.pi * (idx % period) / period
    return np.cos(ang), np.sin(ang)


def _fnet_weight_kernel(scale, c_ref, s_ref, w_ref, o_ref):
    for g in range(w_ref.shape[0]):
        wc = _dot3(c_ref[...], w_ref[g])
        ws = _dot3(s_ref[...], w_ref[g])
        o_ref[g] = jnp.concatenate([wc, -ws], axis=1) * scale


def _fnet_kernel(n1, n2, f_ref, w_ref, m1_ref, m2_ref, o_ref, gr_ref, gi_ref, zr_ref, zi_ref):
    gd = f_ref.shape[1]
    pitch = n2 + FNET_PITCH_PAD
    w = w_ref[...].astype(BF16)

    def chan_body(s1, carry):
        src = pl.ds(pl.multiple_of(s1 * n2, n2), n2)
        dst = pl.ds(pl.multiple_of(s1 * pitch, V7X_SUBLANES), n2)
        z = jnp.dot(f_ref[src, :].astype(BF16), w, preferred_element_type=F32)
        gr_ref[dst, :] = z[:, :gd]
        gi_ref[dst, :] = z[:, gd:]
        return carry

    lax.fori_loop(0, n1, chan_body, 0, unroll=min(FNET_UNROLL, n1))

    m1 = m1_ref[...].astype(BF16)

    def stage1_body(s2, carry):
        idx = pl.ds(s2, n1, stride=pitch)
        x = jnp.concatenate([gr_ref[idx, :], gi_ref[idx, :]], axis=0)
        z = jnp.dot(m1, x.astype(BF16), preferred_element_type=F32)
        zr_ref[idx, :] = z[:n1]
        zi_ref[idx, :] = z[n1:]
        return carry

    lax.fori_loop(0, n2, stage1_body, 0, unroll=2 * FNET_UNROLL)

    def stage2_body(s1, carry):
        src = pl.ds(pl.multiple_of(s1 * pitch, V7X_SUBLANES), n2)
        zz = jnp.concatenate([zr_ref[src, :], zi_ref[src, :]], axis=0)
        gr_ref[src, :] = jnp.dot(m2_ref[s1].astype(BF16), zz.astype(BF16),
                                 preferred_element_type=F32)
        return carry

    lax.fori_loop(0, n1, stage2_body, 0, unroll=min(FNET_UNROLL, n1))

    def reorder_body(s2, carry):
        dst = pl.ds(pl.multiple_of(s2 * n1, V7X_SUBLANES), n1)
        o_ref[dst, :] = gr_ref[pl.ds(s2, n1, stride=pitch), :]
        return carry

    lax.fori_loop(0, n2, reorder_body, 0, unroll=2 * FNET_UNROLL)


def _fnet(p3, col0, w_f):
    bsz, seq, _ = p3.shape
    ng, gd, _ = w_f.shape
    n2 = FNET_N2
    n1 = seq // n2
    scale = 1.0 / math.sqrt(seq * gd)
    ar = np.arange
    cd, sd = _cos_sin(ar(gd)[:, None] * ar(gd)[None, :], gd)
    c1, s1 = _cos_sin(ar(n1)[:, None] * ar(n1)[None, :], n1)
    m1 = np.block([[c1, s1], [-s1, c1]])
    c2, s2 = _cos_sin(ar(n2)[None, None, :] * (ar(n1)[:, None, None] + n1 * ar(n2)[None, :, None]), seq)
    m1 = jnp.asarray(m1, F32)
    m2 = jnp.asarray(np.concatenate([c2, s2], axis=2), F32)

    wcat = pl.pallas_call(
        functools.partial(_fnet_weight_kernel, scale),
        out_shape=jax.ShapeDtypeStruct((ng, gd, 2 * gd), F32),
        name="fnet_weights",
    )(jnp.asarray(cd, F32), jnp.asarray(sd, F32), w_f)

    pitch_rows = n1 * (n2 + FNET_PITCH_PAD)
    const2 = lambda a: pl.BlockSpec(a.shape, lambda b, g: (0, 0))
    const3 = lambda a: pl.BlockSpec(a.shape, lambda b, g: (0, 0, 0), pipeline_mode=pl.Buffered(1))
    return pl.pallas_call(
        functools.partial(_fnet_kernel, n1, n2),
        grid=(bsz, ng),
        in_specs=[pl.BlockSpec((None, seq, gd), lambda b, g: (b, 0, col0 // gd + g)),
                  pl.BlockSpec((None, gd, 2 * gd), lambda b, g: (g, 0, 0)),
                  const2(m1), const3(m2)],
        out_specs=pl.BlockSpec((None, seq, gd), lambda b, g: (b, 0, g)),
        out_shape=jax.ShapeDtypeStruct((bsz, seq, ng * gd), F32),
        scratch_shapes=[pltpu.VMEM((pitch_rows, gd), F32)] * 4,
        compiler_params=_cparams(("parallel", "parallel")),
        name="fnet",
    )(p3, wcat, m1, m2)


def _odd_out_kernel(y0_ref, y1_ref, b0_ref, b1_ref, zc_ref, fn_ref, zd_ref, h_ref, lng_ref,
                    lnb_ref, bd_ref, wo_ref, fg_ref, o_ref):
    c = y0_ref.shape[1]
    bd = bd_ref[...]
    inv_n = 1.0 / HEAD_DIM
    ysum = y0_ref[...].astype(F32) + y1_ref[...].astype(F32)
    mean = _segsum(ysum, bd, 2) * inv_n
    cen = ysum - mean
    var = _segsum(cen * cen, bd, 1) * inv_n
    gn = cen * lax.rsqrt(var + GN_EPS) * lng_ref[...] + lnb_ref[...]
    yc = (gn + b0_ref[...].astype(F32) + b1_ref[...].astype(F32)) * _silu(zc_ref[...])
    yd = fn_ref[...] * _silu(zd_ref[...])
    out = h_ref[...]
    out = out + jnp.dot(yc.astype(BF16), wo_ref[0:c, :], preferred_element_type=F32)
    out = out + jnp.dot(yd.astype(BF16), wo_ref[c:, :], preferred_element_type=F32)
    ms = jnp.mean(out * out, axis=-1, keepdims=True)
    o_ref[...] = out * lax.rsqrt(ms + RMS_EPS) * fg_ref[...]


def _odd_out(y0, y1, b0, b1, p2d, zc_blk, fn, zd_blk, h2d, lnx_g, lnx_b, w_out_bf16, final_g, tm):
    t, d = h2d.shape
    c = y0.shape[1]
    cf = fn.shape[1]
    hd = np.arange(2 * HEAD_DIM) // HEAD_DIM
    bd = jnp.asarray(hd[:, None] == hd[None, :], BF16)
    rowblk = lambda w: pl.BlockSpec((tm, w), lambda i: (i, 0))
    full = lambda shape: pl.BlockSpec(shape, lambda i: (0, 0))
    return pl.pallas_call(
        _odd_out_kernel,
        grid=(t // tm,),
        in_specs=[rowblk(c), rowblk(c), rowblk(c), rowblk(c),
                  pl.BlockSpec((tm, c), lambda i: (i, zc_blk)),
                  rowblk(cf),
                  pl.BlockSpec((tm, cf), lambda i: (i, zd_blk)),
                  rowblk(d), full((1, c)), full((1, c)), full(bd.shape),
                  full(w_out_bf16.shape), full((1, d))],
        out_specs=rowblk(d),
        out_shape=jax.ShapeDtypeStruct((t, d), F32),
        compiler_params=_cparams(("parallel",)),
        name="odd_out",
    )(y0, y1, b0, b1, p2d, fn, p2d, h2d, lnx_g.reshape(1, c), lnx_b.reshape(1, c), bd,
      w_out_bf16, final_g.reshape(1, d))


WKV_LANES = 128
WKV_ROWS = 512
ODD_PROJ_ROWS = 512
ODD_OUT_ROWS = 512


def _odd_layer(h2d, bsz, seq, norm_g, w_in, mu, w0, w2, a0, a2, k_k, k_a, r_k, lnx_g, lnx_b,
               fnet_w, w_out, final_g):
    t, d = h2d.shape
    c = w0.shape[-1]
    rs = 3 * c + 2 * LORA
    cf = fnet_w.shape[0] * fnet_w.shape[1]
    w_perm = jnp.concatenate([w_in[:, rs:], w_in[:, :rs]], axis=1).astype(BF16)
    p2 = _norm_proj(h2d, norm_g, w_perm, tm=min(ODD_PROJ_ROWS, t), tn=w_perm.shape[1], out_dtype=F32)
    col0 = c + 2 * cf
    p3 = p2.reshape(bsz, seq, -1)
    lora = _lora_weights(w2, a2, WKV_LANES)
    rws = min(WKV_ROWS, seq)
    y0, b0, y1, b1 = _wkv(p3, col0, mu, w0, a0, lora, k_k, k_a, r_k.reshape(-1), WKV_LANES, rws)
    fn = _fnet(p3, c, fnet_w).reshape(t, cf)
    flat = lambda a: a.reshape(t, c)
    return _odd_out(flat(y0), flat(y1), flat(b0), flat(b1), p2, 0, fn, c // cf + 1, h2d,
                    lnx_g, lnx_b, w_out.astype(BF16), final_g, tm=min(ODD_OUT_ROWS, t))


def kernel(x, e_norm_g, e_w_in, e_conv_w, e_sgu_ln_g, e_sgu_ln_b, e_sgu_w, e_sgu_b, e_w_out,
           o_norm_g, o_w_in, o_mu, o_w0, o_w2, o_a0, o_a2, o_k_k, o_k_a, o_r_k, o_lnx_g, o_lnx_b,
           o_fnet_w, o_w_out, final_norm_g):
    bsz, seq, d = x.shape
    assert e_norm_g.shape[0] == 1 and o_norm_g.shape[0] == 1, "two-layer trunk: one even, one odd layer"
    h = x.reshape(bsz * seq, d)
    h = _even_layer(h, seq, e_norm_g[0], e_w_in[0], e_conv_w[0], e_sgu_ln_g[0], e_sgu_ln_b[0],
                    e_sgu_w[0], e_sgu_b[0], e_w_out[0])
    out = _odd_layer(h, bsz, seq, o_norm_g[0], o_w_in[0], o_mu[0], o_w0[0], o_w2[0], o_a0[0],
                     o_a2[0], o_k_k[0], o_k_a[0], o_r_k[0], o_lnx_g[0], o_lnx_b[0], o_fnet_w[0],
                     o_w_out[0], final_norm_g)
    return out.reshape(bsz, seq, d)
```

```python
import functools
import math

import numpy as np
import jax
import jax.numpy as jnp
from jax import lax
from jax.experimental import pallas as pl
from jax.experimental.pallas import tpu as pltpu

F32 = jnp.float32
BF16 = jnp.bfloat16

RMS_EPS = 1e-6
SGU_LN_EPS = 1e-5
GN_EPS = 64e-5
SGU_CHUNK = 128
SGU_GROUPS = 8
HEAD_DIM = 64
LORA = 64

V7X_SUBLANES = 8
V7X_VMEM_BYTES = 64 * 1024 * 1024
VMEM_LIMIT = V7X_VMEM_BYTES - 8 * 1024 * 1024


def _cparams(sem):
    return pltpu.CompilerParams(dimension_semantics=sem, vmem_limit_bytes=VMEM_LIMIT)


def _silu(z):
    return z * (1.0 / (1.0 + jnp.exp(-z)))


def _sigmoid(z):
    return 1.0 / (1.0 + jnp.exp(-z))


def _dot(a, b):
    return jnp.dot(a.astype(BF16), b.astype(BF16), preferred_element_type=F32)


def _split(a, n):
    parts = []
    rem = a
    for _ in range(n):
        p = rem.astype(BF16)
        parts.append(p)
        rem = rem - p.astype(F32)
    return parts


def _dot_exact_rhs(a, b_bf16, n=3):
    acc = None
    for p in _split(a, n):
        t = jnp.dot(p, b_bf16, preferred_element_type=F32)
        acc = t if acc is None else acc + t
    return acc


def _proj_kernel(x_ref, g_ref, w_ref, o_ref, hn_ref):
    @pl.when(pl.program_id(1) == 0)
    def _():
        x = x_ref[...]
        ms = jnp.mean(x * x, axis=-1, keepdims=True)
        hn_ref[...] = (x * lax.rsqrt(ms + RMS_EPS) * g_ref[...]).astype(BF16)

    o_ref[...] = jnp.dot(hn_ref[...], w_ref[...], preferred_element_type=F32).astype(o_ref.dtype)


def _norm_proj(x2d, g, w_bf16, tm, tn, out_dtype):
    t, d = x2d.shape
    n = w_bf16.shape[1]
    assert t % tm == 0 and n % tn == 0
    w_mode = dict(pipeline_mode=pl.Buffered(1)) if n == tn else {}
    return pl.pallas_call(
        _proj_kernel,
        grid=(t // tm, n // tn),
        in_specs=[
            pl.BlockSpec((tm, d), lambda i, j: (i, 0)),
            pl.BlockSpec((1, d), lambda i, j: (0, 0)),
            pl.BlockSpec((d, tn), lambda i, j: (0, j), **w_mode),
        ],
        out_specs=pl.BlockSpec((tm, tn), lambda i, j: (i, j)),
        out_shape=jax.ShapeDtypeStruct((t, n), out_dtype),
        scratch_shapes=[pltpu.VMEM((tm, d), BF16)],
        compiler_params=_cparams(("parallel", "arbitrary")),
        name="norm_proj",
    )(x2d, g.reshape(1, d), w_bf16)


def _even_kernel(seq, h_ref, hp_ref, hn_ref, ng_ref, wi_ref, cw_ref, lng_ref, lnb_ref, sw_ref,
                 sb_ref, wo_ref, o_ref):
    tm, d = h_ref.shape
    hr = hp_ref.shape[0]
    row0 = pl.program_id(0) * tm
    at_seq_start = (row0 % seq) == 0
    at_seq_end = ((row0 + tm) % seq) == 0

    h = h_ref[...]
    hx = jnp.concatenate([hp_ref[...], h, hn_ref[...]], axis=0)
    ms = jnp.mean(hx * hx, axis=-1, keepdims=True)
    hx = (hx * lax.rsqrt(ms + RMS_EPS) * ng_ref[...]).astype(BF16)
    hn = hx[hr:hr + tm]
    proj = lambda lhs, j: jnp.dot(lhs, wi_ref[:, j * d:(j + 1) * d], preferred_element_type=F32)

    xce = proj(hx, 0) * proj(hx, 2)
    xc = xce[hr:hr + tm]
    halo_prev = xce[hr - 1:hr]
    halo_next = xce[hr + tm:hr + tm + 1]
    ba, za = proj(hn, 1), proj(hn, 3)
    halo_prev = jnp.where(at_seq_start, 0.0, halo_prev)
    halo_next = jnp.where(at_seq_end, 0.0, halo_next)
    rows = lax.broadcasted_iota(jnp.int32, (tm, 1), 0)
    prev = jnp.where(rows == 0, halo_prev, pltpu.roll(xc, 1, 0))
    nxt = jnp.where(rows == tm - 1, halo_next, pltpu.roll(xc, tm - 1, 0))
    conv = cw_ref[0:1, :] * prev + cw_ref[1:2, :] * xc + cw_ref[2:3, :] * nxt
    ya = ba * conv * _silu(za)

    vb = proj(hn, 5)
    mu = jnp.mean(vb, axis=-1, keepdims=True)
    cen = vb - mu
    var = jnp.mean(cen * cen, axis=-1, keepdims=True)
    vn = (cen * lax.rsqrt(var + SGU_LN_EPS) * lng_ref[...] + lnb_ref[...]).astype(BF16)
    gw = d // SGU_GROUPS
    chunk_rows = []
    for n in range(tm // SGU_CHUNK):
        r0 = n * SGU_CHUNK
        cols = []
        for g in range(SGU_GROUPS):
            cols.append(jnp.dot(sw_ref[g], vn[r0:r0 + SGU_CHUNK, g * gw:(g + 1) * gw],
                                preferred_element_type=F32))
        chunk_rows.append(jnp.concatenate(cols, axis=1) + sb_ref[...])
    mixed = jnp.concatenate(chunk_rows, axis=0)
    yb = proj(hn, 4) * mixed * _silu(proj(hn, 6))

    out = h
    out = out + jnp.dot(ya.astype(BF16), wo_ref[0:d, :], preferred_element_type=F32)
    out = out + jnp.dot(yb.astype(BF16), wo_ref[d:2 * d, :], preferred_element_type=F32)
    o_ref[...] = out


EVEN_ROWS = 512


def _even_layer(h2d, seq, norm_g, w_in, conv_w, ln_g, ln_b, sgu_w, sgu_b, w_out):
    t, d = h2d.shape
    tm = min(EVEN_ROWS, t)
    hr = V7X_SUBLANES
    nblk = t // hr
    rpt = tm // hr
    bias_full = jnp.repeat(sgu_b.T, d // SGU_GROUPS, axis=1)
    sgu_w_bf16 = sgu_w.astype(BF16)
    w_in_bf16 = w_in.astype(BF16)
    w_out_bf16 = w_out.astype(BF16)
    const = lambda a: pl.BlockSpec(a.shape, lambda i: (0,) * a.ndim, pipeline_mode=pl.Buffered(1))
    row = lambda a: a.reshape(1, d)
    consts = (row(norm_g), w_in_bf16, conv_w, row(ln_g), row(ln_b), sgu_w_bf16, bias_full, w_out_bf16)
    return pl.pallas_call(
        functools.partial(_even_kernel, seq),
        grid=(t // tm,),
        in_specs=[
            pl.BlockSpec((tm, d), lambda i: (i, 0)),
            pl.BlockSpec((hr, d), lambda i: (jnp.maximum(i * rpt - 1, 0), 0)),
            pl.BlockSpec((hr, d), lambda i: (jnp.minimum((i + 1) * rpt, nblk - 1), 0)),
        ] + [const(a) for a in consts],
        out_specs=pl.BlockSpec((tm, d), lambda i: (i, 0)),
        out_shape=jax.ShapeDtypeStruct((t, d), F32),
        compiler_params=_cparams(("parallel",)),
        name="even_layer",
    )(h2d, h2d, h2d, *consts)


def _dot3(a, b):
    ah, al = _split(a, 2)
    bh, bl = _split(b, 2)
    d = lambda x, y: jnp.dot(x, y, preferred_element_type=F32)
    return d(ah, bh) + (d(ah, bl) + d(al, bh))


_NN = (((2,), (1,)), ((0,), (0,)))
_NT = (((2,), (2,)), ((0,), (0,)))


def _bmm(a, b, dims):
    return lax.dot_general(a.astype(BF16), b.astype(BF16), dims, preferred_element_type=F32)


def _segsum(x, bd_bf16, n=3):
    w = bd_bf16.shape[0]
    cols = [_dot_exact_rhs(x[:, j:j + w], bd_bf16, n) for j in range(0, x.shape[1], w)]
    return cols[0] if len(cols) == 1 else jnp.concatenate(cols, axis=1)


WKV_CHUNK = 64


WKV_DIR_INPUTS = 12
WKV_DIR_SCRATCH = 9


def _wkv_kernel(nstep, *refs):
    ni, ns = WKV_DIR_INPUTS, WKV_DIR_SCRATCH
    shared = refs[2 * ni:2 * ni + 4]
    outs = refs[2 * ni + 4:2 * ni + 8]
    scratch = refs[2 * ni + 8:]

    step = pl.program_id(0)
    last = pl.num_programs(0) - 1
    chain_start = (step % nstep) == 0

    @pl.when(step == 0)
    def _():
        for ref in scratch:
            ref[...] = jnp.zeros_like(ref)

    def body(with_local):
        per_dir = [_wkv_direction(d, refs[d * ni:(d + 1) * ni], shared, outs[2 * d:2 * d + 2],
                                  scratch[d * ns:(d + 1) * ns], with_local, chain_start)
                   for d in range(2)]
        chains = [c for c, _ in per_dir]
        groups = [g for pair in zip(*[gs for _, gs in per_dir]) for g in pair]
        start = {id(g): i * WKV_SKEW for i, g in enumerate(groups)}
        slot = 0
        while groups:
            for gen in list(groups):
                if slot >= start[id(gen)] and next(gen, "done") == "done":
                    groups.remove(gen)
            for chain in chains:
                next(chain, None)
            slot += 1
        for chain in chains:
            for _ in chain:
                pass

    pl.when(step < last)(functools.partial(body, True))
    pl.when(step == last)(functools.partial(body, False))


def _wkv_direction(direction, ins, shared, outs, scratch, with_local, chain_start):
    (r_ref, k_ref, v_ref, wa_ref, mur_ref, muk_ref, muv_ref, muwa_ref, w0_ref, a0_ref, lora_ref,
     tri_ref) = ins
    kk_ref, ka_ref, rk_ref, bd_ref = shared
    y_ref, bon_ref = outs
    h_ref, cr_ref, ck_ref, cv_ref, cwa_ref, lhs_ref, n_ref, y0_ref, bonp_ref = scratch
    rws, gw = r_ref.shape
    L = WKV_CHUNK
    nch = rws // L

    lane_head = lax.broadcasted_iota(jnp.int32, (1, gw), 1) // HEAD_DIM

    def recurrence():
        h = h_ref[...]
        ys = [None] * nch
        for c in (range(nch) if direction == 0 else range(nch - 1, -1, -1)):
            hb = h.astype(BF16)
            hbd = jnp.concatenate([jnp.where(lane_head == i, hb, 0.0)
                                   for i in range(gw // HEAD_DIM)], axis=0)
            both = jnp.dot(lhs_ref[c], hbd, preferred_element_type=F32)
            ys[c] = both[:L] + y0_ref[c]
            h = both[L:] + n_ref[c]
            yield
        h_ref[...] = jnp.where(chain_start, 0.0, h)
        y_ref[...] = jnp.concatenate(ys, axis=0).astype(y_ref.dtype)

    chain = recurrence()
    bon_ref[...] = bonp_ref[...].astype(bon_ref.dtype)
    if not with_local:
        return chain, []

    streams = ((r_ref, cr_ref), (k_ref, ck_ref), (v_ref, cv_ref), (wa_ref, cwa_ref))
    edge = [jnp.where(chain_start, 0.0, c_ref[...]) for _, c_ref in streams]
    far = rws - 1 if direction == 0 else 0
    for x_ref, c_ref in streams:
        c_ref[...] = x_ref[far:far + 1, :]

    ncg = min(WKV_GROUP, nch)
    groups = [_wkv_local_stages(direction, g0, ncg, edge, r_ref, k_ref, v_ref, wa_ref, mur_ref,
                                muk_ref, muv_ref, muwa_ref, w0_ref, a0_ref, kk_ref, ka_ref, rk_ref,
                                lora_ref, tri_ref, bd_ref, lhs_ref, n_ref, y0_ref, bonp_ref, chain)
              for g0 in range(0, nch, ncg)]
    return chain, groups


WKV_GROUP = 8
WKV_SKEW = 0


def _wkv_local_stages(direction, c0, ncg, edge, r_ref, k_ref, v_ref, wa_ref, mur_ref, muk_ref,
                      muv_ref, muwa_ref, w0_ref, a0_ref, kk_ref, ka_ref, rk_ref, lora_ref, tri_ref,
                      bd_ref, lhs_ref, n_ref, y0_ref, bonp_ref, chain):
    L = WKV_CHUNK
    nch = ncg
    rws, gw = r_ref.shape
    nheads = gw // HEAD_DIM
    rg = ncg * L
    row0 = c0 * L
    rows = lax.broadcasted_iota(jnp.int32, (rg, 1), 0)

    def token_shift(x_ref, edge_row, mu):
        x = x_ref[row0:row0 + rg, :]
        if direction == 0:
            nb = edge_row if row0 == 0 else x_ref[row0 - 1:row0, :]
            sh = jnp.where(rows == 0, nb, pltpu.roll(x, 1, 0))
        else:
            nb = edge_row if row0 + rg == rws else x_ref[row0 + rg:row0 + rg + 1, :]
            sh = jnp.where(rows == rg - 1, nb, pltpu.roll(x, rg - 1, 0))
        return x + mu * (sh - x)

    r = token_shift(r_ref, edge[0], mur_ref[...])
    k = token_shift(k_ref, edge[1], muk_ref[...])
    v = token_shift(v_ref, edge[2], muv_ref[...])
    wa = token_shift(wa_ref, edge[3], muwa_ref[...])
    yield

    lane_wa = lax.broadcasted_iota(jnp.int32, (1, wa.shape[1]), 1)
    wa = jnp.where(lane_wa < LORA, jnp.tanh(wa), wa)
    za = _dot(wa, lora_ref[...])
    zw = w0_ref[...] + za[:, :gw]
    a = _sigmoid(a0_ref[...] + za[:, gw:])
    lw = -math.exp(-0.5) * _sigmoid(zw)

    bd = bd_ref[...]
    kk = k * kk_ref[...]
    kk = kk * lax.rsqrt(jnp.maximum(_segsum(kk * kk, bd, 1), 1e-12))
    k2 = k * (1.0 + (a - 1.0) * ka_ref[...])
    bonp_ref[row0:row0 + rg, :] = _segsum(r * k2 * rk_ref[...], bd, 1) * v
    yield

    tri2 = jnp.broadcast_to(jnp.concatenate([tri_ref[...]] * 2, axis=1), (nch, L, 2 * L))
    parts = jnp.concatenate([p.reshape(nch, L, gw) for p in _split(lw, 2)], axis=1)
    cs = lax.dot_general(tri2, parts, _NN, preferred_element_type=F32).reshape(rg, gw)
    yield
    e_inc = jnp.exp(cs)
    e_inv = jnp.exp(-cs)
    e_exc = jnp.exp(cs - lw)
    to3 = lambda x: x.reshape(nch, L, gw)
    rt = to3(r * e_inc)
    kt = to3(k2 * e_inv)
    at = to3(-kk * e_exc)
    bt = to3(kk * a * e_inv)
    v3 = to3(v)
    last = L - 1 if direction == 0 else 0
    e_end = to3(e_inc)[:, last:last + 1, :]
    kh = kt * e_end
    bh = bt * e_end

    lane = lax.broadcasted_iota(jnp.int32, (1, 1, gw), 2)
    head_masks = [(lane // HEAD_DIM) == h for h in range(nheads)]

    def blockdiag(x3):
        return jnp.concatenate([jnp.where(m, x3, 0.0) for m in head_masks], axis=1)

    def hmm(lp, *xps):
        rhs = [blockdiag(xp.astype(BF16)) for xp in xps]
        rhs = rhs[0] if len(rhs) == 1 else jnp.concatenate(rhs, axis=2)
        return _bmm(lp, rhs, _NN)

    x_ar = jnp.concatenate([at, rt], axis=1)
    g = _bmm(x_ar, jnp.concatenate([blockdiag(bt.astype(BF16)), blockdiag(kt.astype(BF16))], axis=1),
             _NT)
    g_b, g_k = g[:, :, :gw], g[:, :, gw:]
    yield

    t_idx = lax.broadcasted_iota(jnp.int32, (1, L, gw), 1)
    s_idx = lax.broadcasted_iota(jnp.int32, (1, L, gw), 2) % L
    if direction == 0:
        strict, incl = s_idx < t_idx, s_idx <= t_idx
    else:
        strict, incl = s_idx > t_idx, s_idx >= t_idx
    a_ab = jnp.where(strict, g_b[:, :L], 0.0)
    a_rb = jnp.where(incl, g_b[:, L:], 0.0)
    a_ak = jnp.where(strict, g_k[:, :L], 0.0)
    a_rk = jnp.where(incl, g_k[:, L:], 0.0)

    tinv = jnp.where(s_idx == t_idx, 1.0, 0.0) + a_ab
    apow = hmm(a_ab, a_ab)
    nsq = int(math.log2(L)) - 1
    for i in range(nsq):
        yield
        if i + 1 < nsq:
            both = hmm(apow, apow, tinv)
            apow, tinv = both[:, :, :gw], tinv + both[:, :, gw:]
        else:
            tinv = tinv + hmm(apow, tinv)

    akv = _bmm(jnp.concatenate([a_ak, a_rk], axis=1), blockdiag(v3.astype(BF16)), _NN)
    wu = hmm(tinv, at, akv[:, :L])
    wt, u0 = wu[:, :, :gw], wu[:, :, gw:]
    yield
    ry = hmm(a_rb, wt, u0)
    rh = rt + ry[:, :, :gw]
    y0 = ry[:, :, gw:] + akv[:, L:]

    bk_t = jnp.swapaxes(jnp.concatenate([bh, kh], axis=1), 1, 2)
    bk_pack = jnp.concatenate([bk_t[:, i * HEAD_DIM:(i + 1) * HEAD_DIM] for i in range(nheads)],
                              axis=2).astype(BF16)
    rhs = jnp.concatenate([wu, jnp.concatenate([jnp.zeros_like(v3), v3], axis=2)],
                          axis=1).astype(BF16)
    lane2 = lax.broadcasted_iota(jnp.int32, (1, 1, 2 * gw), 2) % gw // HEAD_DIM
    rhs_heads = jnp.concatenate([jnp.where(lane2 == i, rhs, 0.0) for i in range(nheads)], axis=1)
    mn = _bmm(bk_pack, rhs_heads, _NN)
    ri = lax.broadcasted_iota(jnp.int32, (1, HEAD_DIM, gw), 1)
    ci = lax.broadcasted_iota(jnp.int32, (1, HEAD_DIM, gw), 2) % HEAD_DIM
    m_mat = mn[:, :, :gw] + jnp.where(ri == ci, e_end, 0.0)
    yield
    for _ in chain:
        pass
    lhs_ref[c0:c0 + ncg] = jnp.concatenate([rh, m_mat], axis=1).astype(BF16)
    n_ref[c0:c0 + ncg] = mn[:, :, gw:]
    y0_ref[c0:c0 + ncg] = y0


def _wkv(p3, col0, mu, w0, a0, lora_w, k_k, k_a, r_k, gw, rws):
    bsz, seq, _ = p3.shape
    c = w0.shape[-1]
    ng = c // gw
    nstep = seq // rws
    L = WKV_CHUNK
    nch = rws // L
    wa_w = 2 * LORA
    t = np.arange(L)
    hd = np.arange(gw) // HEAD_DIM
    bd = jnp.asarray(hd[:, None] == hd[None, :], BF16)
    row = lambda x: x.reshape(1, -1)
    total = bsz * ng * nstep
    rd = lambda i: jnp.minimum(i, total - 1)
    wr = lambda i: jnp.maximum(i - 1, 0)
    batch_of = lambda i: i // (ng * nstep)
    group_of = lambda i: (i // nstep) % ng
    vec = pl.BlockSpec((1, gw), lambda i: (0, group_of(rd(i))))
    full2 = lambda shape: pl.BlockSpec(shape, lambda i: (0, 0))

    in_specs, args, out_specs, scratch = [], [], [], []
    for direction in range(2):
        blk = (lambda i: i % nstep) if direction == 0 else (lambda i: nstep - 1 - i % nstep)
        stream = lambda off, blk=blk: pl.BlockSpec(
            (None, rws, gw),
            lambda i: (batch_of(rd(i)), blk(rd(i)), off // gw + group_of(rd(i))))
        tri = (t[None, :] <= t[:, None]) if direction == 0 else (t[None, :] >= t[:, None])
        mu_d = mu[direction]
        in_specs += [
            stream(col0), stream(col0 + c), stream(col0 + 2 * c),
            pl.BlockSpec((None, rws, wa_w),
                         lambda i, blk=blk: (batch_of(rd(i)), blk(rd(i)), (col0 + 3 * c) // wa_w)),
            vec, vec, vec, full2((1, wa_w)), vec, vec,
            pl.BlockSpec((None, wa_w, 2 * gw), lambda i: (group_of(rd(i)), 0, 0)),
            full2((L, L)),
        ]
        args += [p3, p3, p3, p3, row(mu_d[:c]), row(mu_d[c:2 * c]), row(mu_d[2 * c:3 * c]),
                 row(mu_d[3 * c:]), row(w0[direction]), row(a0[direction]), lora_w[direction],
                 jnp.asarray(tri, BF16)]
        out_specs += [pl.BlockSpec(
            (None, rws, gw),
            lambda i, blk=blk: (batch_of(wr(i)), blk(wr(i)), group_of(wr(i))))] * 2
        scratch += [pltpu.VMEM((HEAD_DIM, gw), F32), pltpu.VMEM((1, gw), F32), pltpu.VMEM((1, gw), F32),
                    pltpu.VMEM((1, gw), F32), pltpu.VMEM((1, wa_w), F32),
                    pltpu.VMEM((nch, L + HEAD_DIM, gw), BF16), pltpu.VMEM((nch, HEAD_DIM, gw), F32),
                    pltpu.VMEM((nch, L, gw), F32), pltpu.VMEM((rws, gw), F32)]
    assert len(in_specs) == 2 * WKV_DIR_INPUTS and len(scratch) == 2 * WKV_DIR_SCRATCH
    in_specs += [vec, vec, vec, full2((gw, gw))]
    args += [row(k_k), row(k_a), row(r_k), bd]
    return pl.pallas_call(
        functools.partial(_wkv_kernel, nstep),
        grid=(total + 1,),
        in_specs=in_specs,
        out_specs=out_specs,
        out_shape=[jax.ShapeDtypeStruct((bsz, seq, c), BF16)] * 4,
        scratch_shapes=scratch,
        compiler_params=_cparams(("arbitrary",)),
        name="wkv",
    )(*args)


def _lora_weights(w2, a2, gw):
    nd, lo, c = w2.shape
    ng = c // gw
    w2g = w2.reshape(nd, lo, ng, gw).transpose(0, 2, 1, 3)
    a2g = a2.reshape(nd, lo, ng, gw).transpose(0, 2, 1, 3)
    z = jnp.zeros_like(w2g)
    top = jnp.concatenate([w2g, z], axis=3)
    bot = jnp.concatenate([z, a2g], axis=3)
    return jnp.concatenate([top, bot], axis=2).astype(BF16)


FNET_N2 = 128


FNET_PITCH_PAD = 8
FNET_UNROLL = 8


def _cos_sin(idx, period):
    ang = 2.0 * np.pi * (idx % period) / period
    return np.cos(ang), np.sin(ang)


def _fnet_weight_kernel(scale, c_ref, s_ref, w_ref, o_ref):
    for g in range(w_ref.shape[0]):
        wc = _dot3(c_ref[...], w_ref[g])
        ws = _dot3(s_ref[...], w_ref[g])
        o_ref[g] = jnp.concatenate([wc, -ws], axis=1) * scale


def _fnet_kernel(n1, n2, f_ref, w_ref, m1_ref, m2_ref, o_ref, gr_ref, gi_ref, zr_ref, zi_ref):
    gd = f_ref.shape[1]
    pitch = n2 + FNET_PITCH_PAD
    w = w_ref[...].astype(BF16)

    def chan_body(s1, carry):
        src = pl.ds(pl.multiple_of(s1 * n2, n2), n2)
        dst = pl.ds(pl.multiple_of(s1 * pitch, V7X_SUBLANES), n2)
        z = jnp.dot(f_ref[src, :].astype(BF16), w, preferred_element_type=F32)
        gr_ref[dst, :] = z[:, :gd]
        gi_ref[dst, :] = z[:, gd:]
        return carry

    lax.fori_loop(0, n1, chan_body, 0, unroll=min(FNET_UNROLL, n1))

    m1 = m1_ref[...].astype(BF16)

    def stage1_body(s2, carry):
        idx = pl.ds(s2, n1, stride=pitch)
        x = jnp.concatenate([gr_ref[idx, :], gi_ref[idx, :]], axis=0)
        z = jnp.dot(m1, x.astype(BF16), preferred_element_type=F32)
        zr_ref[idx, :] = z[:n1]
        zi_ref[idx, :] = z[n1:]
        return carry

    lax.fori_loop(0, n2, stage1_body, 0, unroll=2 * FNET_UNROLL)

    def stage2_body(s1, carry):
        src = pl.ds(pl.multiple_of(s1 * pitch, V7X_SUBLANES), n2)
        zz = jnp.concatenate([zr_ref[src, :], zi_ref[src, :]], axis=0)
        gr_ref[src, :] = jnp.dot(m2_ref[s1].astype(BF16), zz.astype(BF16),
                                 preferred_element_type=F32)
        return carry

    lax.fori_loop(0, n1, stage2_body, 0, unroll=min(FNET_UNROLL, n1))

    def reorder_body(s2, carry):
        dst = pl.ds(pl.multiple_of(s2 * n1, V7X_SUBLANES), n1)
        o_ref[dst, :] = gr_ref[pl.ds(s2, n1, stride=pitch), :]
        return carry

    lax.fori_loop(0, n2, reorder_body, 0, unroll=2 * FNET_UNROLL)


def _fnet(p3, col0, w_f):
    bsz, seq, _ = p3.shape
    ng, gd, _ = w_f.shape
    n2 = FNET_N2
    n1 = seq // n2
    scale = 1.0 / math.sqrt(seq * gd)
    ar = np.arange
    cd, sd = _cos_sin(ar(gd)[:, None] * ar(gd)[None, :], gd)
    c1, s1 = _cos_sin(ar(n1)[:, None] * ar(n1)[None, :], n1)
    m1 = np.block([[c1, s1], [-s1, c1]])
    c2, s2 = _cos_sin(ar(n2)[None, None, :] * (ar(n1)[:, None, None] + n1 * ar(n2)[None, :, None]), seq)
    m1 = jnp.asarray(m1, F32)
    m2 = jnp.asarray(np.concatenate([c2, s2], axis=2), F32)

    wcat = pl.pallas_call(
        functools.partial(_fnet_weight_kernel, scale),
        out_shape=jax.ShapeDtypeStruct((ng, gd, 2 * gd), F32),
        name="fnet_weights",
    )(jnp.asarray(cd, F32), jnp.asarray(sd, F32), w_f)

    pitch_rows = n1 * (n2 + FNET_PITCH_PAD)
    const2 = lambda a: pl.BlockSpec(a.shape, lambda b, g: (0, 0))
    const3 = lambda a: pl.BlockSpec(a.shape, lambda b, g: (0, 0, 0), pipeline_mode=pl.Buffered(1))
    return pl.pallas_call(
        functools.partial(_fnet_kernel, n1, n2),
        grid=(bsz, ng),
        in_specs=[pl.BlockSpec((None, seq, gd), lambda b, g: (b, 0, col0 // gd + g)),
                  pl.BlockSpec((None, gd, 2 * gd), lambda b, g: (g, 0, 0)),
                  const2(m1), const3(m2)],
        out_specs=pl.BlockSpec((None, seq, gd), lambda b, g: (b, 0, g)),
        out_shape=jax.ShapeDtypeStruct((bsz, seq, ng * gd), F32),
        scratch_shapes=[pltpu.VMEM((pitch_rows, gd), F32)] * 4,
        compiler_params=_cparams(("parallel", "parallel")),
        name="fnet",
    )(p3, wcat, m1, m2)


def _odd_out_kernel(y0_ref, y1_ref, b0_ref, b1_ref, zc_ref, fn_ref, zd_ref, h_ref, lng_ref,
                    lnb_ref, bd_ref, wo_ref, fg_ref, o_ref):
    c = y0_ref.shape[1]
    bd = bd_ref[...]
    inv_n = 1.0 / HEAD_DIM
    ysum = y0_ref[...].astype(F32) + y1_ref[...].astype(F32)
    mean = _segsum(ysum, bd, 2) * inv_n
    cen = ysum - mean
    var = _segsum(cen * cen, bd, 1) * inv_n
    gn = cen * lax.rsqrt(var + GN_EPS) * lng_ref[...] + lnb_ref[...]
    yc = (gn + b0_ref[...].astype(F32) + b1_ref[...].astype(F32)) * _silu(zc_ref[...])
    yd = fn_ref[...] * _silu(zd_ref[...])
    out = h_ref[...]
    out = out + jnp.dot(yc.astype(BF16), wo_ref[0:c, :], preferred_element_type=F32)
    out = out + jnp.dot(yd.astype(BF16), wo_ref[c:, :], preferred_element_type=F32)
    ms = jnp.mean(out * out, axis=-1, keepdims=True)
    o_ref[...] = out * lax.rsqrt(ms + RMS_EPS) * fg_ref[...]


def _odd_out(y0, y1, b0, b1, p2d, zc_blk, fn, zd_blk, h2d, lnx_g, lnx_b, w_out_bf16, final_g, tm):
    t, d = h2d.shape
    c = y0.shape[1]
    cf = fn.shape[1]
    hd = np.arange(2 * HEAD_DIM) // HEAD_DIM
    bd = jnp.asarray(hd[:, None] == hd[None, :], BF16)
    rowblk = lambda w: pl.BlockSpec((tm, w), lambda i: (i, 0))
    full = lambda shape: pl.BlockSpec(shape, lambda i: (0, 0))
    return pl.pallas_call(
        _odd_out_kernel,
        grid=(t // tm,),
        in_specs=[rowblk(c), rowblk(c), rowblk(c), rowblk(c),
                  pl.BlockSpec((tm, c), lambda i: (i, zc_blk)),
                  rowblk(cf),
                  pl.BlockSpec((tm, cf), lambda i: (i, zd_blk)),
                  rowblk(d), full((1, c)), full((1, c)), full(bd.shape),
                  full(w_out_bf16.shape), full((1, d))],
        out_specs=rowblk(d),
        out_shape=jax.ShapeDtypeStruct((t, d), F32),
        compiler_params=_cparams(("parallel",)),
        name="odd_out",
    )(y0, y1, b0, b1, p2d, fn, p2d, h2d, lnx_g.reshape(1, c), lnx_b.reshape(1, c), bd,
      w_out_bf16, final_g.reshape(1, d))


WKV_LANES = 128
WKV_ROWS = 512
ODD_PROJ_ROWS = 512
ODD_OUT_ROWS = 512


def _odd_layer(h2d, bsz, seq, norm_g, w_in, mu, w0, w2, a0, a2, k_k, k_a, r_k, lnx_g, lnx_b,
               fnet_w, w_out, final_g):
    t, d = h2d.shape
    c = w0.shape[-1]
    rs = 3 * c + 2 * LORA
    cf = fnet_w.shape[0] * fnet_w.shape[1]
    w_perm = jnp.concatenate([w_in[:, rs:], w_in[:, :rs]], axis=1).astype(BF16)
    p2 = _norm_proj(h2d, norm_g, w_perm, tm=min(ODD_PROJ_ROWS, t), tn=w_perm.shape[1], out_dtype=F32)
    col0 = c + 2 * cf
    p3 = p2.reshape(bsz, seq, -1)
    lora = _lora_weights(w2, a2, WKV_LANES)
    rws = min(WKV_ROWS, seq)
    y0, b0, y1, b1 = _wkv(p3, col0, mu, w0, a0, lora, k_k, k_a, r_k.reshape(-1), WKV_LANES, rws)
    fn = _fnet(p3, c, fnet_w).reshape(t, cf)
    flat = lambda a: a.reshape(t, c)
    return _odd_out(flat(y0), flat(y1), flat(b0), flat(b1), p2, 0, fn, c // cf + 1, h2d,
                    lnx_g, lnx_b, w_out.astype(BF16), final_g, tm=min(ODD_OUT_ROWS, t))


def kernel(x, e_norm_g, e_w_in, e_conv_w, e_sgu_ln_g, e_sgu_ln_b, e_sgu_w, e_sgu_b, e_w_out,
           o_norm_g, o_w_in, o_mu, o_w0, o_w2, o_a0, o_a2, o_k_k, o_k_a, o_r_k, o_lnx_g, o_lnx_b,
           o_fnet_w, o_w_out, final_norm_g):
    bsz, seq, d = x.shape
    assert e_norm_g.shape[0] == 1 and o_norm_g.shape[0] == 1, "two-layer trunk: one even, one odd layer"
    h = x.reshape(bsz * seq, d)
    h = _even_layer(h, seq, e_norm_g[0], e_w_in[0], e_conv_w[0], e_sgu_ln_g[0], e_sgu_ln_b[0],
                    e_sgu_w[0], e_sgu_b[0], e_w_out[0])
    out = _odd_layer(h, bsz, seq, o_norm_g[0], o_w_in[0], o_mu[0], o_w0[0], o_w2[0], o_a0[0],
                     o_a2[0], o_k_k[0], o_k_a[0], o_r_k[0], o_lnx_g[0], o_lnx_b[0], o_fnet_w[0],
                     o_w_out[0], final_norm_g)
    return out.reshape(bsz, seq, d)
```

```python
import functools
import math

import numpy as np
import jax
import jax.numpy as jnp
from jax import lax
from jax.experimental import pallas as pl
from jax.experimental.pallas import tpu as pltpu

F32 = jnp.float32
BF16 = jnp.bfloat16

RMS_EPS = 1e-6
SGU_LN_EPS = 1e-5
GN_EPS = 64e-5
SGU_CHUNK = 128
SGU_GROUPS = 8
HEAD_DIM = 64
LORA = 64

V7X_SUBLANES = 8
V7X_VMEM_BYTES = 64 * 1024 * 1024
VMEM_LIMIT = V7X_VMEM_BYTES - 8 * 1024 * 1024


def _cparams(sem):
    return pltpu.CompilerParams(dimension_semantics=sem, vmem_limit_bytes=VMEM_LIMIT)


def _silu(z):
    return z * (1.0 / (1.0 + jnp.exp(-z)))


def _sigmoid(z):
    return 1.0 / (1.0 + jnp.exp(-z))


def _dot(a, b):
    return jnp.dot(a.astype(BF16), b.astype(BF16), preferred_element_type=F32)


def _split(a, n):
    parts = []
    rem = a
    for _ in range(n):
        p = rem.astype(BF16)
        parts.append(p)
        rem = rem - p.astype(F32)
    return parts


def _dot_exact_rhs(a, b_bf16, n=3):
    acc = None
    for p in _split(a, n):
        t = jnp.dot(p, b_bf16, preferred_element_type=F32)
        acc = t if acc is None else acc + t
    return acc


def _proj_kernel(x_ref, g_ref, w_ref, o_ref, hn_ref):
    @pl.when(pl.program_id(1) == 0)
    def _():
        x = x_ref[...]
        ms = jnp.mean(x * x, axis=-1, keepdims=True)
        hn_ref[...] = (x * lax.rsqrt(ms + RMS_EPS) * g_ref[...]).astype(BF16)

    o_ref[...] = jnp.dot(hn_ref[...], w_ref[...], preferred_element_type=F32).astype(o_ref.dtype)


def _norm_proj(x2d, g, w_bf16, tm, tn, out_dtype):
    t, d = x2d.shape
    n = w_bf16.shape[1]
    assert t % tm == 0 and n % tn == 0
    w_mode = dict(pipeline_mode=pl.Buffered(1)) if n == tn else {}
    return pl.pallas_call(
        _proj_kernel,
        grid=(t // tm, n // tn),
        in_specs=[
            pl.BlockSpec((tm, d), lambda i, j: (i, 0)),
            pl.BlockSpec((1, d), lambda i, j: (0, 0)),
            pl.BlockSpec((d, tn), lambda i, j: (0, j), **w_mode),
        ],
        out_specs=pl.BlockSpec((tm, tn), lambda i, j: (i, j)),
        out_shape=jax.ShapeDtypeStruct((t, n), out_dtype),
        scratch_shapes=[pltpu.VMEM((tm, d), BF16)],
        compiler_params=_cparams(("parallel", "arbitrary")),
        name="norm_proj",
    )(x2d, g.reshape(1, d), w_bf16)


def _even_kernel(seq, h_ref, hp_ref, hn_ref, ng_ref, wi_ref, cw_ref, lng_ref, lnb_ref, sw_ref,
                 sb_ref, wo_ref, o_ref):
    tm, d = h_ref.shape
    hr = hp_ref.shape[0]
    row0 = pl.program_id(0) * tm
    at_seq_start = (row0 % seq) == 0
    at_seq_end = ((row0 + tm) % seq) == 0

    h = h_ref[...]
    hx = jnp.concatenate([hp_ref[...], h, hn_ref[...]], axis=0)
    ms = jnp.mean(hx * hx, axis=-1, keepdims=True)
    hx = (hx * lax.rsqrt(ms + RMS_EPS) * ng_ref[...]).astype(BF16)
    hn = hx[hr:hr + tm]
    proj = lambda lhs, j: jnp.dot(lhs, wi_ref[:, j * d:(j + 1) * d], preferred_element_type=F32)

    xce = proj(hx, 0) * proj(hx, 2)
    xc = xce[hr:hr + tm]
    halo_prev = xce[hr - 1:hr]
    halo_next = xce[hr + tm:hr + tm + 1]
    ba, za = proj(hn, 1), proj(hn, 3)
    halo_prev = jnp.where(at_seq_start, 0.0, halo_prev)
    halo_next = jnp.where(at_seq_end, 0.0, halo_next)
    rows = lax.broadcasted_iota(jnp.int32, (tm, 1), 0)
    prev = jnp.where(rows == 0, halo_prev, pltpu.roll(xc, 1, 0))
    nxt = jnp.where(rows == tm - 1, halo_next, pltpu.roll(xc, tm - 1, 0))
    conv = cw_ref[0:1, :] * prev + cw_ref[1:2, :] * xc + cw_ref[2:3, :] * nxt
    ya = ba * conv * _silu(za)

    vb = proj(hn, 5)
    mu = jnp.mean(vb, axis=-1, keepdims=True)
    cen = vb - mu
    var = jnp.mean(cen * cen, axis=-1, keepdims=True)
    vn = (cen * lax.rsqrt(var + SGU_LN_EPS) * lng_ref[...] + lnb_ref[...]).astype(BF16)
    gw = d // SGU_GROUPS
    chunk_rows = []
    for n in range(tm // SGU_CHUNK):
        r0 = n * SGU_CHUNK
        cols = []
        for g in range(SGU_GROUPS):
            cols.append(jnp.dot(sw_ref[g], vn[r0:r0 + SGU_CHUNK, g * gw:(g + 1) * gw],
                                preferred_element_type=F32))
        chunk_rows.append(jnp.concatenate(cols, axis=1) + sb_ref[...])
    mixed = jnp.concatenate(chunk_rows, axis=0)
    yb = proj(hn, 4) * mixed * _silu(proj(hn, 6))

    out = h
    out = out + jnp.dot(ya.astype(BF16), wo_ref[0:d, :], preferred_element_type=F32)
    out = out + jnp.dot(yb.astype(BF16), wo_ref[d:2 * d, :], preferred_element_type=F32)
    o_ref[...] = out


EVEN_ROWS = 512


def _even_layer(h2d, seq, norm_g, w_in, conv_w, ln_g, ln_b, sgu_w, sgu_b, w_out):
    t, d = h2d.shape
    tm = min(EVEN_ROWS, t)
    hr = V7X_SUBLANES
    nblk = t // hr
    rpt = tm // hr
    bias_full = jnp.repeat(sgu_b.T, d // SGU_GROUPS, axis=1)
    sgu_w_bf16 = sgu_w.astype(BF16)
    w_in_bf16 = w_in.astype(BF16)
    w_out_bf16 = w_out.astype(BF16)
    const = lambda a: pl.BlockSpec(a.shape, lambda i: (0,) * a.ndim, pipeline_mode=pl.Buffered(1))
    row = lambda a: a.reshape(1, d)
    consts = (row(norm_g), w_in_bf16, conv_w, row(ln_g), row(ln_b), sgu_w_bf16, bias_full, w_out_bf16)
    return pl.pallas_call(
        functools.partial(_even_kernel, seq),
        grid=(t // tm,),
        in_specs=[
            pl.BlockSpec((tm, d), lambda i: (i, 0)),
            pl.BlockSpec((hr, d), lambda i: (jnp.maximum(i * rpt - 1, 0), 0)),
            pl.BlockSpec((hr, d), lambda i: (jnp.minimum((i + 1) * rpt, nblk - 1), 0)),
        ] + [const(a) for a in consts],
        out_specs=pl.BlockSpec((tm, d), lambda i: (i, 0)),
        out_shape=jax.ShapeDtypeStruct((t, d), F32),
        compiler_params=_cparams(("parallel",)),
        name="even_layer",
    )(h2d, h2d, h2d, *consts)


def _dot3(a, b):
    ah, al = _split(a, 2)
    bh, bl = _split(b, 2)
    d = lambda x, y: jnp.dot(x, y, preferred_element_type=F32)
    return d(ah, bh) + (d(ah, bl) + d(al, bh))


_NN = (((2,), (1,)), ((0,), (0,)))
_NT = (((2,), (2,)), ((0,), (0,)))


def _bmm(a, b, dims):
    return lax.dot_general(a.astype(BF16), b.astype(BF16), dims, preferred_element_type=F32)


def _segsum(x, bd_bf16, n=3):
    w = bd_bf16.shape[0]
    cols = [_dot_exact_rhs(x[:, j:j + w], bd_bf16, n) for j in range(0, x.shape[1], w)]
    return cols[0] if len(cols) == 1 else jnp.concatenate(cols, axis=1)


WKV_CHUNK = 64


WKV_DIR_SCRATCH = 9
WKV_VEC_PER_DIR = 5
WKV_VEC_SHARED = 2 * WKV_VEC_PER_DIR
WKV_VEC_ROWS = 16


def _wkv_kernel(nstep, *refs):
    ns = WKV_DIR_SCRATCH
    streams = refs[0:4]
    vec_ref, muwa_ref, tri_ref, lora_ref, bd_ref = refs[4:9]
    outs = refs[9:13]
    scratch = refs[13:]
    gw = bd_ref.shape[0]
    vec = lambda r: vec_ref.at[pl.ds(r, 1)]
    shared = (vec(WKV_VEC_SHARED), vec(WKV_VEC_SHARED + 1), vec(WKV_VEC_SHARED + 2), bd_ref)

    def dir_inputs(d):
        rkv_ref, wa_ref = streams[2 * d:2 * d + 2]
        r, k, v = (rkv_ref.at[:, pl.ds(j * gw, gw)] for j in range(3))
        base = d * WKV_VEC_PER_DIR
        return (r, k, v, wa_ref, vec(base), vec(base + 1), vec(base + 2), muwa_ref.at[pl.ds(d, 1)],
                vec(base + 3), vec(base + 4), lora_ref.at[d], tri_ref.at[d])

    step = pl.program_id(0)
    last = pl.num_programs(0) - 1
    chain_start = (step % nstep) == 0

    @pl.when(step == 0)
    def _():
        for ref in scratch:
            ref[...] = jnp.zeros_like(ref)

    def body(with_local):
        per_dir = [_wkv_direction(d, dir_inputs(d), shared, outs[2 * d:2 * d + 2],
                                  scratch[d * ns:(d + 1) * ns], with_local, chain_start)
                   for d in range(2)]
        chains = [c for c, _ in per_dir]
        groups = [g for pair in zip(*[gs for _, gs in per_dir]) for g in pair]
        start = {id(g): i * WKV_SKEW for i, g in enumerate(groups)}
        slot = 0
        while groups:
            for gen in list(groups):
                if slot >= start[id(gen)] and next(gen, "done") == "done":
                    groups.remove(gen)
            for chain in chains:
                next(chain, None)
            slot += 1
        for chain in chains:
            for _ in chain:
                pass

    pl.when(step < last)(functools.partial(body, True))
    pl.when(step == last)(functools.partial(body, False))


def _wkv_direction(direction, ins, shared, outs, scratch, with_local, chain_start):
    (r_ref, k_ref, v_ref, wa_ref, mur_ref, muk_ref, muv_ref, muwa_ref, w0_ref, a0_ref, lora_ref,
     tri_ref) = ins
    kk_ref, ka_ref, rk_ref, bd_ref = shared
    y_ref, bon_ref = outs
    h_ref, cr_ref, ck_ref, cv_ref, cwa_ref, lhs_ref, n_ref, y0_ref, bonp_ref = scratch
    rws, gw = r_ref.shape
    L = WKV_CHUNK
    nch = rws // L

    lane_head = lax.broadcasted_iota(jnp.int32, (1, gw), 1) // HEAD_DIM

    def recurrence():
        h = h_ref[...]
        ys = [None] * nch
        for c in (range(nch) if direction == 0 else range(nch - 1, -1, -1)):
            hb = h.astype(BF16)
            hbd = jnp.concatenate([jnp.where(lane_head == i, hb, 0.0)
                                   for i in range(gw // HEAD_DIM)], axis=0)
            both = jnp.dot(lhs_ref[c], hbd, preferred_element_type=F32)
            ys[c] = both[:L] + y0_ref[c]
            h = both[L:] + n_ref[c]
            yield
        h_ref[...] = jnp.where(chain_start, 0.0, h)
        y_ref[...] = jnp.concatenate(ys, axis=0).astype(y_ref.dtype)

    chain = recurrence()
    bon_ref[...] = bonp_ref[...].astype(bon_ref.dtype)
    if not with_local:
        return chain, []

    streams = ((r_ref, cr_ref), (k_ref, ck_ref), (v_ref, cv_ref), (wa_ref, cwa_ref))
    edge = [jnp.where(chain_start, 0.0, c_ref[...]) for _, c_ref in streams]
    far = rws - 1 if direction == 0 else 0
    for x_ref, c_ref in streams:
        c_ref[...] = x_ref[far:far + 1, :]

    ncg = min(WKV_GROUP, nch)
    groups = [_wkv_local_stages(direction, g0, ncg, edge, r_ref, k_ref, v_ref, wa_ref, mur_ref,
                                muk_ref, muv_ref, muwa_ref, w0_ref, a0_ref, kk_ref, ka_ref, rk_ref,
                                lora_ref, tri_ref, bd_ref, lhs_ref, n_ref, y0_ref, bonp_ref, chain)
              for g0 in range(0, nch, ncg)]
    return chain, groups


WKV_GROUP = 8
WKV_SKEW = 0


def _wkv_local_stages(direction, c0, ncg, edge, r_ref, k_ref, v_ref, wa_ref, mur_ref, muk_ref,
                      muv_ref, muwa_ref, w0_ref, a0_ref, kk_ref, ka_ref, rk_ref, lora_ref, tri_ref,
                      bd_ref, lhs_ref, n_ref, y0_ref, bonp_ref, chain):
    L = WKV_CHUNK
    nch = ncg
    rws, gw = r_ref.shape
    nheads = gw // HEAD_DIM
    rg = ncg * L
    row0 = c0 * L
    rows = lax.broadcasted_iota(jnp.int32, (rg, 1), 0)

    def token_shift(x_ref, edge_row, mu):
        x = x_ref[row0:row0 + rg, :]
        if direction == 0:
            nb = edge_row if row0 == 0 else x_ref[row0 - 1:row0, :]
            sh = jnp.where(rows == 0, nb, pltpu.roll(x, 1, 0))
        else:
            nb = edge_row if row0 + rg == rws else x_ref[row0 + rg:row0 + rg + 1, :]
            sh = jnp.where(rows == rg - 1, nb, pltpu.roll(x, rg - 1, 0))
        return x + mu * (sh - x)

    r = token_shift(r_ref, edge[0], mur_ref[...])
    k = token_shift(k_ref, edge[1], muk_ref[...])
    v = token_shift(v_ref, edge[2], muv_ref[...])
    wa = token_shift(wa_ref, edge[3], muwa_ref[...])
    yield

    lane_wa = lax.broadcasted_iota(jnp.int32, (1, wa.shape[1]), 1)
    wa = jnp.where(lane_wa < LORA, jnp.tanh(wa), wa)
    za = _dot(wa, lora_ref[...])
    zw = w0_ref[...] + za[:, :gw]
    a = _sigmoid(a0_ref[...] + za[:, gw:])
    lw = -math.exp(-0.5) * _sigmoid(zw)

    bd = bd_ref[...]
    kk = k * kk_ref[...]
    kk = kk * lax.rsqrt(jnp.maximum(_segsum(kk * kk, bd, 1), 1e-12))
    k2 = k * (1.0 + (a - 1.0) * ka_ref[...])
    bonp_ref[row0:row0 + rg, :] = _segsum(r * k2 * rk_ref[...], bd, 1) * v
    yield

    tri2 = jnp.broadcast_to(jnp.concatenate([tri_ref[...]] * 2, axis=1), (nch, L, 2 * L))
    parts = jnp.concatenate([p.reshape(nch, L, gw) for p in _split(lw, 2)], axis=1)
    cs = lax.dot_general(tri2, parts, _NN, preferred_element_type=F32).reshape(rg, gw)
    yield
    e_inc = jnp.exp(cs)
    e_inv = jnp.exp(-cs)
    e_exc = jnp.exp(cs - lw)
    to3 = lambda x: x.reshape(nch, L, gw)
    rt = to3(r * e_inc)
    kt = to3(k2 * e_inv)
    at = to3(-kk * e_exc)
    bt = to3(kk * a * e_inv)
    v3 = to3(v)
    last = L - 1 if direction == 0 else 0
    e_end = to3(e_inc)[:, last:last + 1, :]
    kh = kt * e_end
    bh = bt * e_end

    lane = lax.broadcasted_iota(jnp.int32, (1, 1, gw), 2)
    head_masks = [(lane // HEAD_DIM) == h for h in range(nheads)]

    def blockdiag(x3):
        return jnp.concatenate([jnp.where(m, x3, 0.0) for m in head_masks], axis=1)

    def hmm(lp, *xps):
        rhs = [blockdiag(xp.astype(BF16)) for xp in xps]
        rhs = rhs[0] if len(rhs) == 1 else jnp.concatenate(rhs, axis=2)
        return _bmm(lp, rhs, _NN)

    x_ar = jnp.concatenate([at, rt], axis=1)
    g = _bmm(x_ar, jnp.concatenate([blockdiag(bt.astype(BF16)), blockdiag(kt.astype(BF16))], axis=1),
             _NT)
    g_b, g_k = g[:, :, :gw], g[:, :, gw:]
    yield

    t_idx = lax.broadcasted_iota(jnp.int32, (1, L, gw), 1)
    s_idx = lax.broadcasted_iota(jnp.int32, (1, L, gw), 2) % L
    if direction == 0:
        strict, incl = s_idx < t_idx, s_idx <= t_idx
    else:
        strict, incl = s_idx > t_idx, s_idx >= t_idx
    a_ab = jnp.where(strict, g_b[:, :L], 0.0)
    a_rb = jnp.where(incl, g_b[:, L:], 0.0)
    a_ak = jnp.where(strict, g_k[:, :L], 0.0)
    a_rk = jnp.where(incl, g_k[:, L:], 0.0)

    tinv = jnp.where(s_idx == t_idx, 1.0, 0.0) + a_ab
    apow = hmm(a_ab, a_ab)
    nsq = int(math.log2(L)) - 1
    for i in range(nsq):
        yield
        if i + 1 < nsq:
            both = hmm(apow, apow, tinv)
            apow, tinv = both[:, :, :gw], tinv + both[:, :, gw:]
        else:
            tinv = tinv + hmm(apow, tinv)

    akv = _bmm(jnp.concatenate([a_ak, a_rk], axis=1), blockdiag(v3.astype(BF16)), _NN)
    wu = hmm(tinv, at, akv[:, :L])
    wt, u0 = wu[:, :, :gw], wu[:, :, gw:]
    yield
    ry = hmm(a_rb, wt, u0)
    rh = rt + ry[:, :, :gw]
    y0 = ry[:, :, gw:] + akv[:, L:]

    bk_t = jnp.swapaxes(jnp.concatenate([bh, kh], axis=1), 1, 2)
    bk_pack = jnp.concatenate([bk_t[:, i * HEAD_DIM:(i + 1) * HEAD_DIM] for i in range(nheads)],
                              axis=2).astype(BF16)
    rhs = jnp.concatenate([wu, jnp.concatenate([jnp.zeros_like(v3), v3], axis=2)],
                          axis=1).astype(BF16)
    lane2 = lax.broadcasted_iota(jnp.int32, (1, 1, 2 * gw), 2) % gw // HEAD_DIM
    rhs_heads = jnp.concatenate([jnp.where(lane2 == i, rhs, 0.0) for i in range(nheads)], axis=1)
    mn = _bmm(bk_pack, rhs_heads, _NN)
    ri = lax.broadcasted_iota(jnp.int32, (1, HEAD_DIM, gw), 1)
    ci = lax.broadcasted_iota(jnp.int32, (1, HEAD_DIM, gw), 2) % HEAD_DIM
    m_mat = mn[:, :, :gw] + jnp.where(ri == ci, e_end, 0.0)
    yield
    for _ in chain:
        pass
    lhs_ref[c0:c0 + ncg] = jnp.concatenate([rh, m_mat], axis=1).astype(BF16)
    n_ref[c0:c0 + ncg] = mn[:, :, gw:]
    y0_ref[c0:c0 + ncg] = y0


def _wkv(p3, wa_col, mu, w0, a0, lora_w, k_k, k_a, r_k, gw, rws):
    bsz, seq, _ = p3.shape
    c = w0.shape[-1]
    ng = c // gw
    nstep = seq // rws
    L = WKV_CHUNK
    nch = rws // L
    wa_w = 2 * LORA
    t = np.arange(L)
    hd = np.arange(gw) // HEAD_DIM
    bd = jnp.asarray(hd[:, None] == hd[None, :], BF16)
    rows = [x for d in range(2) for x in (mu[d][:c], mu[d][c:2 * c], mu[d][2 * c:3 * c], w0[d], a0[d])]
    rows += [k_k, k_a, r_k]
    vecs = jnp.zeros((WKV_VEC_ROWS, c), F32).at[:len(rows)].set(jnp.stack(rows))
    muwa = jnp.zeros((V7X_SUBLANES, wa_w), F32).at[:2].set(jnp.stack([mu[0][3 * c:], mu[1][3 * c:]]))
    tri = jnp.asarray(np.stack([t[None, :] <= t[:, None], t[None, :] >= t[:, None]]), BF16)
    total = bsz * ng * nstep
    rd = lambda i: jnp.minimum(i, total - 1)
    wr = lambda i: jnp.maximum(i - 1, 0)
    batch_of = lambda i: i // (ng * nstep)
    group_of = lambda i: (i // nstep) % ng
    const = lambda a: pl.BlockSpec(a.shape, lambda i: (0,) * a.ndim)

    in_specs, args, out_specs, scratch = [], [], [], []
    for direction in range(2):
        blk = (lambda i: i % nstep) if direction == 0 else (lambda i: nstep - 1 - i % nstep)
        in_specs += [
            pl.BlockSpec((None, rws, 3 * gw),
                         lambda i, blk=blk: (batch_of(rd(i)), blk(rd(i)), group_of(rd(i)))),
            pl.BlockSpec((None, rws, wa_w),
                         lambda i, blk=blk: (batch_of(rd(i)), blk(rd(i)), wa_col // wa_w)),
        ]
        args += [p3, p3]
        out_specs += [pl.BlockSpec(
            (None, rws, gw),
            lambda i, blk=blk: (batch_of(wr(i)), blk(wr(i)), group_of(wr(i))))] * 2
        scratch += [pltpu.VMEM((HEAD_DIM, gw), F32), pltpu.VMEM((1, gw), F32), pltpu.VMEM((1, gw), F32),
                    pltpu.VMEM((1, gw), F32), pltpu.VMEM((1, wa_w), F32),
                    pltpu.VMEM((nch, L + HEAD_DIM, gw), BF16), pltpu.VMEM((nch, HEAD_DIM, gw), F32),
                    pltpu.VMEM((nch, L, gw), F32), pltpu.VMEM((rws, gw), F32)]
    assert len(scratch) == 2 * WKV_DIR_SCRATCH
    in_specs += [
        pl.BlockSpec((WKV_VEC_ROWS, gw), lambda i: (0, group_of(rd(i)))),
        const(muwa), const(tri),
        pl.BlockSpec((2, None, wa_w, 2 * gw), lambda i: (0, group_of(rd(i)), 0, 0)),
        const(bd),
    ]
    args += [vecs, muwa, tri, lora_w, bd]
    return pl.pallas_call(
        functools.partial(_wkv_kernel, nstep),
        grid=(total + 1,),
        in_specs=in_specs,
        out_specs=out_specs,
        out_shape=[jax.ShapeDtypeStruct((bsz, seq, c), BF16)] * 4,
        scratch_shapes=scratch,
        compiler_params=_cparams(("arbitrary",)),
        name="wkv",
    )(*args)


def _lora_weights(w2, a2, gw):
    nd, lo, c = w2.shape
    ng = c // gw
    w2g = w2.reshape(nd, lo, ng, gw).transpose(0, 2, 1, 3)
    a2g = a2.reshape(nd, lo, ng, gw).transpose(0, 2, 1, 3)
    z = jnp.zeros_like(w2g)
    top = jnp.concatenate([w2g, z], axis=3)
    bot = jnp.concatenate([z, a2g], axis=3)
    return jnp.concatenate([top, bot], axis=2).astype(BF16)


FNET_N2 = 128


FNET_PITCH_PAD = 8
FNET_UNROLL = 8


def _cos_sin(idx, period):
    ang = 2.0 * np.pi * (idx % period) / period
    return np.cos(ang), np.sin(ang)


def _fnet_weight_kernel(scale, c_ref, s_ref, w_ref, o_ref):
    for g in range(w_ref.shape[0]):
        wc = _dot3(c_ref[...], w_ref[g])
        ws = _dot3(s_ref[...], w_ref[g])
        o_ref[g] = jnp.concatenate([wc, -ws], axis=1) * scale


def _fnet_kernel(n1, n2, f_ref, w_ref, m1_ref, m2_ref, o_ref, gr_ref, gi_ref, zr_ref, zi_ref):
    gd = f_ref.shape[1]
    pitch = n2 + FNET_PITCH_PAD
    w = w_ref[...].astype(BF16)

    def chan_body(s1, carry):
        src = pl.ds(pl.multiple_of(s1 * n2, n2), n2)
        dst = pl.ds(pl.multiple_of(s1 * pitch, V7X_SUBLANES), n2)
        z = jnp.dot(f_ref[src, :].astype(BF16), w, preferred_element_type=F32)
        gr_ref[dst, :] = z[:, :gd]
        gi_ref[dst, :] = z[:, gd:]
        return carry

    lax.fori_loop(0, n1, chan_body, 0, unroll=min(FNET_UNROLL, n1))

    m1 = m1_ref[...].astype(BF16)

    def stage1_body(s2, carry):
        idx = pl.ds(s2, n1, stride=pitch)
        x = jnp.concatenate([gr_ref[idx, :], gi_ref[idx, :]], axis=0)
        z = jnp.dot(m1, x.astype(BF16), preferred_element_type=F32)
        zr_ref[idx, :] = z[:n1]
        zi_ref[idx, :] = z[n1:]
        return carry

    lax.fori_loop(0, n2, stage1_body, 0, unroll=2 * FNET_UNROLL)

    def stage2_body(s1, carry):
        src = pl.ds(pl.multiple_of(s1 * pitch, V7X_SUBLANES), n2)
        zz = jnp.concatenate([zr_ref[src, :], zi_ref[src, :]], axis=0)
        gr_ref[src, :] = jnp.dot(m2_ref[s1].astype(BF16), zz.astype(BF16),
                                 preferred_element_type=F32)
        return carry

    lax.fori_loop(0, n1, stage2_body, 0, unroll=min(FNET_UNROLL, n1))

    def reorder_body(s2, carry):
        dst = pl.ds(pl.multiple_of(s2 * n1, V7X_SUBLANES), n1)
        o_ref[dst, :] = gr_ref[pl.ds(s2, n1, stride=pitch), :]
        return carry

    lax.fori_loop(0, n2, reorder_body, 0, unroll=2 * FNET_UNROLL)


def _fnet(p3, col0, w_f):
    bsz, seq, _ = p3.shape
    ng, gd, _ = w_f.shape
    n2 = FNET_N2
    n1 = seq // n2
    scale = 1.0 / math.sqrt(seq * gd)
    ar = np.arange
    cd, sd = _cos_sin(ar(gd)[:, None] * ar(gd)[None, :], gd)
    c1, s1 = _cos_sin(ar(n1)[:, None] * ar(n1)[None, :], n1)
    m1 = np.block([[c1, s1], [-s1, c1]])
    c2, s2 = _cos_sin(ar(n2)[None, None, :] * (ar(n1)[:, None, None] + n1 * ar(n2)[None, :, None]), seq)
    m1 = jnp.asarray(m1, F32)
    m2 = jnp.asarray(np.concatenate([c2, s2], axis=2), F32)

    wcat = pl.pallas_call(
        functools.partial(_fnet_weight_kernel, scale),
        out_shape=jax.ShapeDtypeStruct((ng, gd, 2 * gd), F32),
        name="fnet_weights",
    )(jnp.asarray(cd, F32), jnp.asarray(sd, F32), w_f)

    pitch_rows = n1 * (n2 + FNET_PITCH_PAD)
    const2 = lambda a: pl.BlockSpec(a.shape, lambda b, g: (0, 0))
    const3 = lambda a: pl.BlockSpec(a.shape, lambda b, g: (0, 0, 0), pipeline_mode=pl.Buffered(1))
    return pl.pallas_call(
        functools.partial(_fnet_kernel, n1, n2),
        grid=(bsz, ng),
        in_specs=[pl.BlockSpec((None, seq, gd), lambda b, g: (b, 0, col0 // gd + g)),
                  pl.BlockSpec((None, gd, 2 * gd), lambda b, g: (g, 0, 0)),
                  const2(m1), const3(m2)],
        out_specs=pl.BlockSpec((None, seq, gd), lambda b, g: (b, 0, g)),
        out_shape=jax.ShapeDtypeStruct((bsz, seq, ng * gd), F32),
        scratch_shapes=[pltpu.VMEM((pitch_rows, gd), F32)] * 4,
        compiler_params=_cparams(("parallel", "parallel")),
        name="fnet",
    )(p3, wcat, m1, m2)


def _odd_out_kernel(y0_ref, y1_ref, b0_ref, b1_ref, zc_ref, fn_ref, zd_ref, h_ref, lng_ref,
                    lnb_ref, bd_ref, wo_ref, fg_ref, o_ref):
    c = y0_ref.shape[1]
    bd = bd_ref[...]
    inv_n = 1.0 / HEAD_DIM
    ysum = y0_ref[...].astype(F32) + y1_ref[...].astype(F32)
    mean = _segsum(ysum, bd, 2) * inv_n
    cen = ysum - mean
    var = _segsum(cen * cen, bd, 1) * inv_n
    gn = cen * lax.rsqrt(var + GN_EPS) * lng_ref[...] + lnb_ref[...]
    yc = (gn + b0_ref[...].astype(F32) + b1_ref[...].astype(F32)) * _silu(zc_ref[...])
    yd = fn_ref[...] * _silu(zd_ref[...])
    out = h_ref[...]
    out = out + jnp.dot(yc.astype(BF16), wo_ref[0:c, :], preferred_element_type=F32)
    out = out + jnp.dot(yd.astype(BF16), wo_ref[c:, :], preferred_element_type=F32)
    ms = jnp.mean(out * out, axis=-1, keepdims=True)
    o_ref[...] = out * lax.rsqrt(ms + RMS_EPS) * fg_ref[...]


def _odd_out(y0, y1, b0, b1, p2d, zc_blk, fn, zd_blk, h2d, lnx_g, lnx_b, w_out_bf16, final_g, tm):
    t, d = h2d.shape
    c = y0.shape[1]
    cf = fn.shape[1]
    hd = np.arange(2 * HEAD_DIM) // HEAD_DIM
    bd = jnp.asarray(hd[:, None] == hd[None, :], BF16)
    rowblk = lambda w: pl.BlockSpec((tm, w), lambda i: (i, 0))
    full = lambda shape: pl.BlockSpec(shape, lambda i: (0, 0))
    return pl.pallas_call(
        _odd_out_kernel,
        grid=(t // tm,),
        in_specs=[rowblk(c), rowblk(c), rowblk(c), rowblk(c),
                  pl.BlockSpec((tm, c), lambda i: (i, zc_blk)),
                  rowblk(cf),
                  pl.BlockSpec((tm, cf), lambda i: (i, zd_blk)),
                  rowblk(d), full((1, c)), full((1, c)), full(bd.shape),
                  full(w_out_bf16.shape), full((1, d))],
        out_specs=rowblk(d),
        out_shape=jax.ShapeDtypeStruct((t, d), F32),
        compiler_params=_cparams(("parallel",)),
        name="odd_out",
    )(y0, y1, b0, b1, p2d, fn, p2d, h2d, lnx_g.reshape(1, c), lnx_b.reshape(1, c), bd,
      w_out_bf16, final_g.reshape(1, d))


WKV_LANES = 128
WKV_ROWS = 512
ODD_PROJ_ROWS = 512
ODD_OUT_ROWS = 512


def _odd_layer(h2d, bsz, seq, norm_g, w_in, mu, w0, w2, a0, a2, k_k, k_a, r_k, lnx_g, lnx_b,
               fnet_w, w_out, final_g):
    t, d = h2d.shape
    c = w0.shape[-1]
    rs = 3 * c + 2 * LORA
    cf = fnet_w.shape[0] * fnet_w.shape[1]
    gw = WKV_LANES
    rkv_cols = [w_in[:, j * c + g * gw:j * c + (g + 1) * gw] for g in range(c // gw) for j in range(3)]
    w_perm = jnp.concatenate(rkv_cols + [w_in[:, rs:], w_in[:, 3 * c:rs]], axis=1).astype(BF16)
    p2 = _norm_proj(h2d, norm_g, w_perm, tm=min(ODD_PROJ_ROWS, t), tn=w_perm.shape[1], out_dtype=F32)
    zc_col, fd_col, zd_col, wa_col = 3 * c, 4 * c, 4 * c + cf, 4 * c + 2 * cf
    p3 = p2.reshape(bsz, seq, -1)
    lora = _lora_weights(w2, a2, gw)
    rws = min(WKV_ROWS, seq)
    y0, b0, y1, b1 = _wkv(p3, wa_col, mu, w0, a0, lora, k_k, k_a, r_k.reshape(-1), gw, rws)
    fn = _fnet(p3, fd_col, fnet_w).reshape(t, cf)
    flat = lambda a: a.reshape(t, c)
    return _odd_out(flat(y0), flat(y1), flat(b0), flat(b1), p2, zc_col // c, fn, zd_col // cf, h2d,
                    lnx_g, lnx_b, w_out.astype(BF16), final_g, tm=min(ODD_OUT_ROWS, t))


def kernel(x, e_norm_g, e_w_in, e_conv_w, e_sgu_ln_g, e_sgu_ln_b, e_sgu_w, e_sgu_b, e_w_out,
           o_norm_g, o_w_in, o_mu, o_w0, o_w2, o_a0, o_a2, o_k_k, o_k_a, o_r_k, o_lnx_g, o_lnx_b,
           o_fnet_w, o_w_out, final_norm_g):
    bsz, seq, d = x.shape
    assert e_norm_g.shape[0] == 1 and o_norm_g.shape[0] == 1, "two-layer trunk: one even, one odd layer"
    h = x.reshape(bsz * seq, d)
    h = _even_layer(h, seq, e_norm_g[0], e_w_in[0], e_conv_w[0], e_sgu_ln_g[0], e_sgu_ln_b[0],
                    e_sgu_w[0], e_sgu_b[0], e_w_out[0])
    out = _odd_layer(h, bsz, seq, o_norm_g[0], o_w_in[0], o_mu[0], o_w0[0], o_w2[0], o_a0[0],
                     o_a2[0], o_k_k[0], o_k_a[0], o_r_k[0], o_lnx_g[0], o_lnx_b[0], o_fnet_w[0],
                     o_w_out[0], final_norm_g)
    return out.reshape(bsz, seq, d)
```

```python
import functools
import math

import numpy as np
import jax
import jax.numpy as jnp
from jax import lax
from jax.experimental import pallas as pl
from jax.experimental.pallas import tpu as pltpu

F32 = jnp.float32
BF16 = jnp.bfloat16

RMS_EPS = 1e-6
SGU_LN_EPS = 1e-5
GN_EPS = 64e-5
SGU_CHUNK = 128
SGU_GROUPS = 8
HEAD_DIM = 64
LORA = 64

V7X_SUBLANES = 8
V7X_VMEM_BYTES = 64 * 1024 * 1024
VMEM_LIMIT = V7X_VMEM_BYTES - 8 * 1024 * 1024


def _cparams(sem):
    return pltpu.CompilerParams(dimension_semantics=sem, vmem_limit_bytes=VMEM_LIMIT)


def _silu(z):
    return z * (1.0 / (1.0 + jnp.exp(-z)))


def _sigmoid(z):
    return 1.0 / (1.0 + jnp.exp(-z))


def _dot(a, b):
    return jnp.dot(a.astype(BF16), b.astype(BF16), preferred_element_type=F32)


def _split(a, n):
    parts = []
    rem = a
    for _ in range(n):
        p = rem.astype(BF16)
        parts.append(p)
        rem = rem - p.astype(F32)
    return parts


def _dot_exact_rhs(a, b_bf16, n=3):
    acc = None
    for p in _split(a, n):
        t = jnp.dot(p, b_bf16, preferred_element_type=F32)
        acc = t if acc is None else acc + t
    return acc


def _proj_kernel(x_ref, g_ref, w_ref, o_ref):
    x = x_ref[...]
    ms = jnp.mean(x * x, axis=-1, keepdims=True)
    hn = (x * lax.rsqrt(ms + RMS_EPS) * g_ref[...]).astype(BF16)
    o_ref[...] = jnp.dot(hn, w_ref[...], preferred_element_type=F32)


def _norm_proj(x2d, g, w_bf16, tm):
    t, d = x2d.shape
    n = w_bf16.shape[1]
    assert t % tm == 0
    return pl.pallas_call(
        _proj_kernel,
        grid=(t // tm,),
        in_specs=[
            pl.BlockSpec((tm, d), lambda i: (i, 0)),
            pl.BlockSpec((1, d), lambda i: (0, 0)),
            pl.BlockSpec((d, n), lambda i: (0, 0), pipeline_mode=pl.Buffered(1)),
        ],
        out_specs=pl.BlockSpec((tm, n), lambda i: (i, 0)),
        out_shape=jax.ShapeDtypeStruct((t, n), F32),
        compiler_params=_cparams(("parallel",)),
        name="norm_proj",
    )(x2d, g.reshape(1, d), w_bf16)


def _even_kernel(seq, h_ref, hp_ref, hn_ref, ng_ref, wi_ref, cw_ref, lng_ref, lnb_ref, sw_ref,
                 sb_ref, wo_ref, o_ref):
    tm, d = h_ref.shape
    hr = hp_ref.shape[0]
    row0 = pl.program_id(0) * tm
    at_seq_start = (row0 % seq) == 0
    at_seq_end = ((row0 + tm) % seq) == 0

    h = h_ref[...]
    hx = jnp.concatenate([hp_ref[...], h, hn_ref[...]], axis=0)
    ms = jnp.mean(hx * hx, axis=-1, keepdims=True)
    hx = (hx * lax.rsqrt(ms + RMS_EPS) * ng_ref[...]).astype(BF16)
    hn = hx[hr:hr + tm]
    proj = lambda lhs, j: jnp.dot(lhs, wi_ref[:, j * d:(j + 1) * d], preferred_element_type=F32)

    xce = proj(hx, 0) * proj(hx, 2)
    xc = xce[hr:hr + tm]
    halo_prev = xce[hr - 1:hr]
    halo_next = xce[hr + tm:hr + tm + 1]
    ba, za = proj(hn, 1), proj(hn, 3)
    halo_prev = jnp.where(at_seq_start, 0.0, halo_prev)
    halo_next = jnp.where(at_seq_end, 0.0, halo_next)
    rows = lax.broadcasted_iota(jnp.int32, (tm, 1), 0)
    prev = jnp.where(rows == 0, halo_prev, pltpu.roll(xc, 1, 0))
    nxt = jnp.where(rows == tm - 1, halo_next, pltpu.roll(xc, tm - 1, 0))
    conv = cw_ref[0:1, :] * prev + cw_ref[1:2, :] * xc + cw_ref[2:3, :] * nxt
    ya = ba * conv * _silu(za)

    vb = proj(hn, 5)
    mu = jnp.mean(vb, axis=-1, keepdims=True)
    cen = vb - mu
    var = jnp.mean(cen * cen, axis=-1, keepdims=True)
    vn = (cen * lax.rsqrt(var + SGU_LN_EPS) * lng_ref[...] + lnb_ref[...]).astype(BF16)
    gw = d // SGU_GROUPS
    chunk_rows = []
    for n in range(tm // SGU_CHUNK):
        r0 = n * SGU_CHUNK
        cols = []
        for g in range(SGU_GROUPS):
            cols.append(jnp.dot(sw_ref[g], vn[r0:r0 + SGU_CHUNK, g * gw:(g + 1) * gw],
                                preferred_element_type=F32))
        chunk_rows.append(jnp.concatenate(cols, axis=1) + sb_ref[...])
    mixed = jnp.concatenate(chunk_rows, axis=0)
    yb = proj(hn, 4) * mixed * _silu(proj(hn, 6))

    out = h
    out = out + jnp.dot(ya.astype(BF16), wo_ref[0:d, :], preferred_element_type=F32)
    out = out + jnp.dot(yb.astype(BF16), wo_ref[d:2 * d, :], preferred_element_type=F32)
    o_ref[...] = out


EVEN_ROWS = 512


def _even_layer(h2d, seq, norm_g, w_in, conv_w, ln_g, ln_b, sgu_w, sgu_b, w_out):
    t, d = h2d.shape
    tm = min(EVEN_ROWS, t)
    hr = V7X_SUBLANES
    nblk = t // hr
    rpt = tm // hr
    bias_full = jnp.repeat(sgu_b.T, d // SGU_GROUPS, axis=1)
    sgu_w_bf16 = sgu_w.astype(BF16)
    w_in_bf16 = w_in.astype(BF16)
    w_out_bf16 = w_out.astype(BF16)
    const = lambda a: pl.BlockSpec(a.shape, lambda i: (0,) * a.ndim, pipeline_mode=pl.Buffered(1))
    row = lambda a: a.reshape(1, d)
    consts = (row(norm_g), w_in_bf16, conv_w, row(ln_g), row(ln_b), sgu_w_bf16, bias_full, w_out_bf16)
    return pl.pallas_call(
        functools.partial(_even_kernel, seq),
        grid=(t // tm,),
        in_specs=[
            pl.BlockSpec((tm, d), lambda i: (i, 0)),
            pl.BlockSpec((hr, d), lambda i: (jnp.maximum(i * rpt - 1, 0), 0)),
            pl.BlockSpec((hr, d), lambda i: (jnp.minimum((i + 1) * rpt, nblk - 1), 0)),
        ] + [const(a) for a in consts],
        out_specs=pl.BlockSpec((tm, d), lambda i: (i, 0)),
        out_shape=jax.ShapeDtypeStruct((t, d), F32),
        compiler_params=_cparams(("parallel",)),
        name="even_layer",
    )(h2d, h2d, h2d, *consts)


def _dot3(a, b):
    ah, al = _split(a, 2)
    bh, bl = _split(b, 2)
    d = lambda x, y: jnp.dot(x, y, preferred_element_type=F32)
    return d(ah, bh) + (d(ah, bl) + d(al, bh))


_NN = (((2,), (1,)), ((0,), (0,)))
_NT = (((2,), (2,)), ((0,), (0,)))


def _bmm(a, b, dims):
    return lax.dot_general(a.astype(BF16), b.astype(BF16), dims, preferred_element_type=F32)


def _segsum(x, bd_bf16, n=3):
    w = bd_bf16.shape[0]
    cols = [_dot_exact_rhs(x[:, j:j + w], bd_bf16, n) for j in range(0, x.shape[1], w)]
    return cols[0] if len(cols) == 1 else jnp.concatenate(cols, axis=1)


WKV_CHUNK = 64


WKV_DIR_SCRATCH = 9
WKV_VEC_PER_DIR = 5
WKV_VEC_SHARED = 2 * WKV_VEC_PER_DIR
WKV_VEC_ROWS = 16


def _wkv_kernel(nstep, *refs):
    ns = WKV_DIR_SCRATCH
    streams = refs[0:4]
    vec_ref, muwa_ref, tri_ref, lora_ref, bd_ref = refs[4:9]
    outs = refs[9:13]
    scratch = refs[13:]
    gw = bd_ref.shape[0]
    vec = lambda r: vec_ref.at[pl.ds(r, 1)]
    shared = (vec(WKV_VEC_SHARED), vec(WKV_VEC_SHARED + 1), vec(WKV_VEC_SHARED + 2), bd_ref)

    def dir_inputs(d):
        rkv_ref, wa_ref = streams[2 * d:2 * d + 2]
        r, k, v = (rkv_ref.at[:, pl.ds(j * gw, gw)] for j in range(3))
        base = d * WKV_VEC_PER_DIR
        return (r, k, v, wa_ref, vec(base), vec(base + 1), vec(base + 2), muwa_ref.at[pl.ds(d, 1)],
                vec(base + 3), vec(base + 4), lora_ref.at[d], tri_ref.at[d])

    step = pl.program_id(0)
    last = pl.num_programs(0) - 1
    chain_start = (step % nstep) == 0

    @pl.when(step == 0)
    def _():
        for ref in scratch:
            ref[...] = jnp.zeros_like(ref)

    def body(with_local):
        per_dir = [_wkv_direction(d, dir_inputs(d), shared, outs[2 * d:2 * d + 2],
                                  scratch[d * ns:(d + 1) * ns], with_local, chain_start)
                   for d in range(2)]
        chains = [c for c, _ in per_dir]
        local = [g for _, g in per_dir if g is not None]
        while local:
            local = [g for g in local if next(g, "done") != "done"]
            for chain in chains:
                next(chain, None)
        for chain in chains:
            for _ in chain:
                pass

    pl.when(step < last)(functools.partial(body, True))
    pl.when(step == last)(functools.partial(body, False))


def _wkv_direction(direction, ins, shared, outs, scratch, with_local, chain_start):
    (r_ref, k_ref, v_ref, wa_ref, mur_ref, muk_ref, muv_ref, muwa_ref, w0_ref, a0_ref, lora_ref,
     tri_ref) = ins
    kk_ref, ka_ref, rk_ref, bd_ref = shared
    y_ref, bon_ref = outs
    h_ref, cr_ref, ck_ref, cv_ref, cwa_ref, lhs_ref, n_ref, y0_ref, bonp_ref = scratch
    rws, gw = r_ref.shape
    L = WKV_CHUNK
    nch = rws // L

    lane_head = lax.broadcasted_iota(jnp.int32, (1, gw), 1) // HEAD_DIM

    def recurrence():
        h = h_ref[...]
        ys = [None] * nch
        for c in (range(nch) if direction == 0 else range(nch - 1, -1, -1)):
            hb = h.astype(BF16)
            hbd = jnp.concatenate([jnp.where(lane_head == i, hb, 0.0)
                                   for i in range(gw // HEAD_DIM)], axis=0)
            both = jnp.dot(lhs_ref[c], hbd, preferred_element_type=F32)
            ys[c] = both[:L] + y0_ref[c]
            h = both[L:] + n_ref[c]
            yield
        h_ref[...] = jnp.where(chain_start, 0.0, h)
        y_ref[...] = jnp.concatenate(ys, axis=0).astype(y_ref.dtype)

    chain = recurrence()
    bon_ref[...] = bonp_ref[...].astype(bon_ref.dtype)
    if not with_local:
        return chain, None

    streams = ((r_ref, cr_ref), (k_ref, ck_ref), (v_ref, cv_ref), (wa_ref, cwa_ref))
    edge = [jnp.where(chain_start, 0.0, c_ref[...]) for _, c_ref in streams]
    far = rws - 1 if direction == 0 else 0
    for x_ref, c_ref in streams:
        c_ref[...] = x_ref[far:far + 1, :]

    local = _wkv_local_stages(direction, 0, nch, edge, r_ref, k_ref, v_ref, wa_ref, mur_ref,
                              muk_ref, muv_ref, muwa_ref, w0_ref, a0_ref, kk_ref, ka_ref, rk_ref,
                              lora_ref, tri_ref, bd_ref, lhs_ref, n_ref, y0_ref, bonp_ref, chain)
    return chain, local


def _wkv_local_stages(direction, c0, ncg, edge, r_ref, k_ref, v_ref, wa_ref, mur_ref, muk_ref,
                      muv_ref, muwa_ref, w0_ref, a0_ref, kk_ref, ka_ref, rk_ref, lora_ref, tri_ref,
                      bd_ref, lhs_ref, n_ref, y0_ref, bonp_ref, chain):
    L = WKV_CHUNK
    nch = ncg
    rws, gw = r_ref.shape
    nheads = gw // HEAD_DIM
    rg = ncg * L
    row0 = c0 * L
    rows = lax.broadcasted_iota(jnp.int32, (rg, 1), 0)

    def token_shift(x_ref, edge_row, mu):
        x = x_ref[row0:row0 + rg, :]
        if direction == 0:
            nb = edge_row if row0 == 0 else x_ref[row0 - 1:row0, :]
            sh = jnp.where(rows == 0, nb, pltpu.roll(x, 1, 0))
        else:
            nb = edge_row if row0 + rg == rws else x_ref[row0 + rg:row0 + rg + 1, :]
            sh = jnp.where(rows == rg - 1, nb, pltpu.roll(x, rg - 1, 0))
        return x + mu * (sh - x)

    r = token_shift(r_ref, edge[0], mur_ref[...])
    k = token_shift(k_ref, edge[1], muk_ref[...])
    v = token_shift(v_ref, edge[2], muv_ref[...])
    wa = token_shift(wa_ref, edge[3], muwa_ref[...])
    yield

    lane_wa = lax.broadcasted_iota(jnp.int32, (1, wa.shape[1]), 1)
    wa = jnp.where(lane_wa < LORA, jnp.tanh(wa), wa)
    za = _dot(wa, lora_ref[...])
    zw = w0_ref[...] + za[:, :gw]
    a = _sigmoid(a0_ref[...] + za[:, gw:])
    lw = -math.exp(-0.5) * _sigmoid(zw)

    bd = bd_ref[...]
    kk = k * kk_ref[...]
    kk = kk * lax.rsqrt(jnp.maximum(_segsum(kk * kk, bd, 1), 1e-12))
    k2 = k * (1.0 + (a - 1.0) * ka_ref[...])
    bonp_ref[row0:row0 + rg, :] = _segsum(r * k2 * rk_ref[...], bd, 1) * v
    yield

    tri2 = jnp.broadcast_to(jnp.concatenate([tri_ref[...]] * 2, axis=1), (nch, L, 2 * L))
    parts = jnp.concatenate([p.reshape(nch, L, gw) for p in _split(lw, 2)], axis=1)
    cs = lax.dot_general(tri2, parts, _NN, preferred_element_type=F32).reshape(rg, gw)
    yield
    e_inc = jnp.exp(cs)
    e_inv = jnp.exp(-cs)
    e_exc = jnp.exp(cs - lw)
    to3 = lambda x: x.reshape(nch, L, gw)
    rt = to3(r * e_inc)
    kt = to3(k2 * e_inv)
    at = to3(-kk * e_exc)
    bt = to3(kk * a * e_inv)
    v3 = to3(v)
    last = L - 1 if direction == 0 else 0
    e_end = to3(e_inc)[:, last:last + 1, :]
    kh = kt * e_end
    bh = bt * e_end

    lane = lax.broadcasted_iota(jnp.int32, (1, 1, gw), 2)
    head_masks = [(lane // HEAD_DIM) == h for h in range(nheads)]

    def blockdiag(x3):
        return jnp.concatenate([jnp.where(m, x3, 0.0) for m in head_masks], axis=1)

    def hmm(lp, *xps):
        rhs = [blockdiag(xp.astype(BF16)) for xp in xps]
        rhs = rhs[0] if len(rhs) == 1 else jnp.concatenate(rhs, axis=2)
        return _bmm(lp, rhs, _NN)

    x_ar = jnp.concatenate([at, rt], axis=1)
    g = _bmm(x_ar, jnp.concatenate([blockdiag(bt.astype(BF16)), blockdiag(kt.astype(BF16))], axis=1),
             _NT)
    g_b, g_k = g[:, :, :gw], g[:, :, gw:]
    yield

    t_idx = lax.broadcasted_iota(jnp.int32, (1, L, gw), 1)
    s_idx = lax.broadcasted_iota(jnp.int32, (1, L, gw), 2) % L
    if direction == 0:
        strict, incl = s_idx < t_idx, s_idx <= t_idx
    else:
        strict, incl = s_idx > t_idx, s_idx >= t_idx
    a_ab = jnp.where(strict, g_b[:, :L], 0.0)
    a_rb = jnp.where(incl, g_b[:, L:], 0.0)
    a_ak = jnp.where(strict, g_k[:, :L], 0.0)
    a_rk = jnp.where(incl, g_k[:, L:], 0.0)

    tinv = jnp.where(s_idx == t_idx, 1.0, 0.0) + a_ab
    apow = hmm(a_ab, a_ab)
    nsq = int(math.log2(L)) - 1
    for i in range(nsq):
        yield
        if i + 1 < nsq:
            both = hmm(apow, apow, tinv)
            apow, tinv = both[:, :, :gw], tinv + both[:, :, gw:]
        else:
            tinv = tinv + hmm(apow, tinv)

    akv = _bmm(jnp.concatenate([a_ak, a_rk], axis=1), blockdiag(v3.astype(BF16)), _NN)
    wu = hmm(tinv, at, akv[:, :L])
    wt, u0 = wu[:, :, :gw], wu[:, :, gw:]
    yield
    ry = hmm(a_rb, wt, u0)
    rh = rt + ry[:, :, :gw]
    y0 = ry[:, :, gw:] + akv[:, L:]

    bk_t = jnp.swapaxes(jnp.concatenate([bh, kh], axis=1), 1, 2)
    bk_pack = jnp.concatenate([bk_t[:, i * HEAD_DIM:(i + 1) * HEAD_DIM] for i in range(nheads)],
                              axis=2).astype(BF16)
    rhs = jnp.concatenate([wu, jnp.concatenate([jnp.zeros_like(v3), v3], axis=2)],
                          axis=1).astype(BF16)
    lane2 = lax.broadcasted_iota(jnp.int32, (1, 1, 2 * gw), 2) % gw // HEAD_DIM
    rhs_heads = jnp.concatenate([jnp.where(lane2 == i, rhs, 0.0) for i in range(nheads)], axis=1)
    mn = _bmm(bk_pack, rhs_heads, _NN)
    ri = lax.broadcasted_iota(jnp.int32, (1, HEAD_DIM, gw), 1)
    ci = lax.broadcasted_iota(jnp.int32, (1, HEAD_DIM, gw), 2) % HEAD_DIM
    m_mat = mn[:, :, :gw] + jnp.where(ri == ci, e_end, 0.0)
    yield
    for _ in chain:
        pass
    lhs_ref[c0:c0 + ncg] = jnp.concatenate([rh, m_mat], axis=1).astype(BF16)
    n_ref[c0:c0 + ncg] = mn[:, :, gw:]
    y0_ref[c0:c0 + ncg] = y0


def _wkv(p3, wa_col, mu, w0, a0, lora_w, k_k, k_a, r_k, gw, rws):
    bsz, seq, _ = p3.shape
    c = w0.shape[-1]
    ng = c // gw
    nstep = seq // rws
    L = WKV_CHUNK
    nch = rws // L
    wa_w = 2 * LORA
    t = np.arange(L)
    hd = np.arange(gw) // HEAD_DIM
    bd = jnp.asarray(hd[:, None] == hd[None, :], BF16)
    rows = [x for d in range(2) for x in (mu[d][:c], mu[d][c:2 * c], mu[d][2 * c:3 * c], w0[d], a0[d])]
    rows += [k_k, k_a, r_k]
    vecs = jnp.zeros((WKV_VEC_ROWS, c), F32).at[:len(rows)].set(jnp.stack(rows))
    muwa = jnp.zeros((V7X_SUBLANES, wa_w), F32).at[:2].set(jnp.stack([mu[0][3 * c:], mu[1][3 * c:]]))
    tri = jnp.asarray(np.stack([t[None, :] <= t[:, None], t[None, :] >= t[:, None]]), BF16)
    total = bsz * ng * nstep
    rd = lambda i: jnp.minimum(i, total - 1)
    wr = lambda i: jnp.maximum(i - 1, 0)
    batch_of = lambda i: i // (ng * nstep)
    group_of = lambda i: (i // nstep) % ng
    const = lambda a: pl.BlockSpec(a.shape, lambda i: (0,) * a.ndim)

    in_specs, args, out_specs, scratch = [], [], [], []
    for direction in range(2):
        blk = (lambda i: i % nstep) if direction == 0 else (lambda i: nstep - 1 - i % nstep)
        in_specs += [
            pl.BlockSpec((None, rws, 3 * gw),
                         lambda i, blk=blk: (batch_of(rd(i)), blk(rd(i)), group_of(rd(i)))),
            pl.BlockSpec((None, rws, wa_w),
                         lambda i, blk=blk: (batch_of(rd(i)), blk(rd(i)), wa_col // wa_w)),
        ]
        args += [p3, p3]
        out_specs += [pl.BlockSpec(
            (None, rws, gw),
            lambda i, blk=blk: (batch_of(wr(i)), blk(wr(i)), group_of(wr(i))))] * 2
        scratch += [pltpu.VMEM((HEAD_DIM, gw), F32), pltpu.VMEM((1, gw), F32), pltpu.VMEM((1, gw), F32),
                    pltpu.VMEM((1, gw), F32), pltpu.VMEM((1, wa_w), F32),
                    pltpu.VMEM((nch, L + HEAD_DIM, gw), BF16), pltpu.VMEM((nch, HEAD_DIM, gw), F32),
                    pltpu.VMEM((nch, L, gw), F32), pltpu.VMEM((rws, gw), F32)]
    assert len(scratch) == 2 * WKV_DIR_SCRATCH
    in_specs += [
        pl.BlockSpec((WKV_VEC_ROWS, gw), lambda i: (0, group_of(rd(i)))),
        const(muwa), const(tri),
        pl.BlockSpec((2, None, wa_w, 2 * gw), lambda i: (0, group_of(rd(i)), 0, 0)),
        const(bd),
    ]
    args += [vecs, muwa, tri, lora_w, bd]
    return pl.pallas_call(
        functools.partial(_wkv_kernel, nstep),
        grid=(total + 1,),
        in_specs=in_specs,
        out_specs=out_specs,
        out_shape=[jax.ShapeDtypeStruct((bsz, seq, c), BF16)] * 4,
        scratch_shapes=scratch,
        compiler_params=_cparams(("arbitrary",)),
        name="wkv",
    )(*args)


def _lora_weights(w2, a2, gw):
    nd, lo, c = w2.shape
    ng = c // gw
    w2g = w2.reshape(nd, lo, ng, gw).transpose(0, 2, 1, 3)
    a2g = a2.reshape(nd, lo, ng, gw).transpose(0, 2, 1, 3)
    z = jnp.zeros_like(w2g)
    top = jnp.concatenate([w2g, z], axis=3)
    bot = jnp.concatenate([z, a2g], axis=3)
    return jnp.concatenate([top, bot], axis=2).astype(BF16)


FNET_N2 = 128


FNET_PITCH_PAD = 8
FNET_UNROLL = 8


def _cos_sin(idx, period):
    ang = 2.0 * np.pi * (idx % period) / period
    return np.cos(ang), np.sin(ang)


def _fnet_weight_kernel(scale, c_ref, s_ref, w_ref, o_ref):
    for g in range(w_ref.shape[0]):
        wc = _dot3(c_ref[...], w_ref[g])
        ws = _dot3(s_ref[...], w_ref[g])
        o_ref[g] = jnp.concatenate([wc, -ws], axis=1) * scale


def _fnet_kernel(n1, n2, f_ref, w_ref, m1_ref, m2_ref, o_ref, gr_ref, gi_ref, zr_ref, zi_ref):
    gd = f_ref.shape[1]
    pitch = n2 + FNET_PITCH_PAD
    w = w_ref[...].astype(BF16)

    def chan_body(s1, carry):
        src = pl.ds(pl.multiple_of(s1 * n2, n2), n2)
        dst = pl.ds(pl.multiple_of(s1 * pitch, V7X_SUBLANES), n2)
        z = jnp.dot(f_ref[src, :].astype(BF16), w, preferred_element_type=F32)
        gr_ref[dst, :] = z[:, :gd]
        gi_ref[dst, :] = z[:, gd:]
        return carry

    lax.fori_loop(0, n1, chan_body, 0, unroll=min(FNET_UNROLL, n1))

    m1 = m1_ref[...].astype(BF16)

    def stage1_body(s2, carry):
        idx = pl.ds(s2, n1, stride=pitch)
        x = jnp.concatenate([gr_ref[idx, :], gi_ref[idx, :]], axis=0)
        z = jnp.dot(m1, x.astype(BF16), preferred_element_type=F32)
        zr_ref[idx, :] = z[:n1]
        zi_ref[idx, :] = z[n1:]
        return carry

    lax.fori_loop(0, n2, stage1_body, 0, unroll=2 * FNET_UNROLL)

    def stage2_body(s1, carry):
        src = pl.ds(pl.multiple_of(s1 * pitch, V7X_SUBLANES), n2)
        zz = jnp.concatenate([zr_ref[src, :], zi_ref[src, :]], axis=0)
        gr_ref[src, :] = jnp.dot(m2_ref[s1].astype(BF16), zz.astype(BF16),
                                 preferred_element_type=F32)
        return carry

    lax.fori_loop(0, n1, stage2_body, 0, unroll=min(FNET_UNROLL, n1))

    def reorder_body(s2, carry):
        dst = pl.ds(pl.multiple_of(s2 * n1, V7X_SUBLANES), n1)
        o_ref[dst, :] = gr_ref[pl.ds(s2, n1, stride=pitch), :]
        return carry

    lax.fori_loop(0, n2, reorder_body, 0, unroll=2 * FNET_UNROLL)


def _fnet(p3, col0, w_f):
    bsz, seq, _ = p3.shape
    ng, gd, _ = w_f.shape
    n2 = FNET_N2
    n1 = seq // n2
    scale = 1.0 / math.sqrt(seq * gd)
    ar = np.arange
    cd, sd = _cos_sin(ar(gd)[:, None] * ar(gd)[None, :], gd)
    c1, s1 = _cos_sin(ar(n1)[:, None] * ar(n1)[None, :], n1)
    m1 = np.block([[c1, s1], [-s1, c1]])
    c2, s2 = _cos_sin(ar(n2)[None, None, :] * (ar(n1)[:, None, None] + n1 * ar(n2)[None, :, None]), seq)
    m1 = jnp.asarray(m1, F32)
    m2 = jnp.asarray(np.concatenate([c2, s2], axis=2), F32)

    wcat = pl.pallas_call(
        functools.partial(_fnet_weight_kernel, scale),
        out_shape=jax.ShapeDtypeStruct((ng, gd, 2 * gd), F32),
        name="fnet_weights",
    )(jnp.asarray(cd, F32), jnp.asarray(sd, F32), w_f)

    pitch_rows = n1 * (n2 + FNET_PITCH_PAD)
    const2 = lambda a: pl.BlockSpec(a.shape, lambda b, g: (0, 0))
    const3 = lambda a: pl.BlockSpec(a.shape, lambda b, g: (0, 0, 0), pipeline_mode=pl.Buffered(1))
    return pl.pallas_call(
        functools.partial(_fnet_kernel, n1, n2),
        grid=(bsz, ng),
        in_specs=[pl.BlockSpec((None, seq, gd), lambda b, g: (b, 0, col0 // gd + g)),
                  pl.BlockSpec((None, gd, 2 * gd), lambda b, g: (g, 0, 0)),
                  const2(m1), const3(m2)],
        out_specs=pl.BlockSpec((None, seq, gd), lambda b, g: (b, 0, g)),
        out_shape=jax.ShapeDtypeStruct((bsz, seq, ng * gd), F32),
        scratch_shapes=[pltpu.VMEM((pitch_rows, gd), F32)] * 4,
        compiler_params=_cparams(("parallel", "parallel")),
        name="fnet",
    )(p3, wcat, m1, m2)


def _odd_out_kernel(y0_ref, y1_ref, b0_ref, b1_ref, zc_ref, fn_ref, zd_ref, h_ref, lng_ref,
                    lnb_ref, bd_ref, wo_ref, fg_ref, o_ref):
    c = y0_ref.shape[1]
    bd = bd_ref[...]
    inv_n = 1.0 / HEAD_DIM
    ysum = y0_ref[...].astype(F32) + y1_ref[...].astype(F32)
    mean = _segsum(ysum, bd, 2) * inv_n
    cen = ysum - mean
    var = _segsum(cen * cen, bd, 1) * inv_n
    gn = cen * lax.rsqrt(var + GN_EPS) * lng_ref[...] + lnb_ref[...]
    yc = (gn + b0_ref[...].astype(F32) + b1_ref[...].astype(F32)) * _silu(zc_ref[...])
    yd = fn_ref[...] * _silu(zd_ref[...])
    out = h_ref[...]
    out = out + jnp.dot(yc.astype(BF16), wo_ref[0:c, :], preferred_element_type=F32)
    out = out + jnp.dot(yd.astype(BF16), wo_ref[c:, :], preferred_element_type=F32)
    ms = jnp.mean(out * out, axis=-1, keepdims=True)
    o_ref[...] = out * lax.rsqrt(ms + RMS_EPS) * fg_ref[...]


def _odd_out(y0, y1, b0, b1, p2d, zc_blk, fn, zd_blk, h2d, lnx_g, lnx_b, w_out_bf16, final_g, tm):
    t, d = h2d.shape
    c = y0.shape[1]
    cf = fn.shape[1]
    hd = np.arange(2 * HEAD_DIM) // HEAD_DIM
    bd = jnp.asarray(hd[:, None] == hd[None, :], BF16)
    rowblk = lambda w: pl.BlockSpec((tm, w), lambda i: (i, 0))
    full = lambda shape: pl.BlockSpec(shape, lambda i: (0, 0))
    return pl.pallas_call(
        _odd_out_kernel,
        grid=(t // tm,),
        in_specs=[rowblk(c), rowblk(c), rowblk(c), rowblk(c),
                  pl.BlockSpec((tm, c), lambda i: (i, zc_blk)),
                  rowblk(cf),
                  pl.BlockSpec((tm, cf), lambda i: (i, zd_blk)),
                  rowblk(d), full((1, c)), full((1, c)), full(bd.shape),
                  full(w_out_bf16.shape), full((1, d))],
        out_specs=rowblk(d),
        out_shape=jax.ShapeDtypeStruct((t, d), F32),
        compiler_params=_cparams(("parallel",)),
        name="odd_out",
    )(y0, y1, b0, b1, p2d, fn, p2d, h2d, lnx_g.reshape(1, c), lnx_b.reshape(1, c), bd,
      w_out_bf16, final_g.reshape(1, d))


WKV_LANES = 128
WKV_ROWS = 512
ODD_PROJ_ROWS = 512
ODD_OUT_ROWS = 512


def _odd_layer(h2d, bsz, seq, norm_g, w_in, mu, w0, w2, a0, a2, k_k, k_a, r_k, lnx_g, lnx_b,
               fnet_w, w_out, final_g):
    t, d = h2d.shape
    c = w0.shape[-1]
    rs = 3 * c + 2 * LORA
    cf = fnet_w.shape[0] * fnet_w.shape[1]
    gw = WKV_LANES
    rkv_cols = [w_in[:, j * c + g * gw:j * c + (g + 1) * gw] for g in range(c // gw) for j in range(3)]
    w_perm = jnp.concatenate(rkv_cols + [w_in[:, rs:], w_in[:, 3 * c:rs]], axis=1).astype(BF16)
    p2 = _norm_proj(h2d, norm_g, w_perm, tm=min(ODD_PROJ_ROWS, t))
    zc_col, fd_col, zd_col, wa_col = 3 * c, 4 * c, 4 * c + cf, 4 * c + 2 * cf
    p3 = p2.reshape(bsz, seq, -1)
    lora = _lora_weights(w2, a2, gw)
    rws = min(WKV_ROWS, seq)
    y0, b0, y1, b1 = _wkv(p3, wa_col, mu, w0, a0, lora, k_k, k_a, r_k.reshape(-1), gw, rws)
    fn = _fnet(p3, fd_col, fnet_w).reshape(t, cf)
    flat = lambda a: a.reshape(t, c)
    return _odd_out(flat(y0), flat(y1), flat(b0), flat(b1), p2, zc_col // c, fn, zd_col // cf, h2d,
                    lnx_g, lnx_b, w_out.astype(BF16), final_g, tm=min(ODD_OUT_ROWS, t))


def kernel(x, e_norm_g, e_w_in, e_conv_w, e_sgu_ln_g, e_sgu_ln_b, e_sgu_w, e_sgu_b, e_w_out,
           o_norm_g, o_w_in, o_mu, o_w0, o_w2, o_a0, o_a2, o_k_k, o_k_a, o_r_k, o_lnx_g, o_lnx_b,
           o_fnet_w, o_w_out, final_norm_g):
    bsz, seq, d = x.shape
    assert e_norm_g.shape[0] == 1 and o_norm_g.shape[0] == 1, "two-layer trunk: one even, one odd layer"
    h = x.reshape(bsz * seq, d)
    h = _even_layer(h, seq, e_norm_g[0], e_w_in[0], e_conv_w[0], e_sgu_ln_g[0], e_sgu_ln_b[0],
                    e_sgu_w[0], e_sgu_b[0], e_w_out[0])
    out = _odd_layer(h, bsz, seq, o_norm_g[0], o_w_in[0], o_mu[0], o_w0[0], o_w2[0], o_a0[0],
                     o_a2[0], o_k_k[0], o_k_a[0], o_r_k[0], o_lnx_g[0], o_lnx_b[0], o_fnet_w[0],
                     o_w_out[0], final_norm_g)
    return out.reshape(bsz, seq, d)
```

```python
import functools
import math

import numpy as np
import jax
import jax.numpy as jnp
from jax import lax
from jax.experimental import pallas as pl
from jax.experimental.pallas import tpu as pltpu

F32 = jnp.float32
BF16 = jnp.bfloat16

RMS_EPS = 1e-6
SGU_LN_EPS = 1e-5
GN_EPS = 64e-5
SGU_CHUNK = 128
SGU_GROUPS = 8
HEAD_DIM = 64
LORA = 64

V7X_SUBLANES = 8
V7X_VMEM_BYTES = 64 * 1024 * 1024
VMEM_LIMIT = V7X_VMEM_BYTES - 8 * 1024 * 1024


def _cparams(sem):
    return pltpu.CompilerParams(dimension_semantics=sem, vmem_limit_bytes=VMEM_LIMIT)


def _silu(z):
    return z * (1.0 / (1.0 + jnp.exp(-z)))


def _sigmoid(z):
    return 1.0 / (1.0 + jnp.exp(-z))


def _dot(a, b):
    return jnp.dot(a.astype(BF16), b.astype(BF16), preferred_element_type=F32)


def _split(a, n):
    parts = []
    rem = a
    for _ in range(n):
        p = rem.astype(BF16)
        parts.append(p)
        rem = rem - p.astype(F32)
    return parts


def _dot_exact_rhs(a, b_bf16, n=3):
    acc = None
    for p in _split(a, n):
        t = jnp.dot(p, b_bf16, preferred_element_type=F32)
        acc = t if acc is None else acc + t
    return acc


def _proj_kernel(x_ref, g_ref, w_ref, o_ref):
    x = x_ref[...]
    ms = jnp.mean(x * x, axis=-1, keepdims=True)
    hn = (x * lax.rsqrt(ms + RMS_EPS) * g_ref[...]).astype(BF16)
    o_ref[...] = jnp.dot(hn, w_ref[...], preferred_element_type=F32)


def _norm_proj(x2d, g, w_bf16, tm):
    t, d = x2d.shape
    n = w_bf16.shape[1]
    assert t % tm == 0
    return pl.pallas_call(
        _proj_kernel,
        grid=(t // tm,),
        in_specs=[
            pl.BlockSpec((tm, d), lambda i: (i, 0)),
            pl.BlockSpec((1, d), lambda i: (0, 0)),
            pl.BlockSpec((d, n), lambda i: (0, 0), pipeline_mode=pl.Buffered(1)),
        ],
        out_specs=pl.BlockSpec((tm, n), lambda i: (i, 0)),
        out_shape=jax.ShapeDtypeStruct((t, n), F32),
        compiler_params=_cparams(("parallel",)),
        name="norm_proj",
    )(x2d, g.reshape(1, d), w_bf16)


def _even_kernel(seq, h_ref, hp_ref, hn_ref, ng_ref, wi_ref, cw_ref, lng_ref, lnb_ref, sw_ref,
                 sb_ref, wo_ref, o_ref):
    tm, d = h_ref.shape
    hr = hp_ref.shape[0]
    row0 = pl.program_id(0) * tm
    at_seq_start = (row0 % seq) == 0
    at_seq_end = ((row0 + tm) % seq) == 0

    h = h_ref[...]
    hx = jnp.concatenate([hp_ref[...], h, hn_ref[...]], axis=0)
    ms = jnp.mean(hx * hx, axis=-1, keepdims=True)
    hx = (hx * lax.rsqrt(ms + RMS_EPS) * ng_ref[...]).astype(BF16)
    hn = hx[hr:hr + tm]
    proj = lambda lhs, j: jnp.dot(lhs, wi_ref[:, j * d:(j + 1) * d], preferred_element_type=F32)

    xce = proj(hx, 0) * proj(hx, 2)
    xc = xce[hr:hr + tm]
    halo_prev = xce[hr - 1:hr]
    halo_next = xce[hr + tm:hr + tm + 1]
    ba, za = proj(hn, 1), proj(hn, 3)
    halo_prev = jnp.where(at_seq_start, 0.0, halo_prev)
    halo_next = jnp.where(at_seq_end, 0.0, halo_next)
    rows = lax.broadcasted_iota(jnp.int32, (tm, 1), 0)
    prev = jnp.where(rows == 0, halo_prev, pltpu.roll(xc, 1, 0))
    nxt = jnp.where(rows == tm - 1, halo_next, pltpu.roll(xc, tm - 1, 0))
    conv = cw_ref[0:1, :] * prev + cw_ref[1:2, :] * xc + cw_ref[2:3, :] * nxt
    ya = ba * conv * _silu(za)

    vb = proj(hn, 5)
    mu = jnp.mean(vb, axis=-1, keepdims=True)
    cen = vb - mu
    var = jnp.mean(cen * cen, axis=-1, keepdims=True)
    vn = (cen * lax.rsqrt(var + SGU_LN_EPS) * lng_ref[...] + lnb_ref[...]).astype(BF16)
    gw = d // SGU_GROUPS
    chunk_rows = []
    for n in range(tm // SGU_CHUNK):
        r0 = n * SGU_CHUNK
        cols = []
        for g in range(SGU_GROUPS):
            cols.append(jnp.dot(sw_ref[g], vn[r0:r0 + SGU_CHUNK, g * gw:(g + 1) * gw],
                                preferred_element_type=F32))
        chunk_rows.append(jnp.concatenate(cols, axis=1) + sb_ref[...])
    mixed = jnp.concatenate(chunk_rows, axis=0)
    yb = proj(hn, 4) * mixed * _silu(proj(hn, 6))

    out = h
    out = out + jnp.dot(ya.astype(BF16), wo_ref[0:d, :], preferred_element_type=F32)
    out = out + jnp.dot(yb.astype(BF16), wo_ref[d:2 * d, :], preferred_element_type=F32)
    o_ref[...] = out


EVEN_ROWS = 512


def _even_layer(h2d, seq, norm_g, w_in, conv_w, ln_g, ln_b, sgu_w, sgu_b, w_out):
    t, d = h2d.shape
    tm = min(EVEN_ROWS, t)
    hr = V7X_SUBLANES
    nblk = t // hr
    rpt = tm // hr
    bias_full = jnp.repeat(sgu_b.T, d // SGU_GROUPS, axis=1)
    sgu_w_bf16 = sgu_w.astype(BF16)
    w_in_bf16 = w_in.astype(BF16)
    w_out_bf16 = w_out.astype(BF16)
    const = lambda a: pl.BlockSpec(a.shape, lambda i: (0,) * a.ndim, pipeline_mode=pl.Buffered(1))
    row = lambda a: a.reshape(1, d)
    consts = (row(norm_g), w_in_bf16, conv_w, row(ln_g), row(ln_b), sgu_w_bf16, bias_full, w_out_bf16)
    return pl.pallas_call(
        functools.partial(_even_kernel, seq),
        grid=(t // tm,),
        in_specs=[
            pl.BlockSpec((tm, d), lambda i: (i, 0)),
            pl.BlockSpec((hr, d), lambda i: (jnp.maximum(i * rpt - 1, 0), 0)),
            pl.BlockSpec((hr, d), lambda i: (jnp.minimum((i + 1) * rpt, nblk - 1), 0)),
        ] + [const(a) for a in consts],
        out_specs=pl.BlockSpec((tm, d), lambda i: (i, 0)),
        out_shape=jax.ShapeDtypeStruct((t, d), F32),
        compiler_params=_cparams(("parallel",)),
        name="even_layer",
    )(h2d, h2d, h2d, *consts)


def _dot3(a, b):
    ah, al = _split(a, 2)
    bh, bl = _split(b, 2)
    d = lambda x, y: jnp.dot(x, y, preferred_element_type=F32)
    return d(ah, bh) + (d(ah, bl) + d(al, bh))


_NN = (((2,), (1,)), ((0,), (0,)))
_NT = (((2,), (2,)), ((0,), (0,)))


def _bmm(a, b, dims):
    return lax.dot_general(a.astype(BF16), b.astype(BF16), dims, preferred_element_type=F32)


def _segsum(x, bd_bf16, n=3):
    w = bd_bf16.shape[0]
    cols = [_dot_exact_rhs(x[:, j:j + w], bd_bf16, n) for j in range(0, x.shape[1], w)]
    return cols[0] if len(cols) == 1 else jnp.concatenate(cols, axis=1)


WKV_CHUNK = 64


WKV_DIR_SCRATCH = 9
WKV_VEC_PER_DIR = 5
WKV_VEC_SHARED = 2 * WKV_VEC_PER_DIR
WKV_VEC_ROWS = 16


def _wkv_kernel(nstep, *refs):
    ns = WKV_DIR_SCRATCH
    streams = refs[0:4]
    vec_ref, muwa_ref, tri_ref, lora_ref, bd_ref = refs[4:9]
    outs = refs[9:13]
    scratch = refs[13:]
    gw = bd_ref.shape[0]
    vec = lambda r: vec_ref.at[pl.ds(r, 1)]
    shared = (vec(WKV_VEC_SHARED), vec(WKV_VEC_SHARED + 1), vec(WKV_VEC_SHARED + 2), bd_ref)

    def dir_inputs(d):
        rkv_ref, wa_ref = streams[2 * d:2 * d + 2]
        r, k, v = (rkv_ref.at[:, pl.ds(j * gw, gw)] for j in range(3))
        base = d * WKV_VEC_PER_DIR
        return (r, k, v, wa_ref, vec(base), vec(base + 1), vec(base + 2), muwa_ref.at[pl.ds(d, 1)],
                vec(base + 3), vec(base + 4), lora_ref.at[d], tri_ref.at[d])

    step = pl.program_id(0)
    last = pl.num_programs(0) - 1
    chain_start = (step % nstep) == 0

    @pl.when(step == 0)
    def _():
        for ref in scratch:
            ref[...] = jnp.zeros_like(ref)

    def body(with_local):
        per_dir = [_wkv_direction(d, dir_inputs(d), shared, outs[2 * d:2 * d + 2],
                                  scratch[d * ns:(d + 1) * ns], with_local, chain_start)
                   for d in range(2)]
        chains = [c for c, _ in per_dir]
        if with_local:
            local = _wkv_local_stages([e for _, e in per_dir], [dir_inputs(d) for d in range(2)],
                                      shared, [scratch[d * ns:(d + 1) * ns] for d in range(2)], chains)
            for _ in local:
                for chain in chains:
                    next(chain, None)
        for chain in chains:
            for _ in chain:
                pass

    pl.when(step < last)(functools.partial(body, True))
    pl.when(step == last)(functools.partial(body, False))


def _wkv_direction(direction, ins, shared, outs, scratch, with_local, chain_start):
    (r_ref, k_ref, v_ref, wa_ref, mur_ref, muk_ref, muv_ref, muwa_ref, w0_ref, a0_ref, lora_ref,
     tri_ref) = ins
    kk_ref, ka_ref, rk_ref, bd_ref = shared
    y_ref, bon_ref = outs
    h_ref, cr_ref, ck_ref, cv_ref, cwa_ref, lhs_ref, n_ref, y0_ref, bonp_ref = scratch
    rws, gw = r_ref.shape
    L = WKV_CHUNK
    nch = rws // L

    lane_head = lax.broadcasted_iota(jnp.int32, (1, gw), 1) // HEAD_DIM

    def recurrence():
        h = h_ref[...]
        ys = [None] * nch
        for c in (range(nch) if direction == 0 else range(nch - 1, -1, -1)):
            hb = h.astype(BF16)
            hbd = jnp.concatenate([jnp.where(lane_head == i, hb, 0.0)
                                   for i in range(gw // HEAD_DIM)], axis=0)
            both = jnp.dot(lhs_ref[c], hbd, preferred_element_type=F32)
            ys[c] = both[:L] + y0_ref[c]
            h = both[L:] + n_ref[c]
            yield
        h_ref[...] = jnp.where(chain_start, 0.0, h)
        y_ref[...] = jnp.concatenate(ys, axis=0).astype(y_ref.dtype)

    chain = recurrence()
    bon_ref[...] = bonp_ref[...].astype(bon_ref.dtype)
    if not with_local:
        return chain, None

    streams = ((r_ref, cr_ref), (k_ref, ck_ref), (v_ref, cv_ref), (wa_ref, cwa_ref))
    edge = [jnp.where(chain_start, 0.0, c_ref[...]) for _, c_ref in streams]
    far = rws - 1 if direction == 0 else 0
    for x_ref, c_ref in streams:
        c_ref[...] = x_ref[far:far + 1, :]

    return chain, edge


def _wkv_local_stages(edges, ins, shared, scratch, chains):
    kk_ref, ka_ref, rk_ref, bd_ref = shared
    L = WKV_CHUNK
    rws, gw = ins[0][0].shape
    nch = rws // L
    nheads = gw // HEAD_DIM
    rows = lax.broadcasted_iota(jnp.int32, (rws, 1), 0)
    bd = bd_ref[...]
    to3 = lambda x: x.reshape(nch, L, gw)
    both_dirs = lambda f: jnp.concatenate([f(d) for d in range(2)], axis=0)

    def token_shift(d, x_ref, edge_row, mu):
        x = x_ref[...]
        if d == 0:
            sh = jnp.where(rows == 0, edge_row, pltpu.roll(x, 1, 0))
        else:
            sh = jnp.where(rows == rws - 1, edge_row, pltpu.roll(x, rws - 1, 0))
        return x + mu * (sh - x)

    rkvw = []
    for d in range(2):
        r_ref, k_ref, v_ref, wa_ref, mur_ref, muk_ref, muv_ref, muwa_ref = ins[d][:8]
        rkvw.append((token_shift(d, r_ref, edges[d][0], mur_ref[...]),
                     token_shift(d, k_ref, edges[d][1], muk_ref[...]),
                     token_shift(d, v_ref, edges[d][2], muv_ref[...]),
                     token_shift(d, wa_ref, edges[d][3], muwa_ref[...])))
    yield

    prep = []
    for d in range(2):
        w0_ref, a0_ref, lora_ref = ins[d][8:11]
        r, k, v, wa = rkvw[d]
        lane_wa = lax.broadcasted_iota(jnp.int32, (1, wa.shape[1]), 1)
        wa = jnp.where(lane_wa < LORA, jnp.tanh(wa), wa)
        za = _dot(wa, lora_ref[...])
        zw = w0_ref[...] + za[:, :gw]
        a = _sigmoid(a0_ref[...] + za[:, gw:])
        lw = -math.exp(-0.5) * _sigmoid(zw)
        kk = k * kk_ref[...]
        kk = kk * lax.rsqrt(jnp.maximum(_segsum(kk * kk, bd, 1), 1e-12))
        k2 = k * (1.0 + (a - 1.0) * ka_ref[...])
        scratch[d][8][...] = _segsum(r * k2 * rk_ref[...], bd, 1) * v
        prep.append((r, k2, v, kk, a, lw))
    yield

    tri2 = both_dirs(lambda d: jnp.broadcast_to(jnp.concatenate([ins[d][11][...]] * 2, axis=1),
                                                (nch, L, 2 * L)))
    parts = both_dirs(lambda d: jnp.concatenate([p.reshape(nch, L, gw)
                                                 for p in _split(prep[d][5], 2)], axis=1))
    cs = lax.dot_general(tri2, parts, _NN, preferred_element_type=F32)
    yield
    r, k2, v3, kk, a, lw = (both_dirs(lambda d, j=j: to3(prep[d][j])) for j in range(6))
    e_inc = jnp.exp(cs)
    e_inv = jnp.exp(-cs)
    e_exc = jnp.exp(cs - lw)
    rt = r * e_inc
    kt = k2 * e_inv
    at = -kk * e_exc
    bt = kk * a * e_inv
    e_end = jnp.concatenate([e_inc[:nch, L - 1:L], e_inc[nch:, 0:1]], axis=0)
    kh = kt * e_end
    bh = bt * e_end

    lane = lax.broadcasted_iota(jnp.int32, (1, 1, gw), 2)
    head_masks = [(lane // HEAD_DIM) == h for h in range(nheads)]

    def blockdiag(x3):
        return jnp.concatenate([jnp.where(m, x3, 0.0) for m in head_masks], axis=1)

    def hmm(lp, *xps):
        rhs = [blockdiag(xp.astype(BF16)) for xp in xps]
        rhs = rhs[0] if len(rhs) == 1 else jnp.concatenate(rhs, axis=2)
        return _bmm(lp, rhs, _NN)

    x_ar = jnp.concatenate([at, rt], axis=1)
    g = _bmm(x_ar, jnp.concatenate([blockdiag(bt.astype(BF16)), blockdiag(kt.astype(BF16))], axis=1),
             _NT)
    g_b, g_k = g[:, :, :gw], g[:, :, gw:]
    yield

    t_idx = lax.broadcasted_iota(jnp.int32, (1, L, gw), 1)
    s_idx = lax.broadcasted_iota(jnp.int32, (1, L, gw), 2) % L

    def keep(x, fwd_mask, bwd_mask):
        return jnp.concatenate([jnp.where(fwd_mask, x[:nch], 0.0), jnp.where(bwd_mask, x[nch:], 0.0)],
                               axis=0)

    a_ab = keep(g_b[:, :L], s_idx < t_idx, s_idx > t_idx)
    a_rb = keep(g_b[:, L:], s_idx <= t_idx, s_idx >= t_idx)
    a_ak = keep(g_k[:, :L], s_idx < t_idx, s_idx > t_idx)
    a_rk = keep(g_k[:, L:], s_idx <= t_idx, s_idx >= t_idx)

    tinv = jnp.where(s_idx == t_idx, 1.0, 0.0) + a_ab
    apow = hmm(a_ab, a_ab)
    nsq = int(math.log2(L)) - 1
    for i in range(nsq):
        yield
        if i + 1 < nsq:
            both = hmm(apow, apow, tinv)
            apow, tinv = both[:, :, :gw], tinv + both[:, :, gw:]
        else:
            tinv = tinv + hmm(apow, tinv)

    akv = _bmm(jnp.concatenate([a_ak, a_rk], axis=1), blockdiag(v3.astype(BF16)), _NN)
    wu = hmm(tinv, at, akv[:, :L])
    wt, u0 = wu[:, :, :gw], wu[:, :, gw:]
    yield
    ry = hmm(a_rb, wt, u0)
    rh = rt + ry[:, :, :gw]
    y0 = ry[:, :, gw:] + akv[:, L:]

    bk_t = jnp.swapaxes(jnp.concatenate([bh, kh], axis=1), 1, 2)
    bk_pack = jnp.concatenate([bk_t[:, i * HEAD_DIM:(i + 1) * HEAD_DIM] for i in range(nheads)],
                              axis=2).astype(BF16)
    rhs = jnp.concatenate([wu, jnp.concatenate([jnp.zeros_like(v3), v3], axis=2)],
                          axis=1).astype(BF16)
    lane2 = lax.broadcasted_iota(jnp.int32, (1, 1, 2 * gw), 2) % gw // HEAD_DIM
    rhs_heads = jnp.concatenate([jnp.where(lane2 == i, rhs, 0.0) for i in range(nheads)], axis=1)
    mn = _bmm(bk_pack, rhs_heads, _NN)
    ri = lax.broadcasted_iota(jnp.int32, (1, HEAD_DIM, gw), 1)
    ci = lax.broadcasted_iota(jnp.int32, (1, HEAD_DIM, gw), 2) % HEAD_DIM
    m_mat = mn[:, :, :gw] + jnp.where(ri == ci, e_end, 0.0)
    yield
    for chain in chains:
        for _ in chain:
            pass
    lhs = jnp.concatenate([rh, m_mat], axis=1).astype(BF16)
    for d in range(2):
        part = slice(d * nch, (d + 1) * nch)
        scratch[d][5][...] = lhs[part]
        scratch[d][6][...] = mn[part, :, gw:]
        scratch[d][7][...] = y0[part]


def _wkv(p3, wa_col, mu, w0, a0, lora_w, k_k, k_a, r_k, gw, rws):
    bsz, seq, _ = p3.shape
    c = w0.shape[-1]
    ng = c // gw
    nstep = seq // rws
    L = WKV_CHUNK
    nch = rws // L
    wa_w = 2 * LORA
    t = np.arange(L)
    hd = np.arange(gw) // HEAD_DIM
    bd = jnp.asarray(hd[:, None] == hd[None, :], BF16)
    rows = [x for d in range(2) for x in (mu[d][:c], mu[d][c:2 * c], mu[d][2 * c:3 * c], w0[d], a0[d])]
    rows += [k_k, k_a, r_k]
    vecs = jnp.zeros((WKV_VEC_ROWS, c), F32).at[:len(rows)].set(jnp.stack(rows))
    muwa = jnp.zeros((V7X_SUBLANES, wa_w), F32).at[:2].set(jnp.stack([mu[0][3 * c:], mu[1][3 * c:]]))
    tri = jnp.asarray(np.stack([t[None, :] <= t[:, None], t[None, :] >= t[:, None]]), BF16)
    total = bsz * ng * nstep
    rd = lambda i: jnp.minimum(i, total - 1)
    wr = lambda i: jnp.maximum(i - 1, 0)
    batch_of = lambda i: i // (ng * nstep)
    group_of = lambda i: (i // nstep) % ng
    const = lambda a: pl.BlockSpec(a.shape, lambda i: (0,) * a.ndim)

    in_specs, args, out_specs, scratch = [], [], [], []
    for direction in range(2):
        blk = (lambda i: i % nstep) if direction == 0 else (lambda i: nstep - 1 - i % nstep)
        in_specs += [
            pl.BlockSpec((None, rws, 3 * gw),
                         lambda i, blk=blk: (batch_of(rd(i)), blk(rd(i)), group_of(rd(i)))),
            pl.BlockSpec((None, rws, wa_w),
                         lambda i, blk=blk: (batch_of(rd(i)), blk(rd(i)), wa_col // wa_w)),
        ]
        args += [p3, p3]
        out_specs += [pl.BlockSpec(
            (None, rws, gw),
            lambda i, blk=blk: (batch_of(wr(i)), blk(wr(i)), group_of(wr(i))))] * 2
        scratch += [pltpu.VMEM((HEAD_DIM, gw), F32), pltpu.VMEM((1, gw), F32), pltpu.VMEM((1, gw), F32),
                    pltpu.VMEM((1, gw), F32), pltpu.VMEM((1, wa_w), F32),
                    pltpu.VMEM((nch, L + HEAD_DIM, gw), BF16), pltpu.VMEM((nch, HEAD_DIM, gw), F32),
                    pltpu.VMEM((nch, L, gw), F32), pltpu.VMEM((rws, gw), F32)]
    assert len(scratch) == 2 * WKV_DIR_SCRATCH
    in_specs += [
        pl.BlockSpec((WKV_VEC_ROWS, gw), lambda i: (0, group_of(rd(i)))),
        const(muwa), const(tri),
        pl.BlockSpec((2, None, wa_w, 2 * gw), lambda i: (0, group_of(rd(i)), 0, 0)),
        const(bd),
    ]
    args += [vecs, muwa, tri, lora_w, bd]
    return pl.pallas_call(
        functools.partial(_wkv_kernel, nstep),
        grid=(total + 1,),
        in_specs=in_specs,
        out_specs=out_specs,
        out_shape=[jax.ShapeDtypeStruct((bsz, seq, c), BF16)] * 4,
        scratch_shapes=scratch,
        compiler_params=_cparams(("arbitrary",)),
        name="wkv",
    )(*args)


def _lora_weights(w2, a2, gw):
    nd, lo, c = w2.shape
    ng = c // gw
    w2g = w2.reshape(nd, lo, ng, gw).transpose(0, 2, 1, 3)
    a2g = a2.reshape(nd, lo, ng, gw).transpose(0, 2, 1, 3)
    z = jnp.zeros_like(w2g)
    top = jnp.concatenate([w2g, z], axis=3)
    bot = jnp.concatenate([z, a2g], axis=3)
    return jnp.concatenate([top, bot], axis=2).astype(BF16)


FNET_N2 = 128


FNET_PITCH_PAD = 8
FNET_UNROLL = 8


def _cos_sin(idx, period):
    ang = 2.0 * np.pi * (idx % period) / period
    return np.cos(ang), np.sin(ang)


def _fnet_weight_kernel(scale, c_ref, s_ref, w_ref, o_ref):
    for g in range(w_ref.shape[0]):
        wc = _dot3(c_ref[...], w_ref[g])
        ws = _dot3(s_ref[...], w_ref[g])
        o_ref[g] = jnp.concatenate([wc, -ws], axis=1) * scale


def _fnet_kernel(n1, n2, f_ref, w_ref, m1_ref, m2_ref, o_ref, gr_ref, gi_ref, zr_ref, zi_ref):
    gd = f_ref.shape[1]
    pitch = n2 + FNET_PITCH_PAD
    w = w_ref[...].astype(BF16)

    def chan_body(s1, carry):
        src = pl.ds(pl.multiple_of(s1 * n2, n2), n2)
        dst = pl.ds(pl.multiple_of(s1 * pitch, V7X_SUBLANES), n2)
        z = jnp.dot(f_ref[src, :].astype(BF16), w, preferred_element_type=F32)
        gr_ref[dst, :] = z[:, :gd]
        gi_ref[dst, :] = z[:, gd:]
        return carry

    lax.fori_loop(0, n1, chan_body, 0, unroll=min(FNET_UNROLL, n1))

    m1 = m1_ref[...].astype(BF16)

    def stage1_body(s2, carry):
        idx = pl.ds(s2, n1, stride=pitch)
        x = jnp.concatenate([gr_ref[idx, :], gi_ref[idx, :]], axis=0)
        z = jnp.dot(m1, x.astype(BF16), preferred_element_type=F32)
        zr_ref[idx, :] = z[:n1]
        zi_ref[idx, :] = z[n1:]
        return carry

    lax.fori_loop(0, n2, stage1_body, 0, unroll=2 * FNET_UNROLL)

    def stage2_body(s1, carry):
        src = pl.ds(pl.multiple_of(s1 * pitch, V7X_SUBLANES), n2)
        zz = jnp.concatenate([zr_ref[src, :], zi_ref[src, :]], axis=0)
        gr_ref[src, :] = jnp.dot(m2_ref[s1].astype(BF16), zz.astype(BF16),
                                 preferred_element_type=F32)
        return carry

    lax.fori_loop(0, n1, stage2_body, 0, unroll=min(FNET_UNROLL, n1))

    def reorder_body(s2, carry):
        dst = pl.ds(pl.multiple_of(s2 * n1, V7X_SUBLANES), n1)
        o_ref[dst, :] = gr_ref[pl.ds(s2, n1, stride=pitch), :]
        return carry

    lax.fori_loop(0, n2, reorder_body, 0, unroll=2 * FNET_UNROLL)


def _fnet(p3, col0, w_f):
    bsz, seq, _ = p3.shape
    ng, gd, _ = w_f.shape
    n2 = FNET_N2
    n1 = seq // n2
    scale = 1.0 / math.sqrt(seq * gd)
    ar = np.arange
    cd, sd = _cos_sin(ar(gd)[:, None] * ar(gd)[None, :], gd)
    c1, s1 = _cos_sin(ar(n1)[:, None] * ar(n1)[None, :], n1)
    m1 = np.block([[c1, s1], [-s1, c1]])
    c2, s2 = _cos_sin(ar(n2)[None, None, :] * (ar(n1)[:, None, None] + n1 * ar(n2)[None, :, None]), seq)
    m1 = jnp.asarray(m1, F32)
    m2 = jnp.asarray(np.concatenate([c2, s2], axis=2), F32)

    wcat = pl.pallas_call(
        functools.partial(_fnet_weight_kernel, scale),
        out_shape=jax.ShapeDtypeStruct((ng, gd, 2 * gd), F32),
        name="fnet_weights",
    )(jnp.asarray(cd, F32), jnp.asarray(sd, F32), w_f)

    pitch_rows = n1 * (n2 + FNET_PITCH_PAD)
    const2 = lambda a: pl.BlockSpec(a.shape, lambda b, g: (0, 0))
    const3 = lambda a: pl.BlockSpec(a.shape, lambda b, g: (0, 0, 0), pipeline_mode=pl.Buffered(1))
    return pl.pallas_call(
        functools.partial(_fnet_kernel, n1, n2),
        grid=(bsz, ng),
        in_specs=[pl.BlockSpec((None, seq, gd), lambda b, g: (b, 0, col0 // gd + g)),
                  pl.BlockSpec((None, gd, 2 * gd), lambda b, g: (g, 0, 0)),
                  const2(m1), const3(m2)],
        out_specs=pl.BlockSpec((None, seq, gd), lambda b, g: (b, 0, g)),
        out_shape=jax.ShapeDtypeStruct((bsz, seq, ng * gd), F32),
        scratch_shapes=[pltpu.VMEM((pitch_rows, gd), F32)] * 4,
        compiler_params=_cparams(("parallel", "parallel")),
        name="fnet",
    )(p3, wcat, m1, m2)


def _odd_out_kernel(y0_ref, y1_ref, b0_ref, b1_ref, zc_ref, fn_ref, zd_ref, h_ref, lng_ref,
                    lnb_ref, bd_ref, wo_ref, fg_ref, o_ref):
    c = y0_ref.shape[1]
    bd = bd_ref[...]
    inv_n = 1.0 / HEAD_DIM
    ysum = y0_ref[...].astype(F32) + y1_ref[...].astype(F32)
    mean = _segsum(ysum, bd, 2) * inv_n
    cen = ysum - mean
    var = _segsum(cen * cen, bd, 1) * inv_n
    gn = cen * lax.rsqrt(var + GN_EPS) * lng_ref[...] + lnb_ref[...]
    yc = (gn + b0_ref[...].astype(F32) + b1_ref[...].astype(F32)) * _silu(zc_ref[...])
    yd = fn_ref[...] * _silu(zd_ref[...])
    out = h_ref[...]
    out = out + jnp.dot(yc.astype(BF16), wo_ref[0:c, :], preferred_element_type=F32)
    out = out + jnp.dot(yd.astype(BF16), wo_ref[c:, :], preferred_element_type=F32)
    ms = jnp.mean(out * out, axis=-1, keepdims=True)
    o_ref[...] = out * lax.rsqrt(ms + RMS_EPS) * fg_ref[...]


def _odd_out(y0, y1, b0, b1, p2d, zc_blk, fn, zd_blk, h2d, lnx_g, lnx_b, w_out_bf16, final_g, tm):
    t, d = h2d.shape
    c = y0.shape[1]
    cf = fn.shape[1]
    hd = np.arange(2 * HEAD_DIM) // HEAD_DIM
    bd = jnp.asarray(hd[:, None] == hd[None, :], BF16)
    rowblk = lambda w: pl.BlockSpec((tm, w), lambda i: (i, 0))
    full = lambda shape: pl.BlockSpec(shape, lambda i: (0, 0))
    return pl.pallas_call(
        _odd_out_kernel,
        grid=(t // tm,),
        in_specs=[rowblk(c), rowblk(c), rowblk(c), rowblk(c),
                  pl.BlockSpec((tm, c), lambda i: (i, zc_blk)),
                  rowblk(cf),
                  pl.BlockSpec((tm, cf), lambda i: (i, zd_blk)),
                  rowblk(d), full((1, c)), full((1, c)), full(bd.shape),
                  full(w_out_bf16.shape), full((1, d))],
        out_specs=rowblk(d),
        out_shape=jax.ShapeDtypeStruct((t, d), F32),
        compiler_params=_cparams(("parallel",)),
        name="odd_out",
    )(y0, y1, b0, b1, p2d, fn, p2d, h2d, lnx_g.reshape(1, c), lnx_b.reshape(1, c), bd,
      w_out_bf16, final_g.reshape(1, d))


WKV_LANES = 128
WKV_ROWS = 512
ODD_PROJ_ROWS = 512
ODD_OUT_ROWS = 512


def _odd_layer(h2d, bsz, seq, norm_g, w_in, mu, w0, w2, a0, a2, k_k, k_a, r_k, lnx_g, lnx_b,
               fnet_w, w_out, final_g):
    t, d = h2d.shape
    c = w0.shape[-1]
    rs = 3 * c + 2 * LORA
    cf = fnet_w.shape[0] * fnet_w.shape[1]
    gw = WKV_LANES
    rkv_cols = [w_in[:, j * c + g * gw:j * c + (g + 1) * gw] for g in range(c // gw) for j in range(3)]
    w_perm = jnp.concatenate(rkv_cols + [w_in[:, rs:], w_in[:, 3 * c:rs]], axis=1).astype(BF16)
    p2 = _norm_proj(h2d, norm_g, w_perm, tm=min(ODD_PROJ_ROWS, t))
    zc_col, fd_col, zd_col, wa_col = 3 * c, 4 * c, 4 * c + cf, 4 * c + 2 * cf
    p3 = p2.reshape(bsz, seq, -1)
    lora = _lora_weights(w2, a2, gw)
    rws = min(WKV_ROWS, seq)
    y0, b0, y1, b1 = _wkv(p3, wa_col, mu, w0, a0, lora, k_k, k_a, r_k.reshape(-1), gw, rws)
    fn = _fnet(p3, fd_col, fnet_w).reshape(t, cf)
    flat = lambda a: a.reshape(t, c)
    return _odd_out(flat(y0), flat(y1), flat(b0), flat(b1), p2, zc_col // c, fn, zd_col // cf, h2d,
                    lnx_g, lnx_b, w_out.astype(BF16), final_g, tm=min(ODD_OUT_ROWS, t))


def kernel(x, e_norm_g, e_w_in, e_conv_w, e_sgu_ln_g, e_sgu_ln_b, e_sgu_w, e_sgu_b, e_w_out,
           o_norm_g, o_w_in, o_mu, o_w0, o_w2, o_a0, o_a2, o_k_k, o_k_a, o_r_k, o_lnx_g, o_lnx_b,
           o_fnet_w, o_w_out, final_norm_g):
    bsz, seq, d = x.shape
    assert e_norm_g.shape[0] == 1 and o_norm_g.shape[0] == 1, "two-layer trunk: one even, one odd layer"
    h = x.reshape(bsz * seq, d)
    h = _even_layer(h, seq, e_norm_g[0], e_w_in[0], e_conv_w[0], e_sgu_ln_g[0], e_sgu_ln_b[0],
                    e_sgu_w[0], e_sgu_b[0], e_w_out[0])
    out = _odd_layer(h, bsz, seq, o_norm_g[0], o_w_in[0], o_mu[0], o_w0[0], o_w2[0], o_a0[0],
                     o_a2[0], o_k_k[0], o_k_a[0], o_r_k[0], o_lnx_g[0], o_lnx_b[0], o_fnet_w[0],
                     o_w_out[0], final_norm_g)
    return out.reshape(bsz, seq, d)
```

```python
import functools
import math

import numpy as np
import jax
import jax.numpy as jnp
from jax import lax
from jax.experimental import pallas as pl
from jax.experimental.pallas import tpu as pltpu

F32 = jnp.float32
BF16 = jnp.bfloat16

RMS_EPS = 1e-6
SGU_LN_EPS = 1e-5
GN_EPS = 64e-5
SGU_CHUNK = 128
SGU_GROUPS = 8
HEAD_DIM = 64
LORA = 64

V7X_SUBLANES = 8
V7X_VMEM_BYTES = 64 * 1024 * 1024
VMEM_LIMIT = V7X_VMEM_BYTES - 8 * 1024 * 1024


def _cparams(sem):
    return pltpu.CompilerParams(dimension_semantics=sem, vmem_limit_bytes=VMEM_LIMIT)


def _silu(z):
    return z * (1.0 / (1.0 + jnp.exp(-z)))


def _sigmoid(z):
    return 1.0 / (1.0 + jnp.exp(-z))


def _dot(a, b):
    return jnp.dot(a.astype(BF16), b.astype(BF16), preferred_element_type=F32)


def _split(a, n):
    parts = []
    rem = a
    for _ in range(n):
        p = rem.astype(BF16)
        parts.append(p)
        rem = rem - p.astype(F32)
    return parts


def _dot_exact_rhs(a, b_bf16, n=3):
    acc = None
    for p in _split(a, n):
        t = jnp.dot(p, b_bf16, preferred_element_type=F32)
        acc = t if acc is None else acc + t
    return acc


def _proj_kernel(x_ref, g_ref, w_ref, o_ref):
    x = x_ref[...]
    ms = jnp.mean(x * x, axis=-1, keepdims=True)
    hn = (x * lax.rsqrt(ms + RMS_EPS) * g_ref[...]).astype(BF16)
    o_ref[...] = jnp.dot(hn, w_ref[...], preferred_element_type=F32)


def _norm_proj(x2d, g, w_bf16, tm):
    t, d = x2d.shape
    n = w_bf16.shape[1]
    assert t % tm == 0
    return pl.pallas_call(
        _proj_kernel,
        grid=(t // tm,),
        in_specs=[
            pl.BlockSpec((tm, d), lambda i: (i, 0)),
            pl.BlockSpec((1, d), lambda i: (0, 0)),
            pl.BlockSpec((d, n), lambda i: (0, 0), pipeline_mode=pl.Buffered(1)),
        ],
        out_specs=pl.BlockSpec((tm, n), lambda i: (i, 0)),
        out_shape=jax.ShapeDtypeStruct((t, n), F32),
        compiler_params=_cparams(("parallel",)),
        name="norm_proj",
    )(x2d, g.reshape(1, d), w_bf16)


def _even_kernel(seq, h_ref, hp_ref, hn_ref, ng_ref, wi_ref, cw_ref, lng_ref, lnb_ref, sw_ref,
                 sb_ref, wo_ref, o_ref):
    tm, d = h_ref.shape
    hr = hp_ref.shape[0]
    row0 = pl.program_id(0) * tm
    at_seq_start = (row0 % seq) == 0
    at_seq_end = ((row0 + tm) % seq) == 0

    h = h_ref[...]
    hx = jnp.concatenate([hp_ref[...], h, hn_ref[...]], axis=0)
    ms = jnp.mean(hx * hx, axis=-1, keepdims=True)
    hx = (hx * lax.rsqrt(ms + RMS_EPS) * ng_ref[...]).astype(BF16)
    hn = hx[hr:hr + tm]
    proj = lambda lhs, j: jnp.dot(lhs, wi_ref[:, j * d:(j + 1) * d], preferred_element_type=F32)

    xce = proj(hx, 0) * proj(hx, 2)
    xc = xce[hr:hr + tm]
    halo_prev = xce[hr - 1:hr]
    halo_next = xce[hr + tm:hr + tm + 1]
    ba, za = proj(hn, 1), proj(hn, 3)
    halo_prev = jnp.where(at_seq_start, 0.0, halo_prev)
    halo_next = jnp.where(at_seq_end, 0.0, halo_next)
    rows = lax.broadcasted_iota(jnp.int32, (tm, 1), 0)
    prev = jnp.where(rows == 0, halo_prev, pltpu.roll(xc, 1, 0))
    nxt = jnp.where(rows == tm - 1, halo_next, pltpu.roll(xc, tm - 1, 0))
    conv = cw_ref[0:1, :] * prev + cw_ref[1:2, :] * xc + cw_ref[2:3, :] * nxt
    ya = ba * conv * _silu(za)

    vb = proj(hn, 5)
    mu = jnp.mean(vb, axis=-1, keepdims=True)
    cen = vb - mu
    var = jnp.mean(cen * cen, axis=-1, keepdims=True)
    vn = (cen * lax.rsqrt(var + SGU_LN_EPS) * lng_ref[...] + lnb_ref[...]).astype(BF16)
    gw = d // SGU_GROUPS
    chunk_rows = []
    for n in range(tm // SGU_CHUNK):
        r0 = n * SGU_CHUNK
        cols = []
        for g in range(SGU_GROUPS):
            cols.append(jnp.dot(sw_ref[g], vn[r0:r0 + SGU_CHUNK, g * gw:(g + 1) * gw],
                                preferred_element_type=F32))
        chunk_rows.append(jnp.concatenate(cols, axis=1) + sb_ref[...])
    mixed = jnp.concatenate(chunk_rows, axis=0)
    yb = proj(hn, 4) * mixed * _silu(proj(hn, 6))

    out = h
    out = out + jnp.dot(ya.astype(BF16), wo_ref[0:d, :], preferred_element_type=F32)
    out = out + jnp.dot(yb.astype(BF16), wo_ref[d:2 * d, :], preferred_element_type=F32)
    o_ref[...] = out


EVEN_ROWS = 512


def _even_layer(h2d, seq, norm_g, w_in, conv_w, ln_g, ln_b, sgu_w, sgu_b, w_out):
    t, d = h2d.shape
    tm = min(EVEN_ROWS, t)
    hr = V7X_SUBLANES
    nblk = t // hr
    rpt = tm // hr
    bias_full = jnp.repeat(sgu_b.T, d // SGU_GROUPS, axis=1)
    sgu_w_bf16 = sgu_w.astype(BF16)
    w_in_bf16 = w_in.astype(BF16)
    w_out_bf16 = w_out.astype(BF16)
    const = lambda a: pl.BlockSpec(a.shape, lambda i: (0,) * a.ndim, pipeline_mode=pl.Buffered(1))
    row = lambda a: a.reshape(1, d)
    consts = (row(norm_g), w_in_bf16, conv_w, row(ln_g), row(ln_b), sgu_w_bf16, bias_full, w_out_bf16)
    return pl.pallas_call(
        functools.partial(_even_kernel, seq),
        grid=(t // tm,),
        in_specs=[
            pl.BlockSpec((tm, d), lambda i: (i, 0)),
            pl.BlockSpec((hr, d), lambda i: (jnp.maximum(i * rpt - 1, 0), 0)),
            pl.BlockSpec((hr, d), lambda i: (jnp.minimum((i + 1) * rpt, nblk - 1), 0)),
        ] + [const(a) for a in consts],
        out_specs=pl.BlockSpec((tm, d), lambda i: (i, 0)),
        out_shape=jax.ShapeDtypeStruct((t, d), F32),
        compiler_params=_cparams(("parallel",)),
        name="even_layer",
    )(h2d, h2d, h2d, *consts)


def _dot3(a, b):
    ah, al = _split(a, 2)
    bh, bl = _split(b, 2)
    d = lambda x, y: jnp.dot(x, y, preferred_element_type=F32)
    return d(ah, bh) + (d(ah, bl) + d(al, bh))


_NN = (((2,), (1,)), ((0,), (0,)))
_NT = (((2,), (2,)), ((0,), (0,)))


def _bmm(a, b, dims):
    return lax.dot_general(a.astype(BF16), b.astype(BF16), dims, preferred_element_type=F32)


def _segsum(x, bd_bf16, n=3):
    w = bd_bf16.shape[0]
    cols = [_dot_exact_rhs(x[:, j:j + w], bd_bf16, n) for j in range(0, x.shape[1], w)]
    return cols[0] if len(cols) == 1 else jnp.concatenate(cols, axis=1)


WKV_CHUNK = 64


WKV_DIR_SCRATCH = 9
WKV_VEC_PER_DIR = 5
WKV_VEC_SHARED = 2 * WKV_VEC_PER_DIR
WKV_VEC_ROWS = 16


def _wkv_kernel(nstep, *refs):
    ns = WKV_DIR_SCRATCH
    streams = refs[0:4]
    vec_ref, muwa_ref, tri_ref, lora_ref, bd_ref = refs[4:9]
    outs = refs[9:13]
    scratch = refs[13:]
    gw = bd_ref.shape[0]
    vec = lambda r: vec_ref.at[pl.ds(r, 1)]
    shared = (vec(WKV_VEC_SHARED), vec(WKV_VEC_SHARED + 1), vec(WKV_VEC_SHARED + 2), bd_ref)

    def dir_inputs(d):
        rkv_ref, wa_ref = streams[2 * d:2 * d + 2]
        r, k, v = (rkv_ref.at[:, pl.ds(j * gw, gw)] for j in range(3))
        base = d * WKV_VEC_PER_DIR
        return (r, k, v, wa_ref, vec(base), vec(base + 1), vec(base + 2), muwa_ref.at[pl.ds(d, 1)],
                vec(base + 3), vec(base + 4), lora_ref.at[d], tri_ref.at[d])

    step = pl.program_id(0)
    last = pl.num_programs(0) - 1
    chain_start = (step % nstep) == 0

    @pl.when(step == 0)
    def _():
        for ref in scratch:
            ref[...] = jnp.zeros_like(ref)

    def body(with_local):
        per_dir = [_wkv_direction(d, dir_inputs(d), shared, outs[2 * d:2 * d + 2],
                                  scratch[d * ns:(d + 1) * ns], with_local, chain_start)
                   for d in range(2)]
        chains = [_wkv_joint_recurrence(outs, [scratch[d * ns:(d + 1) * ns] for d in range(2)],
                                        chain_start)]
        if with_local:
            local = _wkv_local_stages([e for _, e in per_dir], [dir_inputs(d) for d in range(2)],
                                      shared, [scratch[d * ns:(d + 1) * ns] for d in range(2)], chains)
            for _ in local:
                for chain in chains:
                    next(chain, None)
        for chain in chains:
            for _ in chain:
                pass

    pl.when(step < last)(functools.partial(body, True))
    pl.when(step == last)(functools.partial(body, False))


def _wkv_joint_recurrence(outs, scratch, chain_start):
    L = WKV_CHUNK
    h_refs = [scratch[d][0] for d in range(2)]
    lhs_refs = [scratch[d][5] for d in range(2)]
    n_refs = [scratch[d][6] for d in range(2)]
    y0_refs = [scratch[d][7] for d in range(2)]
    nch = lhs_refs[0].shape[0]
    gw = h_refs[0].shape[1]
    lane_head = lax.broadcasted_iota(jnp.int32, (1, 1, gw), 2) // HEAD_DIM
    h = jnp.stack([h_refs[0][...], h_refs[1][...]])
    ys = [[None] * nch, [None] * nch]
    for i in range(nch):
        cs = (i, nch - 1 - i)
        hb = h.astype(BF16)
        hbd = jnp.concatenate([jnp.where(lane_head == j, hb, 0.0) for j in range(gw // HEAD_DIM)], axis=1)
        lhs = jnp.stack([lhs_refs[d][cs[d]] for d in range(2)])
        both = lax.dot_general(lhs, hbd, _NN, preferred_element_type=F32)
        for d in range(2):
            ys[d][cs[d]] = both[d, :L] + y0_refs[d][cs[d]]
        h = both[:, L:] + jnp.stack([n_refs[d][cs[d]] for d in range(2)])
        yield
    for d in range(2):
        h_refs[d][...] = jnp.where(chain_start, 0.0, h[d])
        outs[2 * d][...] = jnp.concatenate(ys[d], axis=0).astype(outs[2 * d].dtype)


def _wkv_direction(direction, ins, shared, outs, scratch, with_local, chain_start):
    (r_ref, k_ref, v_ref, wa_ref, mur_ref, muk_ref, muv_ref, muwa_ref, w0_ref, a0_ref, lora_ref,
     tri_ref) = ins
    kk_ref, ka_ref, rk_ref, bd_ref = shared
    y_ref, bon_ref = outs
    h_ref, cr_ref, ck_ref, cv_ref, cwa_ref, lhs_ref, n_ref, y0_ref, bonp_ref = scratch
    rws, gw = r_ref.shape
    L = WKV_CHUNK
    nch = rws // L

    lane_head = lax.broadcasted_iota(jnp.int32, (1, gw), 1) // HEAD_DIM

    def recurrence():
        h = h_ref[...]
        ys = [None] * nch
        for c in (range(nch) if direction == 0 else range(nch - 1, -1, -1)):
            hb = h.astype(BF16)
            hbd = jnp.concatenate([jnp.where(lane_head == i, hb, 0.0)
                                   for i in range(gw // HEAD_DIM)], axis=0)
            both = jnp.dot(lhs_ref[c], hbd, preferred_element_type=F32)
            ys[c] = both[:L] + y0_ref[c]
            h = both[L:] + n_ref[c]
            yield
        h_ref[...] = jnp.where(chain_start, 0.0, h)
        y_ref[...] = jnp.concatenate(ys, axis=0).astype(y_ref.dtype)

    chain = None
    bon_ref[...] = bonp_ref[...].astype(bon_ref.dtype)
    if not with_local:
        return chain, None

    streams = ((r_ref, cr_ref), (k_ref, ck_ref), (v_ref, cv_ref), (wa_ref, cwa_ref))
    edge = [jnp.where(chain_start, 0.0, c_ref[...]) for _, c_ref in streams]
    far = rws - 1 if direction == 0 else 0
    for x_ref, c_ref in streams:
        c_ref[...] = x_ref[far:far + 1, :]

    return chain, edge


def _wkv_local_stages(edges, ins, shared, scratch, chains):
    kk_ref, ka_ref, rk_ref, bd_ref = shared
    L = WKV_CHUNK
    rws, gw = ins[0][0].shape
    nch = rws // L
    nheads = gw // HEAD_DIM
    rows = lax.broadcasted_iota(jnp.int32, (rws, 1), 0)
    bd = bd_ref[...]
    to3 = lambda x: x.reshape(nch, L, gw)
    both_dirs = lambda f: jnp.concatenate([f(d) for d in range(2)], axis=0)

    def token_shift(d, x_ref, edge_row, mu):
        x = x_ref[...]
        if d == 0:
            sh = jnp.where(rows == 0, edge_row, pltpu.roll(x, 1, 0))
        else:
            sh = jnp.where(rows == rws - 1, edge_row, pltpu.roll(x, rws - 1, 0))
        return x + mu * (sh - x)

    rkvw = []
    for d in range(2):
        r_ref, k_ref, v_ref, wa_ref, mur_ref, muk_ref, muv_ref, muwa_ref = ins[d][:8]
        rkvw.append((token_shift(d, r_ref, edges[d][0], mur_ref[...]),
                     token_shift(d, k_ref, edges[d][1], muk_ref[...]),
                     token_shift(d, v_ref, edges[d][2], muv_ref[...]),
                     token_shift(d, wa_ref, edges[d][3], muwa_ref[...])))
    yield

    prep = []
    for d in range(2):
        w0_ref, a0_ref, lora_ref = ins[d][8:11]
        r, k, v, wa = rkvw[d]
        lane_wa = lax.broadcasted_iota(jnp.int32, (1, wa.shape[1]), 1)
        wa = jnp.where(lane_wa < LORA, jnp.tanh(wa), wa)
        za = _dot(wa, lora_ref[...])
        zw = w0_ref[...] + za[:, :gw]
        a = _sigmoid(a0_ref[...] + za[:, gw:])
        lw = -math.exp(-0.5) * _sigmoid(zw)
        kk = k * kk_ref[...]
        kk = kk * lax.rsqrt(jnp.maximum(_segsum(kk * kk, bd, 1), 1e-12))
        k2 = k * (1.0 + (a - 1.0) * ka_ref[...])
        scratch[d][8][...] = _segsum(r * k2 * rk_ref[...], bd, 1) * v
        prep.append((r, k2, v, kk, a, lw))
    yield

    tri2 = both_dirs(lambda d: jnp.broadcast_to(jnp.concatenate([ins[d][11][...]] * 2, axis=1),
                                                (nch, L, 2 * L)))
    parts = both_dirs(lambda d: jnp.concatenate([p.reshape(nch, L, gw)
                                                 for p in _split(prep[d][5], 2)], axis=1))
    cs = lax.dot_general(tri2, parts, _NN, preferred_element_type=F32)
    yield
    r, k2, v3, kk, a, lw = (both_dirs(lambda d, j=j: to3(prep[d][j])) for j in range(6))
    e_inc = jnp.exp(cs)
    e_inv = jnp.exp(-cs)
    e_exc = jnp.exp(cs - lw)
    rt = r * e_inc
    kt = k2 * e_inv
    at = -kk * e_exc
    bt = kk * a * e_inv
    e_end = jnp.concatenate([e_inc[:nch, L - 1:L], e_inc[nch:, 0:1]], axis=0)
    kh = kt * e_end
    bh = bt * e_end

    lane = lax.broadcasted_iota(jnp.int32, (1, 1, gw), 2)
    head_masks = [(lane // HEAD_DIM) == h for h in range(nheads)]

    def blockdiag(x3):
        return jnp.concatenate([jnp.where(m, x3, 0.0) for m in head_masks], axis=1)

    def hmm(lp, *xps):
        rhs = [blockdiag(xp.astype(BF16)) for xp in xps]
        rhs = rhs[0] if len(rhs) == 1 else jnp.concatenate(rhs, axis=2)
        return _bmm(lp, rhs, _NN)

    x_ar = jnp.concatenate([at, rt], axis=1)
    g = _bmm(x_ar, jnp.concatenate([blockdiag(bt.astype(BF16)), blockdiag(kt.astype(BF16))], axis=1),
             _NT)
    g_b, g_k = g[:, :, :gw], g[:, :, gw:]
    yield

    t_idx = lax.broadcasted_iota(jnp.int32, (1, L, gw), 1)
    s_idx = lax.broadcasted_iota(jnp.int32, (1, L, gw), 2) % L

    def keep(x, fwd_mask, bwd_mask):
        return jnp.concatenate([jnp.where(fwd_mask, x[:nch], 0.0), jnp.where(bwd_mask, x[nch:], 0.0)],
                               axis=0)

    a_ab = keep(g_b[:, :L], s_idx < t_idx, s_idx > t_idx)
    a_rb = keep(g_b[:, L:], s_idx <= t_idx, s_idx >= t_idx)
    a_ak = keep(g_k[:, :L], s_idx < t_idx, s_idx > t_idx)
    a_rk = keep(g_k[:, L:], s_idx <= t_idx, s_idx >= t_idx)

    tinv = jnp.where(s_idx == t_idx, 1.0, 0.0) + a_ab
    apow = hmm(a_ab, a_ab)
    nsq = int(math.log2(L)) - 1
    for i in range(nsq):
        yield
        if i + 1 < nsq:
            both = hmm(apow, apow, tinv)
            apow, tinv = both[:, :, :gw], tinv + both[:, :, gw:]
        else:
            tinv = tinv + hmm(apow, tinv)

    akv = _bmm(jnp.concatenate([a_ak, a_rk], axis=1), blockdiag(v3.astype(BF16)), _NN)
    wu = hmm(tinv, at, akv[:, :L])
    wt, u0 = wu[:, :, :gw], wu[:, :, gw:]
    yield
    ry = hmm(a_rb, wt, u0)
    rh = rt + ry[:, :, :gw]
    y0 = ry[:, :, gw:] + akv[:, L:]

    bk_t = jnp.swapaxes(jnp.concatenate([bh, kh], axis=1), 1, 2)
    bk_pack = jnp.concatenate([bk_t[:, i * HEAD_DIM:(i + 1) * HEAD_DIM] for i in range(nheads)],
                              axis=2).astype(BF16)
    rhs = jnp.concatenate([wu, jnp.concatenate([jnp.zeros_like(v3), v3], axis=2)],
                          axis=1).astype(BF16)
    lane2 = lax.broadcasted_iota(jnp.int32, (1, 1, 2 * gw), 2) % gw // HEAD_DIM
    rhs_heads = jnp.concatenate([jnp.where(lane2 == i, rhs, 0.0) for i in range(nheads)], axis=1)
    mn = _bmm(bk_pack, rhs_heads, _NN)
    ri = lax.broadcasted_iota(jnp.int32, (1, HEAD_DIM, gw), 1)
    ci = lax.broadcasted_iota(jnp.int32, (1, HEAD_DIM, gw), 2) % HEAD_DIM
    m_mat = mn[:, :, :gw] + jnp.where(ri == ci, e_end, 0.0)
    yield
    for chain in chains:
        for _ in chain:
            pass
    lhs = jnp.concatenate([rh, m_mat], axis=1).astype(BF16)
    for d in range(2):
        part = slice(d * nch, (d + 1) * nch)
        scratch[d][5][...] = lhs[part]
        scratch[d][6][...] = mn[part, :, gw:]
        scratch[d][7][...] = y0[part]


def _wkv(p3, wa_col, mu, w0, a0, lora_w, k_k, k_a, r_k, gw, rws):
    bsz, seq, _ = p3.shape
    c = w0.shape[-1]
    ng = c // gw
    nstep = seq // rws
    L = WKV_CHUNK
    nch = rws // L
    wa_w = 2 * LORA
    t = np.arange(L)
    hd = np.arange(gw) // HEAD_DIM
    bd = jnp.asarray(hd[:, None] == hd[None, :], BF16)
    rows = [x for d in range(2) for x in (mu[d][:c], mu[d][c:2 * c], mu[d][2 * c:3 * c], w0[d], a0[d])]
    rows += [k_k, k_a, r_k]
    vecs = jnp.zeros((WKV_VEC_ROWS, c), F32).at[:len(rows)].set(jnp.stack(rows))
    muwa = jnp.zeros((V7X_SUBLANES, wa_w), F32).at[:2].set(jnp.stack([mu[0][3 * c:], mu[1][3 * c:]]))
    tri = jnp.asarray(np.stack([t[None, :] <= t[:, None], t[None, :] >= t[:, None]]), BF16)
    total = bsz * ng * nstep
    rd = lambda i: jnp.minimum(i, total - 1)
    wr = lambda i: jnp.maximum(i - 1, 0)
    batch_of = lambda i: i // (ng * nstep)
    group_of = lambda i: (i // nstep) % ng
    const = lambda a: pl.BlockSpec(a.shape, lambda i: (0,) * a.ndim)

    in_specs, args, out_specs, scratch = [], [], [], []
    for direction in range(2):
        blk = (lambda i: i % nstep) if direction == 0 else (lambda i: nstep - 1 - i % nstep)
        in_specs += [
            pl.BlockSpec((None, rws, 3 * gw),
                         lambda i, blk=blk: (batch_of(rd(i)), blk(rd(i)), group_of(rd(i)))),
            pl.BlockSpec((None, rws, wa_w),
                         lambda i, blk=blk: (batch_of(rd(i)), blk(rd(i)), wa_col // wa_w)),
        ]
        args += [p3, p3]
        out_specs += [pl.BlockSpec(
            (None, rws, gw),
            lambda i, blk=blk: (batch_of(wr(i)), blk(wr(i)), group_of(wr(i))))] * 2
        scratch += [pltpu.VMEM((HEAD_DIM, gw), F32), pltpu.VMEM((1, gw), F32), pltpu.VMEM((1, gw), F32),
                    pltpu.VMEM((1, gw), F32), pltpu.VMEM((1, wa_w), F32),
                    pltpu.VMEM((nch, L + HEAD_DIM, gw), BF16), pltpu.VMEM((nch, HEAD_DIM, gw), F32),
                    pltpu.VMEM((nch, L, gw), F32), pltpu.VMEM((rws, gw), F32)]
    assert len(scratch) == 2 * WKV_DIR_SCRATCH
    in_specs += [
        pl.BlockSpec((WKV_VEC_ROWS, gw), lambda i: (0, group_of(rd(i)))),
        const(muwa), const(tri),
        pl.BlockSpec((2, None, wa_w, 2 * gw), lambda i: (0, group_of(rd(i)), 0, 0)),
        const(bd),
    ]
    args += [vecs, muwa, tri, lora_w, bd]
    return pl.pallas_call(
        functools.partial(_wkv_kernel, nstep),
        grid=(total + 1,),
        in_specs=in_specs,
        out_specs=out_specs,
        out_shape=[jax.ShapeDtypeStruct((bsz, seq, c), BF16)] * 4,
        scratch_shapes=scratch,
        compiler_params=_cparams(("arbitrary",)),
        name="wkv",
    )(*args)


def _lora_weights(w2, a2, gw):
    nd, lo, c = w2.shape
    ng = c // gw
    w2g = w2.reshape(nd, lo, ng, gw).transpose(0, 2, 1, 3)
    a2g = a2.reshape(nd, lo, ng, gw).transpose(0, 2, 1, 3)
    z = jnp.zeros_like(w2g)
    top = jnp.concatenate([w2g, z], axis=3)
    bot = jnp.concatenate([z, a2g], axis=3)
    return jnp.concatenate([top, bot], axis=2).astype(BF16)


FNET_N2 = 128


FNET_PITCH_PAD = 8
FNET_UNROLL = 8


def _cos_sin(idx, period):
    ang = 2.0 * np.pi * (idx % period) / period
    return np.cos(ang), np.sin(ang)


def _fnet_weight_kernel(scale, c_ref, s_ref, w_ref, o_ref):
    for g in range(w_ref.shape[0]):
        wc = _dot3(c_ref[...], w_ref[g])
        ws = _dot3(s_ref[...], w_ref[g])
        o_ref[g] = jnp.concatenate([wc, -ws], axis=1) * scale


def _fnet_kernel(n1, n2, f_ref, w_ref, m1_ref, m2_ref, o_ref, gr_ref, gi_ref, zr_ref, zi_ref):
    gd = f_ref.shape[1]
    pitch = n2 + FNET_PITCH_PAD
    w = w_ref[...].astype(BF16)

    def chan_body(s1, carry):
        src = pl.ds(pl.multiple_of(s1 * n2, n2), n2)
        dst = pl.ds(pl.multiple_of(s1 * pitch, V7X_SUBLANES), n2)
        z = jnp.dot(f_ref[src, :].astype(BF16), w, preferred_element_type=F32)
        gr_ref[dst, :] = z[:, :gd]
        gi_ref[dst, :] = z[:, gd:]
        return carry

    lax.fori_loop(0, n1, chan_body, 0, unroll=min(FNET_UNROLL, n1))

    m1 = m1_ref[...].astype(BF16)

    def stage1_body(s2, carry):
        idx = pl.ds(s2, n1, stride=pitch)
        x = jnp.concatenate([gr_ref[idx, :], gi_ref[idx, :]], axis=0)
        z = jnp.dot(m1, x.astype(BF16), preferred_element_type=F32)
        zr_ref[idx, :] = z[:n1]
        zi_ref[idx, :] = z[n1:]
        return carry

    lax.fori_loop(0, n2, stage1_body, 0, unroll=2 * FNET_UNROLL)

    def stage2_body(s1, carry):
        src = pl.ds(pl.multiple_of(s1 * pitch, V7X_SUBLANES), n2)
        zz = jnp.concatenate([zr_ref[src, :], zi_ref[src, :]], axis=0)
        gr_ref[src, :] = jnp.dot(m2_ref[s1].astype(BF16), zz.astype(BF16),
                                 preferred_element_type=F32)
        return carry

    lax.fori_loop(0, n1, stage2_body, 0, unroll=min(FNET_UNROLL, n1))

    def reorder_body(s2, carry):
        dst = pl.ds(pl.multiple_of(s2 * n1, V7X_SUBLANES), n1)
        o_ref[dst, :] = gr_ref[pl.ds(s2, n1, stride=pitch), :]
        return carry

    lax.fori_loop(0, n2, reorder_body, 0, unroll=2 * FNET_UNROLL)


def _fnet(p3, col0, w_f):
    bsz, seq, _ = p3.shape
    ng, gd, _ = w_f.shape
    n2 = FNET_N2
    n1 = seq // n2
    scale = 1.0 / math.sqrt(seq * gd)
    ar = np.arange
    cd, sd = _cos_sin(ar(gd)[:, None] * ar(gd)[None, :], gd)
    c1, s1 = _cos_sin(ar(n1)[:, None] * ar(n1)[None, :], n1)
    m1 = np.block([[c1, s1], [-s1, c1]])
    c2, s2 = _cos_sin(ar(n2)[None, None, :] * (ar(n1)[:, None, None] + n1 * ar(n2)[None, :, None]), seq)
    m1 = jnp.asarray(m1, F32)
    m2 = jnp.asarray(np.concatenate([c2, s2], axis=2), F32)

    wcat = pl.pallas_call(
        functools.partial(_fnet_weight_kernel, scale),
        out_shape=jax.ShapeDtypeStruct((ng, gd, 2 * gd), F32),
        name="fnet_weights",
    )(jnp.asarray(cd, F32), jnp.asarray(sd, F32), w_f)

    pitch_rows = n1 * (n2 + FNET_PITCH_PAD)
    const2 = lambda a: pl.BlockSpec(a.shape, lambda b, g: (0, 0))
    const3 = lambda a: pl.BlockSpec(a.shape, lambda b, g: (0, 0, 0), pipeline_mode=pl.Buffered(1))
    return pl.pallas_call(
        functools.partial(_fnet_kernel, n1, n2),
        grid=(bsz, ng),
        in_specs=[pl.BlockSpec((None, seq, gd), lambda b, g: (b, 0, col0 // gd + g)),
                  pl.BlockSpec((None, gd, 2 * gd), lambda b, g: (g, 0, 0)),
                  const2(m1), const3(m2)],
        out_specs=pl.BlockSpec((None, seq, gd), lambda b, g: (b, 0, g)),
        out_shape=jax.ShapeDtypeStruct((bsz, seq, ng * gd), F32),
        scratch_shapes=[pltpu.VMEM((pitch_rows, gd), F32)] * 4,
        compiler_params=_cparams(("parallel", "parallel")),
        name="fnet",
    )(p3, wcat, m1, m2)


def _odd_out_kernel(y0_ref, y1_ref, b0_ref, b1_ref, zc_ref, fn_ref, zd_ref, h_ref, lng_ref,
                    lnb_ref, bd_ref, wo_ref, fg_ref, o_ref):
    c = y0_ref.shape[1]
    bd = bd_ref[...]
    inv_n = 1.0 / HEAD_DIM
    ysum = y0_ref[...].astype(F32) + y1_ref[...].astype(F32)
    mean = _segsum(ysum, bd, 2) * inv_n
    cen = ysum - mean
    var = _segsum(cen * cen, bd, 1) * inv_n
    gn = cen * lax.rsqrt(var + GN_EPS) * lng_ref[...] + lnb_ref[...]
    yc = (gn + b0_ref[...].astype(F32) + b1_ref[...].astype(F32)) * _silu(zc_ref[...])
    yd = fn_ref[...] * _silu(zd_ref[...])
    out = h_ref[...]
    out = out + jnp.dot(yc.astype(BF16), wo_ref[0:c, :], preferred_element_type=F32)
    out = out + jnp.dot(yd.astype(BF16), wo_ref[c:, :], preferred_element_type=F32)
    ms = jnp.mean(out * out, axis=-1, keepdims=True)
    o_ref[...] = out * lax.rsqrt(ms + RMS_EPS) * fg_ref[...]


def _odd_out(y0, y1, b0, b1, p2d, zc_blk, fn, zd_blk, h2d, lnx_g, lnx_b, w_out_bf16, final_g, tm):
    t, d = h2d.shape
    c = y0.shape[1]
    cf = fn.shape[1]
    hd = np.arange(2 * HEAD_DIM) // HEAD_DIM
    bd = jnp.asarray(hd[:, None] == hd[None, :], BF16)
    rowblk = lambda w: pl.BlockSpec((tm, w), lambda i: (i, 0))
    full = lambda shape: pl.BlockSpec(shape, lambda i: (0, 0))
    return pl.pallas_call(
        _odd_out_kernel,
        grid=(t // tm,),
        in_specs=[rowblk(c), rowblk(c), rowblk(c), rowblk(c),
                  pl.BlockSpec((tm, c), lambda i: (i, zc_blk)),
                  rowblk(cf),
                  pl.BlockSpec((tm, cf), lambda i: (i, zd_blk)),
                  rowblk(d), full((1, c)), full((1, c)), full(bd.shape),
                  full(w_out_bf16.shape), full((1, d))],
        out_specs=rowblk(d),
        out_shape=jax.ShapeDtypeStruct((t, d), F32),
        compiler_params=_cparams(("parallel",)),
        name="odd_out",
    )(y0, y1, b0, b1, p2d, fn, p2d, h2d, lnx_g.reshape(1, c), lnx_b.reshape(1, c), bd,
      w_out_bf16, final_g.reshape(1, d))


WKV_LANES = 128
WKV_ROWS = 512
ODD_PROJ_ROWS = 512
ODD_OUT_ROWS = 512


def _odd_layer(h2d, bsz, seq, norm_g, w_in, mu, w0, w2, a0, a2, k_k, k_a, r_k, lnx_g, lnx_b,
               fnet_w, w_out, final_g):
    t, d = h2d.shape
    c = w0.shape[-1]
    rs = 3 * c + 2 * LORA
    cf = fnet_w.shape[0] * fnet_w.shape[1]
    gw = WKV_LANES
    rkv_cols = [w_in[:, j * c + g * gw:j * c + (g + 1) * gw] for g in range(c // gw) for j in range(3)]
    w_perm = jnp.concatenate(rkv_cols + [w_in[:, rs:], w_in[:, 3 * c:rs]], axis=1).astype(BF16)
    p2 = _norm_proj(h2d, norm_g, w_perm, tm=min(ODD_PROJ_ROWS, t))
    zc_col, fd_col, zd_col, wa_col = 3 * c, 4 * c, 4 * c + cf, 4 * c + 2 * cf
    p3 = p2.reshape(bsz, seq, -1)
    lora = _lora_weights(w2, a2, gw)
    rws = min(WKV_ROWS, seq)
    y0, b0, y1, b1 = _wkv(p3, wa_col, mu, w0, a0, lora, k_k, k_a, r_k.reshape(-1), gw, rws)
    fn = _fnet(p3, fd_col, fnet_w).reshape(t, cf)
    flat = lambda a: a.reshape(t, c)
    return _odd_out(flat(y0), flat(y1), flat(b0), flat(b1), p2, zc_col // c, fn, zd_col // cf, h2d,
                    lnx_g, lnx_b, w_out.astype(BF16), final_g, tm=min(ODD_OUT_ROWS, t))


def kernel(x, e_norm_g, e_w_in, e_conv_w, e_sgu_ln_g, e_sgu_ln_b, e_sgu_w, e_sgu_b, e_w_out,
           o_norm_g, o_w_in, o_mu, o_w0, o_w2, o_a0, o_a2, o_k_k, o_k_a, o_r_k, o_lnx_g, o_lnx_b,
           o_fnet_w, o_w_out, final_norm_g):
    bsz, seq, d = x.shape
    assert e_norm_g.shape[0] == 1 and o_norm_g.shape[0] == 1, "two-layer trunk: one even, one odd layer"
    h = x.reshape(bsz * seq, d)
    h = _even_layer(h, seq, e_norm_g[0], e_w_in[0], e_conv_w[0], e_sgu_ln_g[0], e_sgu_ln_b[0],
                    e_sgu_w[0], e_sgu_b[0], e_w_out[0])
    out = _odd_layer(h, bsz, seq, o_norm_g[0], o_w_in[0], o_mu[0], o_w0[0], o_w2[0], o_a0[0],
                     o_a2[0], o_k_k[0], o_k_a[0], o_r_k[0], o_lnx_g[0], o_lnx_b[0], o_fnet_w[0],
                     o_w_out[0], final_norm_g)
    return out.reshape(bsz, seq, d)
```
